```python
import jax, jax.numpy as jnp
from jax import lax
import numpy as np

D_MODEL = 1024
BATCH = 1
SEQ = 16384
DEPTH = 2

N_EVEN = (DEPTH + 1) // 2
N_ODD = DEPTH // 2

A_WIDTH = D_MODEL // 2
A_GROUP = 128
A_GROUPS = A_WIDTH // A_GROUP
A_CHUNK = 128
LN_EPS = 1e-5
B_WIDTH = D_MODEL // 2
B_HEAD = 64
B_HEADS = B_WIDTH // B_HEAD
LORA_W = 64
LORA_A = 64
LORA_G = 128
GN_EPS = 64e-5
SHIFT_W = 3 * B_WIDTH + LORA_W + LORA_A + LORA_G
EVEN_IN = 2 * A_WIDTH + SHIFT_W
C_HEAD = 64
C_HEADS = D_MODEL // C_HEAD
C_PATTERNS = ((128, 1), (512, 4), (2048, 16))
Q_BLOCK = 128
NEG_INF = -1e30
D_FF = 2816
CONV_WIDTH = 3
RMS_EPS = 1e-6

kernel_name = "hybrid_gmlp_rwkv7_dilated_attn_convffn"


def rmsnorm(x, g):
    xf = x.astype(jnp.float32)
    y = xf * lax.rsqrt(jnp.mean(xf * xf, axis=-1, keepdims=True) + RMS_EPS)
    return (y * g.astype(jnp.float32)).astype(x.dtype)


def shift_prev(h):
    return jnp.pad(h, ((0, 0), (1, 0), (0, 0)))[:, :-1]


def chunked_spatial_gating(u, v, ln_g, ln_b, w_s, b_s):
    bsz, t_len, _ = v.shape
    vf = v.astype(jnp.float32)
    mean = jnp.mean(vf, axis=-1, keepdims=True)
    var = jnp.mean(jnp.square(vf - mean), axis=-1, keepdims=True)
    vn = (vf - mean) * lax.rsqrt(var + LN_EPS) * ln_g.astype(jnp.float32) + ln_b.astype(jnp.float32)
    vc = vn.reshape(bsz, t_len // A_CHUNK, A_CHUNK, A_GROUPS, A_GROUP)
    causal = jnp.tril(jnp.ones((A_CHUNK, A_CHUNK), jnp.float32))
    mixed = jnp.einsum('gts,bnsgc->bntgc', w_s.astype(jnp.float32) * causal, vc)
    mixed = mixed + b_s.astype(jnp.float32).T[None, None, :, :, None]
    return u * mixed.reshape(bsz, t_len, A_WIDTH).astype(u.dtype)


def rwkv7_time_mix(p, mu, w0, w2, a0, a2, g2, k_k, k_a, r_k, gn_g, gn_b):
    bsz, t_len, _ = p.shape
    p = p + (shift_prev(p) - p) * mu
    cuts = np.cumsum([B_WIDTH, B_WIDTH, B_WIDTH, LORA_W, LORA_A])
    r, k, v, lw, la, lg = jnp.split(p, [int(c) for c in cuts], axis=-1)
    w = -jax.nn.softplus(-(w0 + jnp.tanh(lw) @ w2)) - 0.5
    decay = jnp.exp(-jnp.exp(w.astype(jnp.float32)))
    a = jax.nn.sigmoid(a0 + la @ a2)
    g = jax.nn.sigmoid(lg) @ g2

    def heads(z):
        return z.reshape(bsz, t_len, B_HEADS, B_HEAD).astype(jnp.float32)

    kk = heads(k * k_k)
    kk = kk * lax.rsqrt(jnp.maximum(jnp.sum(kk * kk, -1, keepdims=True), 1e-24))
    k = k * (1.0 + (a - 1.0) * k_a)
    rh, kh, vh, ah, wh = heads(r), heads(k), heads(v), heads(a), decay.reshape(bsz, t_len, B_HEADS, B_HEAD)

    def step(S, inp):
        r_t, w_t, k_t, v_t, kk_t, a_t = inp
        sa = jnp.einsum('bhvk,bhk->bhv', S, -kk_t)
        S = (S * w_t[:, :, None, :] + sa[..., None] * (kk_t * a_t)[:, :, None, :]
             + v_t[..., None] * k_t[:, :, None, :])
        return S, jnp.einsum('bhvk,bhk->bhv', S, r_t)

    tm = lambda z: jnp.swapaxes(z, 0, 1)
    S0 = jnp.zeros((bsz, B_HEADS, B_HEAD, B_HEAD), jnp.float32)
    _, o = lax.scan(step, S0, (tm(rh), tm(wh), tm(kh), tm(vh), tm(kk), tm(ah)))
    o = tm(o)
    mean = jnp.mean(o, -1, keepdims=True)
    var = jnp.mean(jnp.square(o - mean), -1, keepdims=True)
    o = ((o - mean) * lax.rsqrt(var + GN_EPS) * gn_g.reshape(B_HEADS, B_HEAD).astype(jnp.float32)
         + gn_b.reshape(B_HEADS, B_HEAD).astype(jnp.float32))
    bonus = jnp.sum(rh * kh * r_k.astype(jnp.float32), -1, keepdims=True) * vh
    out = (o + bonus).reshape(bsz, t_len, B_WIDTH).astype(p.dtype)
    return out * g


def dilated_attention(q, k, v):
    bsz, t_len = q.shape[:2]
    qf = q.astype(jnp.float32) * (C_HEAD ** -0.5)
    kf = k.astype(jnp.float32)
    vf = v.astype(jnp.float32)

    def block(t0):
        qb = lax.dynamic_slice_in_dim(qf, t0, Q_BLOCK, axis=1)
        i = t0 + jnp.arange(Q_BLOCK)
        outs, lses = [], []
        for window, dil in C_PATTERNS:
            j = jnp.arange(window // dil + 1)
            pos = i[:, None] - dil * j[None, :]
            valid = pos >= 0
            idx = jnp.maximum(pos, 0)
            kg = jnp.take(kf, idx, axis=1)
            vg = jnp.take(vf, idx, axis=1)
            s = jnp.einsum('bihd,bijhd->bhij', qb, kg)
            s = jnp.where(valid[None, None], s, NEG_INF)
            lse = jax.nn.logsumexp(s, axis=-1)
            pr = jnp.exp(s - lse[..., None])
            outs.append(jnp.einsum('bhij,bijhd->bihd', pr, vg))
            lses.append(lse)
        wts = jax.nn.softmax(jnp.stack(lses), axis=0)
        wts = jnp.transpose(wts, (0, 1, 3, 2))[..., None]
        return jnp.sum(wts * jnp.stack(outs), axis=0)

    starts = jnp.arange(t_len // Q_BLOCK) * Q_BLOCK
    ob = lax.map(block, starts)
    out = jnp.swapaxes(ob, 0, 1).reshape(bsz, t_len, C_HEADS * C_HEAD)
    return out.astype(q.dtype)


def conv_glu_ffn(h, w_up, conv_w, conv_b, w_down):
    z = h @ w_up
    z1 = shift_prev(z)
    z2 = shift_prev(z1)
    z = conv_w[0] * z2 + conv_w[1] * z1 + conv_w[2] * z + conv_b
    gate, val = jnp.split(z, 2, axis=-1)
    return (jax.nn.silu(gate) * val) @ w_down


def setup_inputs(seed: int = 0) -> dict:
    key = jax.random.key(seed)
    cnt = [0]

    def nk():
        cnt[0] += 1
        return jax.random.fold_in(key, cnt[0])

    def nrm(shape, scale):
        return scale * jax.random.normal(nk(), shape, jnp.float32)

    def gain(shape):
        return 1.0 + 0.02 * jax.random.normal(nk(), shape, jnp.float32)

    def unif(shape, lo, hi):
        return jax.random.uniform(nk(), shape, jnp.float32, lo, hi)

    D, E, O = D_MODEL, N_EVEN, N_ODD
    return {
        "x": nrm((BATCH, SEQ, D), 1.0),
        "ev_norm": gain((E, D)),
        "ev_w_in": nrm((E, D, EVEN_IN), D ** -0.5),
        "ev_ln_g": gain((E, A_WIDTH)),
        "ev_ln_b": nrm((E, A_WIDTH), 0.02),
        "ev_w_s": nrm((E, A_GROUPS, A_CHUNK, A_CHUNK), A_CHUNK ** -0.5),
        "ev_b_s": gain((E, A_GROUPS, A_CHUNK)),
        "ev_mu": unif((E, SHIFT_W), 0.0, 1.0),
        "ev_w0": unif((E, B_WIDTH), -6.5, -1.5),
        "ev_w2": nrm((E, LORA_W, B_WIDTH), 0.5 * LORA_W ** -0.5),
        "ev_a0": nrm((E, B_WIDTH), 0.1),
        "ev_a2": nrm((E, LORA_A, B_WIDTH), 0.5 * LORA_A ** -0.5),
        "ev_g2": nrm((E, LORA_G, B_WIDTH), LORA_G ** -0.5),
        "ev_k_k": 0.85 + nrm((E, B_WIDTH), 0.02),
        "ev_k_a": gain((E, B_WIDTH)),
        "ev_r_k": nrm((E, B_HEADS, B_HEAD), 0.1),
        "ev_gn_g": gain((E, B_WIDTH)),
        "ev_gn_b": nrm((E, B_WIDTH), 0.02),
        "ev_w_out": nrm((E, D, D), D ** -0.5),
        "od_norm": gain((O, D)),
        "od_w_qkv": nrm((O, D, 3 * D), D ** -0.5),
        "od_w_out": nrm((O, D, D), D ** -0.5),
        "ff_norm": gain((DEPTH, D)),
        "ff_w_up": nrm((DEPTH, D, 2 * D_FF), D ** -0.5),
        "ff_conv_w": nrm((DEPTH, CONV_WIDTH, 2 * D_FF), CONV_WIDTH ** -0.5),
        "ff_conv_b": nrm((DEPTH, 2 * D_FF), 0.02),
        "ff_w_down": nrm((DEPTH, D_FF, D), D_FF ** -0.5),
        "final_norm": gain((D,)),
    }


def reference(x, ev_norm, ev_w_in, ev_ln_g, ev_ln_b, ev_w_s, ev_b_s, ev_mu, ev_w0, ev_w2,
              ev_a0, ev_a2, ev_g2, ev_k_k, ev_k_a, ev_r_k, ev_gn_g, ev_gn_b, ev_w_out,
              od_norm, od_w_qkv, od_w_out, ff_norm, ff_w_up, ff_conv_w, ff_conv_b, ff_w_down,
              final_norm):
    bsz, t_len, _ = x.shape
    for layer in range(DEPTH):
        j = layer // 2
        if layer % 2 == 0:
            h = rmsnorm(x, ev_norm[j])
            p = h @ ev_w_in[j]
            y_a = chunked_spatial_gating(p[..., :A_WIDTH], p[..., A_WIDTH:2 * A_WIDTH],
                                         ev_ln_g[j], ev_ln_b[j], ev_w_s[j], ev_b_s[j])
            y_b = rwkv7_time_mix(p[..., 2 * A_WIDTH:], ev_mu[j], ev_w0[j], ev_w2[j], ev_a0[j],
                                 ev_a2[j], ev_g2[j], ev_k_k[j], ev_k_a[j], ev_r_k[j],
                                 ev_gn_g[j], ev_gn_b[j])
            x = x + jnp.concatenate([y_a, y_b], axis=-1) @ ev_w_out[j]
        else:
            h = rmsnorm(x, od_norm[j])
            qkv = (h @ od_w_qkv[j]).reshape(bsz, t_len, 3, C_HEADS, C_HEAD)
            y = dilated_attention(qkv[:, :, 0], qkv[:, :, 1], qkv[:, :, 2])
            x = x + y @ od_w_out[j]
        x = x + conv_glu_ffn(rmsnorm(x, ff_norm[layer]), ff_w_up[layer], ff_conv_w[layer],
                             ff_conv_b[layer], ff_w_down[layer])
    return rmsnorm(x, final_norm)
```

```python
import functools

import jax
import jax.numpy as jnp
from jax import lax
from jax.experimental import pallas as pl
from jax.experimental.pallas import tpu as pltpu

F32 = jnp.float32
BF16 = jnp.bfloat16

D_MODEL = 1024
A_WIDTH = 512
A_GROUP = 128
A_GROUPS = 4
A_CHUNK = 128
LN_EPS = 1e-5
B_WIDTH = 512
B_HEAD = 64
LORA_W = 64
LORA_A = 64
LORA_G = 128
GN_EPS = 64e-5
SHIFT_W = 3 * B_WIDTH + LORA_W + LORA_A + LORA_G
C_HEAD = 64
C_HEADS = 16
C_PATTERNS = ((128, 1), (512, 4), (2048, 16))
NEG_INF = -1e30
D_FF = 2816
RMS_EPS = 1e-6

LANES = 128
SUBLANES = 8
HEADS_PER_BLOCK = LANES // B_HEAD
VMEM_LIMIT_BYTES = 52 * 1024 * 1024

ROW_TILE = 512
FFN_ROW_TILE = 1024
FFN_COL_TILE = 256
PREP_ROW_TILE = 256
RWKV_CHUNK = 64
ATTN_BLOCK = 128
NEUMANN_LEVELS = 5


def _params(n_axes):
    return pltpu.CompilerParams(dimension_semantics=("arbitrary",) * n_axes,
                                vmem_limit_bytes=VMEM_LIMIT_BYTES)


def _rmsnorm(x, g):
    return x * lax.rsqrt(jnp.mean(x * x, axis=-1, keepdims=True) + RMS_EPS) * g


def _dot(a, b):
    return jnp.dot(a, b, preferred_element_type=F32)


def _dot_nt(a, b):
    return lax.dot_general(a, b, (((1,), (1,)), ((), ())), preferred_element_type=F32)


def _split2(x):
    hi = x.astype(BF16)
    lo = (x - hi.astype(F32)).astype(BF16)
    return hi, lo


def _split3(x):
    hi = x.astype(BF16)
    r1 = x - hi.astype(F32)
    mid = r1.astype(BF16)
    lo = (r1 - mid.astype(F32)).astype(BF16)
    return hi, mid, lo


def _mm3(a, b):
    ah, al = _split2(a)
    bh, bl = _split2(b)
    return _dot(ah, bh) + (_dot(al, bh) + _dot(ah, bl))


def _mm3_nt(a, b):
    ah, al = _split2(a)
    bh, bl = _split2(b)
    return _dot_nt(ah, bh) + (_dot_nt(al, bh) + _dot_nt(ah, bl))


def _exact_rhs(x, m_bf16):
    hi, mid, lo = _split3(x)
    return _dot(hi, m_bf16) + (_dot(mid, m_bf16) + _dot(lo, m_bf16))


def _exact_lhs(m_bf16, x):
    hi, mid, lo = _split3(x)
    return _dot(m_bf16, hi) + (_dot(m_bf16, mid) + _dot(m_bf16, lo))


def _norm_matmul_kernel(x_ref, g_ref, w_ref, o_ref):
    h = _rmsnorm(x_ref[...], g_ref[...]).astype(BF16)
    o_ref[...] = _dot(h, w_ref[...]).astype(o_ref.dtype)


def norm_matmul(x, g, w, out_dtype):
    t_len, d = x.shape
    n = w.shape[1]
    tm = min(ROW_TILE, t_len)
    return pl.pallas_call(
        _norm_matmul_kernel,
        grid=(t_len // tm,),
        in_specs=[pl.BlockSpec((tm, d), lambda i: (i, 0)),
                  pl.BlockSpec((1, d), lambda i: (0, 0)),
                  pl.BlockSpec((d, n), lambda i: (0, 0))],
        out_specs=pl.BlockSpec((tm, n), lambda i: (i, 0)),
        out_shape=jax.ShapeDtypeStruct((t_len, n), out_dtype),
        compiler_params=_params(1),
        name="norm_matmul",
    )(x, g.reshape(1, d), w)


def _gmlp_kernel(p_ref, lng_ref, lnb_ref, ws_ref, bs_ref, o_ref, *, n_chunks):
    row = lax.broadcasted_iota(jnp.int32, (A_CHUNK, A_CHUNK), 0)
    col = lax.broadcasted_iota(jnp.int32, (A_CHUNK, A_CHUNK), 1)
    causal = col <= row
    w_masked = [jnp.where(causal, ws_ref[g], 0.0).astype(BF16) for g in range(A_GROUPS)]
    for c in range(n_chunks):
        rows = slice(c * A_CHUNK, (c + 1) * A_CHUNK)
        u = p_ref[rows, :A_WIDTH]
        v = p_ref[rows, A_WIDTH:]
        mean = jnp.mean(v, axis=-1, keepdims=True)
        cen = v - mean
        var = jnp.mean(cen * cen, axis=-1, keepdims=True)
        vn = (cen * lax.rsqrt(var + LN_EPS) * lng_ref[...] + lnb_ref[...]).astype(BF16)
        for g in range(A_GROUPS):
            cols = slice(g * A_GROUP, (g + 1) * A_GROUP)
            mixed = _dot(w_masked[g], vn[:, cols]) + bs_ref[:, g:g + 1]
            o_ref[rows, cols] = (u[:, cols] * mixed).astype(o_ref.dtype)


def gmlp(p_a, ln_g, ln_b, w_s, b_s):
    t_len = p_a.shape[0]
    tm = min(ROW_TILE, t_len)
    return pl.pallas_call(
        functools.partial(_gmlp_kernel, n_chunks=tm // A_CHUNK),
        grid=(t_len // tm,),
        in_specs=[pl.BlockSpec((tm, 2 * A_WIDTH), lambda i: (i, 0)),
                  pl.BlockSpec((1, A_WIDTH), lambda i: (0, 0)),
                  pl.BlockSpec((1, A_WIDTH), lambda i: (0, 0)),
                  pl.BlockSpec((A_GROUPS, A_CHUNK, A_CHUNK), lambda i: (0, 0, 0)),
                  pl.BlockSpec((A_CHUNK, A_GROUPS), lambda i: (0, 0))],
        out_specs=pl.BlockSpec((tm, A_WIDTH), lambda i: (i, 0)),
        out_shape=jax.ShapeDtypeStruct((t_len, A_WIDTH), BF16),
        compiler_params=_params(1),
        name="gmlp",
    )(p_a, ln_g.reshape(1, A_WIDTH), ln_b.reshape(1, A_WIDTH), w_s, b_s.T)


def _rwkv_prep_kernel(p_ref, mu_ref, w0_ref, wa_ref, a0_ref, g2_ref, kkw_ref, kaw_ref, seg_ref,
                      r_o, k_o, v_o, kk_o, b_o, lw_o, g_o, carry_ref):
    tm = p_ref.shape[0]

    @pl.when(pl.program_id(0) == 0)
    def _():
        carry_ref[...] = jnp.zeros_like(carry_ref)

    p = p_ref[...]
    rowid = lax.broadcasted_iota(jnp.int32, (tm, 1), 0)
    prev = jnp.where(rowid == 0, carry_ref[0:1, :], pltpu.roll(p, 1, 0))
    carry_ref[0:1, :] = p[tm - 1:tm, :]
    ps = p + (prev - p) * mu_ref[...]

    r = ps[:, :B_WIDTH]
    k = ps[:, B_WIDTH:2 * B_WIDTH]
    v = ps[:, 2 * B_WIDTH:3 * B_WIDTH]
    lwa = ps[:, 3 * B_WIDTH:3 * B_WIDTH + LORA_W + LORA_A]
    lg = ps[:, 3 * B_WIDTH + LORA_W + LORA_A:]

    lane = lax.broadcasted_iota(jnp.int32, (1, LORA_W + LORA_A), 1)
    lora_in = jnp.where(lane < LORA_W, jnp.tanh(lwa), lwa).astype(BF16)
    lora = _dot(lora_in, wa_ref[...])
    neg = -(w0_ref[...] + lora[:, :B_WIDTH])
    softplus = jnp.maximum(neg, 0.0) + jnp.log1p(jnp.exp(-jnp.abs(neg)))
    w = -softplus - 0.5
    a = jax.nn.sigmoid(a0_ref[...] + lora[:, B_WIDTH:])
    g = _dot(jax.nn.sigmoid(lg).astype(BF16), g2_ref[...])

    kkr = k * kkw_ref[...]
    ss = _exact_rhs(kkr * kkr, seg_ref[...])
    kk = kkr * lax.rsqrt(jnp.maximum(ss, 1e-24))

    r_o[...] = r
    k_o[...] = k * (1.0 + (a - 1.0) * kaw_ref[...])
    v_o[...] = v
    kk_o[...] = kk
    b_o[...] = kk * a
    lw_o[...] = -jnp.exp(w)
    g_o[...] = g


def rwkv_prep(p_b, mu, w0, w2, a0, a2, g2, k_k, k_a):
    t_len = p_b.shape[0]
    tm = min(PREP_ROW_TILE, t_len)
    zeros = jnp.zeros((LORA_W, B_WIDTH), F32)
    wa = jnp.concatenate([jnp.concatenate([w2, zeros], 1),
                          jnp.concatenate([zeros, a2], 1)], 0).astype(BF16)
    head = jnp.arange(B_WIDTH) // B_HEAD
    seg = (head[:, None] == head[None, :]).astype(BF16)
    row = lambda z: z.reshape(1, -1)
    full = lambda shape: pl.BlockSpec(shape, lambda i: (0,) * len(shape))
    out_sds = jax.ShapeDtypeStruct((t_len, B_WIDTH), F32)
    out_spec = pl.BlockSpec((tm, B_WIDTH), lambda i: (i, 0))
    return pl.pallas_call(
        _rwkv_prep_kernel,
        grid=(t_len // tm,),
        in_specs=[pl.BlockSpec((tm, SHIFT_W), lambda i: (i, 0)),
                  full((1, SHIFT_W)), full((1, B_WIDTH)), full((LORA_W + LORA_A, 2 * B_WIDTH)),
                  full((1, B_WIDTH)), full((LORA_G, B_WIDTH)), full((1, B_WIDTH)),
                  full((1, B_WIDTH)), full((B_WIDTH, B_WIDTH))],
        out_specs=[out_spec] * 7,
        out_shape=[out_sds] * 7,
        scratch_shapes=[pltpu.VMEM((SUBLANES, SHIFT_W), F32)],
        compiler_params=_params(1),
        name="rwkv_prep",
    )(p_b, row(mu), row(w0), wa, row(a0), g2.astype(BF16), row(k_k), row(k_a), seg)


def _rwkv_scan_kernel(r_ref, k_ref, v_ref, kk_ref, b_ref, lw_ref, g_ref, rk_ref, gng_ref, gnb_ref,
                      o_ref, h_ref):
    c_len = RWKV_CHUNK
    n_blocks = B_WIDTH // LANES

    @pl.when(pl.program_id(0) == 0)
    def _():
        h_ref[...] = jnp.zeros_like(h_ref)

    lane = lax.broadcasted_iota(jnp.int32, (1, LANES), 1)
    head0 = lane < B_HEAD
    rho = lax.broadcasted_iota(jnp.int32, (LANES, LANES), 0)
    sig = lax.broadcasted_iota(jnp.int32, (LANES, LANES), 1)
    same_head = (rho // c_len) == (sig // c_len)
    strict = same_head & ((sig % c_len) < (rho % c_len))
    incl = same_head & ((sig % c_len) <= (rho % c_len))
    eye = rho == sig
    seg = ((rho // B_HEAD) == (sig // B_HEAD)).astype(BF16)
    tri_r = lax.broadcasted_iota(jnp.int32, (c_len, c_len), 0)
    tri_c = lax.broadcasted_iota(jnp.int32, (c_len, c_len), 1)
    cum_mat = (tri_c <= tri_r).astype(BF16)

    def stack(x):
        return jnp.concatenate([jnp.where(head0, x, 0.0), jnp.where(head0, 0.0, x)], axis=0)

    for blk in range(n_blocks):
        cols = slice(blk * LANES, (blk + 1) * LANES)
        r = r_ref[:, cols]
        k = k_ref[:, cols]
        v = v_ref[:, cols]
        kk = kk_ref[:, cols]
        b = b_ref[:, cols]
        lw = lw_ref[:, cols]

        cum = _exact_lhs(cum_mat, lw)
        total = cum[c_len - 1:c_len, :]
        r_t = r * jnp.exp(cum)
        kk_t = kk * jnp.exp(cum - lw)
        inv = jnp.exp(-cum)
        b_t = b * inv
        k_t = k * inv
        to_end = jnp.exp(total - cum)
        b_e = b * to_end
        k_e = k * to_end

        kk_s = stack(kk_t)
        r_s = stack(r_t)
        v_s = stack(v)
        a_all = _mm3_nt(jnp.concatenate([kk_s, r_s], axis=0),
                        jnp.concatenate([stack(b_t), stack(k_t)], axis=0))
        a_bb = jnp.where(strict, a_all[:LANES, :LANES], 0.0)
        a_bk = jnp.where(strict, a_all[:LANES, LANES:], 0.0)
        a_rb = jnp.where(incl, a_all[LANES:, :LANES], 0.0)
        a_rk = jnp.where(incl, a_all[LANES:, LANES:], 0.0)

        neg_l = -a_bb
        t_inv = jnp.where(eye, 1.0, 0.0) + neg_l
        power = neg_l
        for _ in range(NEUMANN_LEVELS):
            power = _mm3(power, power)
            t_inv = t_inv + _mm3(t_inv, power)

        akv = _mm3(a_bk, v_s)
        w12 = _mm3(t_inv, jnp.concatenate([kk_s, akv], axis=1))
        bw = _mm3(stack(b_e).T, w12)
        kv = _mm3(stack(k_e).T, v_s)
        p_mat = jnp.where(eye, jnp.exp(total), 0.0) - bw[:, :LANES]
        q_mat = kv - bw[:, LANES:]
        aw = _mm3(a_rb, w12)
        o1 = r_s - aw[:, :LANES]
        o2 = _mm3(a_rk, v_s) - aw[:, LANES:]

        h_old = h_ref[blk]
        po = _mm3(jnp.concatenate([p_mat, o1], axis=0), h_old)
        h_ref[blk] = po[:LANES] + q_mat
        o_st = po[LANES:] + o2
        o = o_st[:c_len] + o_st[c_len:]

        mean = _exact_rhs(o, seg) * (1.0 / B_HEAD)
        cen = o - mean
        var = _exact_rhs(cen * cen, seg) * (1.0 / B_HEAD)
        normed = cen * lax.rsqrt(var + GN_EPS) * gng_ref[:, cols] + gnb_ref[:, cols]
        bonus = _exact_rhs(r * k * rk_ref[:, cols], seg) * v
        o_ref[:, cols] = ((normed + bonus) * g_ref[:, cols]).astype(o_ref.dtype)


def rwkv_scan(r, k, v, kk, b, lw, g, r_k, gn_g, gn_b):
    t_len = r.shape[0]
    c_len = RWKV_CHUNK
    row = lambda z: z.reshape(1, B_WIDTH)
    tile = pl.BlockSpec((c_len, B_WIDTH), lambda i: (i, 0))
    vec = pl.BlockSpec((1, B_WIDTH), lambda i: (0, 0))
    return pl.pallas_call(
        _rwkv_scan_kernel,
        grid=(t_len // c_len,),
        in_specs=[tile] * 7 + [vec] * 3,
        out_specs=tile,
        out_shape=jax.ShapeDtypeStruct((t_len, B_WIDTH), BF16),
        scratch_shapes=[pltpu.VMEM((B_WIDTH // LANES, LANES, LANES), F32)],
        compiler_params=_params(1),
        name="rwkv_scan",
    )(r, k, v, kk, b, lw, g, row(r_k), row(gn_g), row(gn_b))


def _proj_residual_kernel(x_ref, ya_ref, yb_ref, wa_ref, wb_ref, o_ref):
    o_ref[...] = x_ref[...] + _dot(ya_ref[...], wa_ref[...]) + _dot(yb_ref[...], wb_ref[...])


def proj_residual(x, y_a, y_b, w_out):
    t_len, d = x.shape
    tm = min(ROW_TILE, t_len)
    w = w_out.astype(BF16)
    half = pl.BlockSpec((tm, A_WIDTH), lambda i: (i, 0))
    wspec = pl.BlockSpec((A_WIDTH, d), lambda i: (0, 0))
    return pl.pallas_call(
        _proj_residual_kernel,
        grid=(t_len // tm,),
        in_specs=[pl.BlockSpec((tm, d), lambda i: (i, 0)), half, half, wspec, wspec],
        out_specs=pl.BlockSpec((tm, d), lambda i: (i, 0)),
        out_shape=jax.ShapeDtypeStruct((t_len, d), F32),
        compiler_params=_params(1),
        name="proj_residual",
    )(x, y_a, y_b, w[:A_WIDTH], w[A_WIDTH:])


def _attn_kernel(q_ref, kp_ref, kc_ref, vp_ref, vc_ref, o_ref, lse_ref):
    blk = ATTN_BLOCK
    i = pl.program_id(0)
    row = lax.broadcasted_iota(jnp.int32, (blk, 2 * blk), 0)
    col = lax.broadcasted_iota(jnp.int32, (blk, 2 * blk), 1)
    dist = row + blk - col
    valid = (dist >= 0) & (dist <= blk) & ((col >= blk) | (i > 0))
    lane = lax.broadcasted_iota(jnp.int32, (1, LANES), 1)
    head0 = lane < C_HEAD
    lse_acc = jnp.zeros((blk, LANES), F32)
    for pair in range(C_HEADS // HEADS_PER_BLOCK):
        cols = slice(pair * LANES, (pair + 1) * LANES)
        q2 = q_ref[:, cols]
        k2 = jnp.concatenate([kp_ref[:, cols], kc_ref[:, cols]], axis=0)
        v2 = jnp.concatenate([vp_ref[:, cols], vc_ref[:, cols]], axis=0)
        outs = []
        for h in range(HEADS_PER_BLOCK):
            mask = head0 if h == 0 else jnp.logical_not(head0)
            qh = jnp.where(mask, q2, jnp.zeros_like(q2))
            s = _dot_nt(qh, k2) * (C_HEAD ** -0.5)
            s = jnp.where(valid, s, NEG_INF)
            mx = jnp.max(s, axis=-1, keepdims=True)
            e = jnp.exp(s - mx)
            den = jnp.sum(e, axis=-1, keepdims=True)
            outs.append(_dot(e.astype(BF16), v2) / den)
            lse = mx + jnp.log(den)
            lse_acc = lse_acc + jnp.where(lane == pair * HEADS_PER_BLOCK + h, lse, 0.0)
        o_ref[:, cols] = jnp.where(head0, outs[0], outs[1])
    lse_ref[...] = lse_acc


def dilated_attention_pattern(qkv, dil):
    t_len = qkv.shape[0]
    d = D_MODEL
    blk = ATTN_BLOCK
    sub = t_len // dil
    view = qkv.reshape(sub, dil * 3 * d)
    cur = lambda part: pl.BlockSpec((blk, d), lambda i, r: (i, 3 * r + part))
    prv = lambda part: pl.BlockSpec((blk, d), lambda i, r: (jnp.maximum(i - 1, 0), 3 * r + part))
    out, lse = pl.pallas_call(
        _attn_kernel,
        grid=(sub // blk, dil),
        in_specs=[cur(0), prv(1), cur(1), prv(2), cur(2)],
        out_specs=[pl.BlockSpec((blk, d), lambda i, r: (i, r)),
                   pl.BlockSpec((blk, LANES), lambda i, r: (i, r))],
        out_shape=[jax.ShapeDtypeStruct((sub, dil * d), F32),
                   jax.ShapeDtypeStruct((sub, dil * LANES), F32)],
        compiler_params=_params(2),
        name=f"dilated_attn_d{dil}",
    )(view, view, view, view, view)
    return out.reshape(t_len, d), lse.reshape(t_len, LANES)


def _attn_combine_proj_kernel(x_ref, o1_ref, o2_ref, o3_ref, l1_ref, l2_ref, l3_ref, ex_ref, w_ref,
                              o_ref):
    lses = [l1_ref[...], l2_ref[...], l3_ref[...]]
    mx = jnp.maximum(jnp.maximum(lses[0], lses[1]), lses[2])
    es = [jnp.exp(l - mx) for l in lses]
    den = es[0] + es[1] + es[2]
    y = jnp.zeros(o_ref.shape, F32)
    for e, o in zip(es, (o1_ref, o2_ref, o3_ref)):
        y = y + _exact_rhs(e / den, ex_ref[...]) * o[...]
    o_ref[...] = x_ref[...] + _dot(y.astype(BF16), w_ref[...])


def attn_combine_proj(x, outs, lses, w_out):
    t_len, d = x.shape
    tm = min(ROW_TILE, t_len)
    head_of_lane = jnp.arange(d) // C_HEAD
    expand = (jnp.arange(LANES)[:, None] == head_of_lane[None, :]).astype(BF16)
    wide = pl.BlockSpec((tm, d), lambda i: (i, 0))
    narrow = pl.BlockSpec((tm, LANES), lambda i: (i, 0))
    return pl.pallas_call(
        _attn_combine_proj_kernel,
        grid=(t_len // tm,),
        in_specs=[wide, wide, wide, wide, narrow, narrow, narrow,
                  pl.BlockSpec((LANES, d), lambda i: (0, 0)),
                  pl.BlockSpec((d, d), lambda i: (0, 0))],
        out_specs=wide,
        out_shape=jax.ShapeDtypeStruct((t_len, d), F32),
        compiler_params=_params(1),
        name="attn_combine_proj",
    )(x, *outs, *lses, expand, w_out.astype(BF16))


def _ffn_kernel(x_ref, g_ref, wg_ref, wv_ref, cwg_ref, cwv_ref, cbg_ref, cbv_ref, wd_ref, fn_ref,
                o_ref, h_ref, zg_ref, zv_ref, carry_g_ref, carry_v_ref, *, final_norm):
    i = pl.program_id(0)
    j = pl.program_id(1)
    tm = x_ref.shape[0]
    halo = SUBLANES

    @pl.when(j == 0)
    def _():
        x = x_ref[...]
        h_ref[...] = _rmsnorm(x, g_ref[...]).astype(BF16)
        o_ref[...] = x

    @pl.when(i == 0)
    def _():
        carry_g_ref[j] = jnp.zeros(carry_g_ref.shape[1:], F32)
        carry_v_ref[j] = jnp.zeros(carry_v_ref.shape[1:], F32)

    h = h_ref[...]

    def conv_branch(w_ref, cw_ref, cb_ref, z_ref, carry_ref):
        z = _dot(h, w_ref[...])
        z_ref[0:halo, :] = carry_ref[j]
        z_ref[halo:halo + tm, :] = z
        carry_ref[j] = z[tm - halo:tm, :]
        z1 = z_ref[halo - 1:halo - 1 + tm, :]
        z2 = z_ref[halo - 2:halo - 2 + tm, :]
        return cw_ref[0:1, :] * z2 + cw_ref[1:2, :] * z1 + cw_ref[2:3, :] * z + cb_ref[...]

    gate = conv_branch(wg_ref, cwg_ref, cbg_ref, zg_ref, carry_g_ref)
    val = conv_branch(wv_ref, cwv_ref, cbv_ref, zv_ref, carry_v_ref)
    act = (gate * jax.nn.sigmoid(gate) * val).astype(BF16)
    o_ref[...] += _dot(act, wd_ref[...])

    if final_norm:
        @pl.when(j == pl.num_programs(1) - 1)
        def _():
            o_ref[...] = _rmsnorm(o_ref[...], fn_ref[...])


def conv_glu_ffn(x, norm_g, w_up, conv_w, conv_b, w_down, final_g, final_norm):
    t_len, d = x.shape
    tm = min(FFN_ROW_TILE, t_len)
    tf = FFN_COL_TILE
    n_j = D_FF // tf
    w_up = w_up.astype(BF16)
    w_down = w_down.astype(BF16)
    conv_b = conv_b.reshape(1, 2 * D_FF)
    return pl.pallas_call(
        functools.partial(_ffn_kernel, final_norm=final_norm),
        grid=(t_len // tm, n_j),
        in_specs=[pl.BlockSpec((tm, d), lambda i, j: (i, 0)),
                  pl.BlockSpec((1, d), lambda i, j: (0, 0)),
                  pl.BlockSpec((d, tf), lambda i, j: (0, j)),
                  pl.BlockSpec((d, tf), lambda i, j: (0, n_j + j)),
                  pl.BlockSpec((3, tf), lambda i, j: (0, j)),
                  pl.BlockSpec((3, tf), lambda i, j: (0, n_j + j)),
                  pl.BlockSpec((1, tf), lambda i, j: (0, j)),
                  pl.BlockSpec((1, tf), lambda i, j: (0, n_j + j)),
                  pl.BlockSpec((tf, d), lambda i, j: (j, 0)),
                  pl.BlockSpec((1, d), lambda i, j: (0, 0))],
        out_specs=pl.BlockSpec((tm, d), lambda i, j: (i, 0)),
        out_shape=jax.ShapeDtypeStruct((t_len, d), F32),
        scratch_shapes=[pltpu.VMEM((tm, d), BF16),
                        pltpu.VMEM((tm + SUBLANES, tf), F32),
                        pltpu.VMEM((tm + SUBLANES, tf), F32),
                        pltpu.VMEM((n_j, SUBLANES, tf), F32),
                        pltpu.VMEM((n_j, SUBLANES, tf), F32)],
        compiler_params=_params(2),
        name="conv_glu_ffn",
    )(x, norm_g.reshape(1, d), w_up, w_up, conv_w, conv_w, conv_b, conv_b, w_down,
      final_g.reshape(1, d))


def even_mixer(x, norm, w_in, ln_g, ln_b, w_s, b_s, mu, w0, w2, a0, a2, g2, k_k, k_a, r_k, gn_g,
               gn_b, w_out):
    w_in = w_in.astype(BF16)
    p_a = norm_matmul(x, norm, w_in[:, :2 * A_WIDTH], F32)
    p_b = norm_matmul(x, norm, w_in[:, 2 * A_WIDTH:], F32)
    y_a = gmlp(p_a, ln_g, ln_b, w_s, b_s)
    r, k, v, kk, b, lw, g = rwkv_prep(p_b, mu, w0, w2, a0, a2, g2, k_k, k_a)
    y_b = rwkv_scan(r, k, v, kk, b, lw, g, r_k, gn_g, gn_b)
    return proj_residual(x, y_a, y_b, w_out)


def odd_mixer(x, norm, w_qkv, w_out):
    qkv = norm_matmul(x, norm, w_qkv.astype(BF16), BF16)
    outs, lses = [], []
    for _, dil in C_PATTERNS:
        o, l = dilated_attention_pattern(qkv, dil)
        outs.append(o)
        lses.append(l)
    return attn_combine_proj(x, outs, lses, w_out)


def kernel(x, ev_norm, ev_w_in, ev_ln_g, ev_ln_b, ev_w_s, ev_b_s, ev_mu, ev_w0, ev_w2, ev_a0, ev_a2, ev_g2, ev_k_k, ev_k_a, ev_r_k, ev_gn_g, ev_gn_b, ev_w_out, od_norm, od_w_qkv, od_w_out, ff_norm, ff_w_up, ff_conv_w, ff_conv_b, ff_w_down, final_norm):
    bsz, t_len, d = x.shape
    depth = ff_norm.shape[0]
    outs = []
    for bi in range(bsz):
        xb = x[bi]
        for layer in range(depth):
            j = layer // 2
            if layer % 2 == 0:
                xb = even_mixer(xb, ev_norm[j], ev_w_in[j], ev_ln_g[j], ev_ln_b[j], ev_w_s[j],
                                ev_b_s[j], ev_mu[j], ev_w0[j], ev_w2[j], ev_a0[j], ev_a2[j],
                                ev_g2[j], ev_k_k[j], ev_k_a[j], ev_r_k[j], ev_gn_g[j], ev_gn_b[j],
                                ev_w_out[j])
            else:
                xb = odd_mixer(xb, od_norm[j], od_w_qkv[j], od_w_out[j])
            xb = conv_glu_ffn(xb, ff_norm[layer], ff_w_up[layer], ff_conv_w[layer],
                              ff_conv_b[layer], ff_w_down[layer], final_norm,
                              final_norm=(layer == depth - 1))
        outs.append(xb)
    return jnp.stack(outs, axis=0)
```

```python
import functools

import jax
import jax.numpy as jnp
from jax import lax
from jax.experimental import pallas as pl
from jax.experimental.pallas import tpu as pltpu

F32 = jnp.float32
BF16 = jnp.bfloat16

D_MODEL = 1024
A_WIDTH = 512
A_GROUP = 128
A_GROUPS = 4
A_CHUNK = 128
LN_EPS = 1e-5
B_WIDTH = 512
B_HEAD = 64
LORA_W = 64
LORA_A = 64
LORA_G = 128
GN_EPS = 64e-5
SHIFT_W = 3 * B_WIDTH + LORA_W + LORA_A + LORA_G
C_HEAD = 64
C_HEADS = 16
C_PATTERNS = ((128, 1), (512, 4), (2048, 16))
NEG_INF = -1e30
D_FF = 2816
RMS_EPS = 1e-6

LANES = 128
SUBLANES = 8
HEADS_PER_BLOCK = LANES // B_HEAD
VMEM_LIMIT_BYTES = 52 * 1024 * 1024

ROW_TILE = 512
FFN_ROW_TILE = 1024
FFN_COL_TILE = 256
PREP_ROW_TILE = 256
RWKV_CHUNK = 64
RWKV_CHUNKS_PER_STEP = 4
ATTN_BLOCK = 128
NEUMANN_LEVELS = 5


def _params(n_axes):
    return pltpu.CompilerParams(dimension_semantics=("arbitrary",) * n_axes,
                                vmem_limit_bytes=VMEM_LIMIT_BYTES)


def _rmsnorm(x, g):
    return x * lax.rsqrt(jnp.mean(x * x, axis=-1, keepdims=True) + RMS_EPS) * g


def _dot(a, b):
    return jnp.dot(a, b, preferred_element_type=F32)


def _dot_nt(a, b):
    return lax.dot_general(a, b, (((1,), (1,)), ((), ())), preferred_element_type=F32)


def _split2(x):
    hi = x.astype(BF16)
    lo = (x - hi.astype(F32)).astype(BF16)
    return hi, lo


def _split3(x):
    hi = x.astype(BF16)
    r1 = x - hi.astype(F32)
    mid = r1.astype(BF16)
    lo = (r1 - mid.astype(F32)).astype(BF16)
    return hi, mid, lo


def _mm3(a, b):
    ah, al = _split2(a)
    bh, bl = _split2(b)
    return _dot(ah, bh) + (_dot(al, bh) + _dot(ah, bl))


def _mm3_nt(a, b):
    ah, al = _split2(a)
    bh, bl = _split2(b)
    return _dot_nt(ah, bh) + (_dot_nt(al, bh) + _dot_nt(ah, bl))


def _exact_rhs(x, m_bf16):
    hi, mid, lo = _split3(x)
    return _dot(hi, m_bf16) + (_dot(mid, m_bf16) + _dot(lo, m_bf16))


def _exact_lhs(m_bf16, x):
    hi, mid, lo = _split3(x)
    return _dot(m_bf16, hi) + (_dot(m_bf16, mid) + _dot(m_bf16, lo))


def _norm_matmul_kernel(x_ref, g_ref, w_ref, o_ref):
    h = _rmsnorm(x_ref[...], g_ref[...]).astype(BF16)
    o_ref[...] = _dot(h, w_ref[...]).astype(o_ref.dtype)


def norm_matmul(x, g, w, out_dtype):
    t_len, d = x.shape
    n = w.shape[1]
    tm = min(ROW_TILE, t_len)
    return pl.pallas_call(
        _norm_matmul_kernel,
        grid=(t_len // tm,),
        in_specs=[pl.BlockSpec((tm, d), lambda i: (i, 0)),
                  pl.BlockSpec((1, d), lambda i: (0, 0)),
                  pl.BlockSpec((d, n), lambda i: (0, 0))],
        out_specs=pl.BlockSpec((tm, n), lambda i: (i, 0)),
        out_shape=jax.ShapeDtypeStruct((t_len, n), out_dtype),
        compiler_params=_params(1),
        name="norm_matmul",
    )(x, g.reshape(1, d), w)


def _gmlp_kernel(p_ref, lng_ref, lnb_ref, ws_ref, bs_ref, o_ref, *, n_chunks):
    row = lax.broadcasted_iota(jnp.int32, (A_CHUNK, A_CHUNK), 0)
    col = lax.broadcasted_iota(jnp.int32, (A_CHUNK, A_CHUNK), 1)
    causal = col <= row
    w_masked = [jnp.where(causal, ws_ref[g], 0.0).astype(BF16) for g in range(A_GROUPS)]
    for c in range(n_chunks):
        rows = slice(c * A_CHUNK, (c + 1) * A_CHUNK)
        u = p_ref[rows, :A_WIDTH]
        v = p_ref[rows, A_WIDTH:]
        mean = jnp.mean(v, axis=-1, keepdims=True)
        cen = v - mean
        var = jnp.mean(cen * cen, axis=-1, keepdims=True)
        vn = (cen * lax.rsqrt(var + LN_EPS) * lng_ref[...] + lnb_ref[...]).astype(BF16)
        for g in range(A_GROUPS):
            cols = slice(g * A_GROUP, (g + 1) * A_GROUP)
            mixed = _dot(w_masked[g], vn[:, cols]) + bs_ref[:, g:g + 1]
            o_ref[rows, cols] = (u[:, cols] * mixed).astype(o_ref.dtype)


def gmlp(p_a, ln_g, ln_b, w_s, b_s):
    t_len = p_a.shape[0]
    tm = min(ROW_TILE, t_len)
    return pl.pallas_call(
        functools.partial(_gmlp_kernel, n_chunks=tm // A_CHUNK),
        grid=(t_len // tm,),
        in_specs=[pl.BlockSpec((tm, 2 * A_WIDTH), lambda i: (i, 0)),
                  pl.BlockSpec((1, A_WIDTH), lambda i: (0, 0)),
                  pl.BlockSpec((1, A_WIDTH), lambda i: (0, 0)),
                  pl.BlockSpec((A_GROUPS, A_CHUNK, A_CHUNK), lambda i: (0, 0, 0)),
                  pl.BlockSpec((A_CHUNK, A_GROUPS), lambda i: (0, 0))],
        out_specs=pl.BlockSpec((tm, A_WIDTH), lambda i: (i, 0)),
        out_shape=jax.ShapeDtypeStruct((t_len, A_WIDTH), BF16),
        compiler_params=_params(1),
        name="gmlp",
    )(p_a, ln_g.reshape(1, A_WIDTH), ln_b.reshape(1, A_WIDTH), w_s, b_s.T)


def _rwkv_prep_kernel(p_ref, mu_ref, w0_ref, wa_ref, a0_ref, g2_ref, kkw_ref, kaw_ref, seg_ref,
                      r_o, k_o, v_o, kk_o, b_o, lw_o, g_o, carry_ref):
    tm = p_ref.shape[0]

    @pl.when(pl.program_id(0) == 0)
    def _():
        carry_ref[...] = jnp.zeros_like(carry_ref)

    p = p_ref[...]
    rowid = lax.broadcasted_iota(jnp.int32, (tm, 1), 0)
    prev = jnp.where(rowid == 0, carry_ref[0:1, :], pltpu.roll(p, 1, 0))
    carry_ref[0:1, :] = p[tm - 1:tm, :]
    ps = p + (prev - p) * mu_ref[...]

    r = ps[:, :B_WIDTH]
    k = ps[:, B_WIDTH:2 * B_WIDTH]
    v = ps[:, 2 * B_WIDTH:3 * B_WIDTH]
    lwa = ps[:, 3 * B_WIDTH:3 * B_WIDTH + LORA_W + LORA_A]
    lg = ps[:, 3 * B_WIDTH + LORA_W + LORA_A:]

    lane = lax.broadcasted_iota(jnp.int32, (1, LORA_W + LORA_A), 1)
    lora_in = jnp.where(lane < LORA_W, jnp.tanh(lwa), lwa).astype(BF16)
    lora = _dot(lora_in, wa_ref[...])
    neg = -(w0_ref[...] + lora[:, :B_WIDTH])
    softplus = jnp.maximum(neg, 0.0) + jnp.log1p(jnp.exp(-jnp.abs(neg)))
    w = -softplus - 0.5
    a = jax.nn.sigmoid(a0_ref[...] + lora[:, B_WIDTH:])
    g = _dot(jax.nn.sigmoid(lg).astype(BF16), g2_ref[...])

    kkr = k * kkw_ref[...]
    ss = _exact_rhs(kkr * kkr, seg_ref[...])
    kk = kkr * lax.rsqrt(jnp.maximum(ss, 1e-24))

    r_o[...] = r
    k_o[...] = k * (1.0 + (a - 1.0) * kaw_ref[...])
    v_o[...] = v
    kk_o[...] = kk
    b_o[...] = kk * a
    lw_o[...] = -jnp.exp(w)
    g_o[...] = g


def _head_segment_matrix():
    head = jnp.arange(B_WIDTH) // B_HEAD
    return (head[:, None] == head[None, :]).astype(BF16)


def rwkv_prep(p_b, mu, w0, w2, a0, a2, g2, k_k, k_a):
    t_len = p_b.shape[0]
    tm = min(PREP_ROW_TILE, t_len)
    zeros = jnp.zeros((LORA_W, B_WIDTH), F32)
    wa = jnp.concatenate([jnp.concatenate([w2, zeros], 1),
                          jnp.concatenate([zeros, a2], 1)], 0).astype(BF16)
    seg = _head_segment_matrix()
    row = lambda z: z.reshape(1, -1)
    full = lambda shape: pl.BlockSpec(shape, lambda i: (0,) * len(shape))
    out_sds = jax.ShapeDtypeStruct((t_len, B_WIDTH), F32)
    out_spec = pl.BlockSpec((tm, B_WIDTH), lambda i: (i, 0))
    return pl.pallas_call(
        _rwkv_prep_kernel,
        grid=(t_len // tm,),
        in_specs=[pl.BlockSpec((tm, SHIFT_W), lambda i: (i, 0)),
                  full((1, SHIFT_W)), full((1, B_WIDTH)), full((LORA_W + LORA_A, 2 * B_WIDTH)),
                  full((1, B_WIDTH)), full((LORA_G, B_WIDTH)), full((1, B_WIDTH)),
                  full((1, B_WIDTH)), full((B_WIDTH, B_WIDTH))],
        out_specs=[out_spec] * 7,
        out_shape=[out_sds] * 7,
        scratch_shapes=[pltpu.VMEM((SUBLANES, SHIFT_W), F32)],
        compiler_params=_params(1),
        name="rwkv_prep",
    )(p_b, row(mu), row(w0), wa, row(a0), g2.astype(BF16), row(k_k), row(k_a), seg)


def _rwkv_scan_kernel(r_ref, k_ref, v_ref, kk_ref, b_ref, lw_ref, g_ref, rk_ref, gng_ref, gnb_ref,
                      seg_ref, o_ref, h_ref, po_ref, q_ref, o2_ref, oacc_ref, *, n_chunks):
    c_len = RWKV_CHUNK
    n_blocks = B_WIDTH // LANES

    @pl.when(pl.program_id(0) == 0)
    def _():
        h_ref[...] = jnp.zeros_like(h_ref)

    lane = lax.broadcasted_iota(jnp.int32, (1, LANES), 1)
    head0 = lane < B_HEAD
    rho = lax.broadcasted_iota(jnp.int32, (LANES, LANES), 0)
    sig = lax.broadcasted_iota(jnp.int32, (LANES, LANES), 1)
    same_head = (rho // c_len) == (sig // c_len)
    strict = same_head & ((sig % c_len) < (rho % c_len))
    incl = same_head & ((sig % c_len) <= (rho % c_len))
    eye = rho == sig
    tm = n_chunks * c_len
    tri_r = lax.broadcasted_iota(jnp.int32, (tm, tm), 0)
    tri_c = lax.broadcasted_iota(jnp.int32, (tm, tm), 1)
    cum_mat = ((tri_r // c_len == tri_c // c_len) & (tri_c <= tri_r)).astype(BF16)

    def stack(x):
        return jnp.concatenate([jnp.where(head0, x, 0.0), jnp.where(head0, 0.0, x)], axis=0)

    lw_all = lw_ref[...]
    cum_all = _exact_lhs(cum_mat, lw_all)
    chains = []
    for c in range(n_chunks):
        rows = slice(c * c_len, (c + 1) * c_len)
        cum = cum_all[rows, :]
        total = cum[c_len - 1:c_len, :]
        grow = jnp.exp(cum)
        inv = jnp.exp(-cum)
        to_end = jnp.exp(total - cum)
        decay = jnp.exp(total)
        r_t = r_ref[rows, :] * grow
        kk_t = kk_ref[rows, :] * jnp.exp(cum - lw_all[rows, :])
        b_t = b_ref[rows, :] * inv
        k_t = k_ref[rows, :] * inv
        b_e = b_ref[rows, :] * to_end
        k_e = k_ref[rows, :] * to_end
        v = v_ref[rows, :]
        for blk in range(n_blocks):
            cols = slice(blk * LANES, (blk + 1) * LANES)
            kk_s = stack(kk_t[:, cols]).astype(BF16)
            r_s = stack(r_t[:, cols])
            a_all = _dot_nt(
                jnp.concatenate([kk_s, r_s.astype(BF16)], axis=0),
                jnp.concatenate([stack(b_t[:, cols]), stack(k_t[:, cols])], axis=0).astype(BF16))
            neg_l = -jnp.where(strict, a_all[:LANES, :LANES], 0.0)
            v_lhs = jnp.concatenate(
                [stack(k_e[:, cols]).T,
                 jnp.where(incl, a_all[LANES:, LANES:], 0.0),
                 jnp.where(strict, a_all[:LANES, LANES:], 0.0)], axis=0).astype(BF16)
            w_lhs = jnp.concatenate(
                [stack(b_e[:, cols]).T, jnp.where(incl, a_all[LANES:, :LANES], 0.0)],
                axis=0).astype(BF16)
            chains.append(dict(kk_s=kk_s, r_s=r_s, v_s=stack(v[:, cols]).astype(BF16),
                               v_lhs=v_lhs, w_lhs=w_lhs, decay=decay[:, cols],
                               power=neg_l.astype(BF16),
                               t_inv=jnp.where(eye, 1.0, 0.0) + neg_l))

    for ch in chains:
        ch["power"] = _dot(ch["power"], ch["power"]).astype(BF16)
    for level in range(1, NEUMANN_LEVELS):
        for ch in chains:
            both = _dot(jnp.concatenate([ch["power"], ch["t_inv"].astype(BF16)], axis=0),
                        ch["power"])
            ch["power"] = both[:LANES].astype(BF16)
            ch["t_inv"] = ch["t_inv"] + both[LANES:]
    for ch in chains:
        ch["t_inv"] = (ch["t_inv"] + _dot(ch["t_inv"].astype(BF16), ch["power"])).astype(BF16)

    for ch in chains:
        ch["xv"] = _dot(ch["v_lhs"], ch["v_s"])
    for ch in chains:
        rhs = jnp.concatenate([ch["kk_s"], ch["xv"][2 * LANES:].astype(BF16)], axis=1)
        ch["w12"] = _dot(ch["t_inv"], rhs).astype(BF16)
    for idx, ch in enumerate(chains):
        xw = _dot(ch["w_lhs"], ch["w12"])
        p_mat = jnp.where(eye, ch["decay"], 0.0) - xw[:LANES, :LANES]
        o1 = ch["r_s"] - xw[LANES:, :LANES]
        po_ref[idx] = jnp.concatenate([p_mat, o1], axis=0).astype(BF16)
        q_ref[idx] = ch["xv"][:LANES] - xw[:LANES, LANES:]
        o2_ref[idx] = ch["xv"][LANES:2 * LANES] - xw[LANES:, LANES:]

    states = [h_ref[blk] for blk in range(n_blocks)]
    for c in range(n_chunks):
        rows = slice(c * c_len, (c + 1) * c_len)
        pos = [_dot(po_ref[c * n_blocks + blk], states[blk].astype(BF16))
               for blk in range(n_blocks)]
        for blk in range(n_blocks):
            chain = c * n_blocks + blk
            states[blk] = pos[blk][:LANES] + q_ref[chain]
            o_st = pos[blk][LANES:] + o2_ref[chain]
            oacc_ref[rows, blk * LANES:(blk + 1) * LANES] = o_st[:c_len] + o_st[c_len:]
    for blk in range(n_blocks):
        h_ref[blk] = states[blk]

    o = oacc_ref[...]
    seg = seg_ref[...]
    mean = _exact_rhs(o, seg) * (1.0 / B_HEAD)
    cen = o - mean
    var = _exact_rhs(cen * cen, seg) * (1.0 / B_HEAD)
    normed = cen * lax.rsqrt(var + GN_EPS) * gng_ref[...] + gnb_ref[...]
    bonus = _exact_rhs(r_ref[...] * k_ref[...] * rk_ref[...], seg) * v_ref[...]
    o_ref[...] = ((normed + bonus) * g_ref[...]).astype(o_ref.dtype)


def rwkv_scan(r, k, v, kk, b, lw, g, r_k, gn_g, gn_b):
    t_len = r.shape[0]
    n_chunks = RWKV_CHUNKS_PER_STEP
    tm = n_chunks * RWKV_CHUNK
    n_blocks = B_WIDTH // LANES
    n_chains = n_chunks * n_blocks
    row = lambda z: z.reshape(1, B_WIDTH)
    tile = pl.BlockSpec((tm, B_WIDTH), lambda i: (i, 0))
    vec = pl.BlockSpec((1, B_WIDTH), lambda i: (0, 0))
    return pl.pallas_call(
        functools.partial(_rwkv_scan_kernel, n_chunks=n_chunks),
        grid=(t_len // tm,),
        in_specs=[tile] * 7 + [vec] * 3 + [pl.BlockSpec((B_WIDTH, B_WIDTH), lambda i: (0, 0))],
        out_specs=tile,
        out_shape=jax.ShapeDtypeStruct((t_len, B_WIDTH), BF16),
        scratch_shapes=[pltpu.VMEM((n_blocks, LANES, LANES), F32),
                        pltpu.VMEM((n_chains, 2 * LANES, LANES), BF16),
                        pltpu.VMEM((n_chains, LANES, LANES), F32),
                        pltpu.VMEM((n_chains, LANES, LANES), F32),
                        pltpu.VMEM((tm, B_WIDTH), F32)],
        compiler_params=_params(1),
        name="rwkv_scan",
    )(r, k, v, kk, b, lw, g, row(r_k), row(gn_g), row(gn_b), _head_segment_matrix())


def _proj_residual_kernel(x_ref, ya_ref, yb_ref, wa_ref, wb_ref, o_ref):
    o_ref[...] = x_ref[...] + _dot(ya_ref[...], wa_ref[...]) + _dot(yb_ref[...], wb_ref[...])


def proj_residual(x, y_a, y_b, w_out):
    t_len, d = x.shape
    tm = min(ROW_TILE, t_len)
    w = w_out.astype(BF16)
    half = pl.BlockSpec((tm, A_WIDTH), lambda i: (i, 0))
    wspec = pl.BlockSpec((A_WIDTH, d), lambda i: (0, 0))
    return pl.pallas_call(
        _proj_residual_kernel,
        grid=(t_len // tm,),
        in_specs=[pl.BlockSpec((tm, d), lambda i: (i, 0)), half, half, wspec, wspec],
        out_specs=pl.BlockSpec((tm, d), lambda i: (i, 0)),
        out_shape=jax.ShapeDtypeStruct((t_len, d), F32),
        compiler_params=_params(1),
        name="proj_residual",
    )(x, y_a, y_b, w[:A_WIDTH], w[A_WIDTH:])


def _attn_kernel(q_ref, kp_ref, kc_ref, vp_ref, vc_ref, o_ref, lse_ref):
    blk = ATTN_BLOCK
    i = pl.program_id(0)
    row = lax.broadcasted_iota(jnp.int32, (blk, 2 * blk), 0)
    col = lax.broadcasted_iota(jnp.int32, (blk, 2 * blk), 1)
    dist = row + blk - col
    valid = (dist >= 0) & (dist <= blk) & ((col >= blk) | (i > 0))
    lane = lax.broadcasted_iota(jnp.int32, (1, LANES), 1)
    head0 = lane < C_HEAD
    lse_acc = jnp.zeros((blk, LANES), F32)
    for pair in range(C_HEADS // HEADS_PER_BLOCK):
        cols = slice(pair * LANES, (pair + 1) * LANES)
        q2 = q_ref[:, cols]
        k2 = jnp.concatenate([kp_ref[:, cols], kc_ref[:, cols]], axis=0)
        v2 = jnp.concatenate([vp_ref[:, cols], vc_ref[:, cols]], axis=0)
        outs = []
        for h in range(HEADS_PER_BLOCK):
            mask = head0 if h == 0 else jnp.logical_not(head0)
            qh = jnp.where(mask, q2, jnp.zeros_like(q2))
            s = _dot_nt(qh, k2) * (C_HEAD ** -0.5)
            s = jnp.where(valid, s, NEG_INF)
            mx = jnp.max(s, axis=-1, keepdims=True)
            e = jnp.exp(s - mx)
            den = jnp.sum(e, axis=-1, keepdims=True)
            outs.append(_dot(e.astype(BF16), v2) / den)
            lse = mx + jnp.log(den)
            lse_acc = lse_acc + jnp.where(lane == pair * HEADS_PER_BLOCK + h, lse, 0.0)
        o_ref[:, cols] = jnp.where(head0, outs[0], outs[1])
    lse_ref[...] = lse_acc


def dilated_attention_pattern(qkv, dil):
    t_len = qkv.shape[0]
    d = D_MODEL
    blk = ATTN_BLOCK
    sub = t_len // dil
    view = qkv.reshape(sub, dil * 3 * d)
    cur = lambda part: pl.BlockSpec((blk, d), lambda i, r: (i, 3 * r + part))
    prv = lambda part: pl.BlockSpec((blk, d), lambda i, r: (jnp.maximum(i - 1, 0), 3 * r + part))
    out, lse = pl.pallas_call(
        _attn_kernel,
        grid=(sub // blk, dil),
        in_specs=[cur(0), prv(1), cur(1), prv(2), cur(2)],
        out_specs=[pl.BlockSpec((blk, d), lambda i, r: (i, r)),
                   pl.BlockSpec((blk, LANES), lambda i, r: (i, r))],
        out_shape=[jax.ShapeDtypeStruct((sub, dil * d), F32),
                   jax.ShapeDtypeStruct((sub, dil * LANES), F32)],
        compiler_params=_params(2),
        name=f"dilated_attn_d{dil}",
    )(view, view, view, view, view)
    return out.reshape(t_len, d), lse.reshape(t_len, LANES)


def _attn_combine_proj_kernel(x_ref, o1_ref, o2_ref, o3_ref, l1_ref, l2_ref, l3_ref, ex_ref, w_ref,
                              o_ref):
    lses = [l1_ref[...], l2_ref[...], l3_ref[...]]
    mx = jnp.maximum(jnp.maximum(lses[0], lses[1]), lses[2])
    es = [jnp.exp(l - mx) for l in lses]
    den = es[0] + es[1] + es[2]
    y = jnp.zeros(o_ref.shape, F32)
    for e, o in zip(es, (o1_ref, o2_ref, o3_ref)):
        y = y + _exact_rhs(e / den, ex_ref[...]) * o[...]
    o_ref[...] = x_ref[...] + _dot(y.astype(BF16), w_ref[...])


def attn_combine_proj(x, outs, lses, w_out):
    t_len, d = x.shape
    tm = min(ROW_TILE, t_len)
    head_of_lane = jnp.arange(d) // C_HEAD
    expand = (jnp.arange(LANES)[:, None] == head_of_lane[None, :]).astype(BF16)
    wide = pl.BlockSpec((tm, d), lambda i: (i, 0))
    narrow = pl.BlockSpec((tm, LANES), lambda i: (i, 0))
    return pl.pallas_call(
        _attn_combine_proj_kernel,
        grid=(t_len // tm,),
        in_specs=[wide, wide, wide, wide, narrow, narrow, narrow,
                  pl.BlockSpec((LANES, d), lambda i: (0, 0)),
                  pl.BlockSpec((d, d), lambda i: (0, 0))],
        out_specs=wide,
        out_shape=jax.ShapeDtypeStruct((t_len, d), F32),
        compiler_params=_params(1),
        name="attn_combine_proj",
    )(x, *outs, *lses, expand, w_out.astype(BF16))


def _ffn_kernel(x_ref, g_ref, wg_ref, wv_ref, cwg_ref, cwv_ref, cbg_ref, cbv_ref, wd_ref, fn_ref,
                o_ref, h_ref, zg_ref, zv_ref, carry_g_ref, carry_v_ref, *, final_norm):
    i = pl.program_id(0)
    j = pl.program_id(1)
    tm = x_ref.shape[0]
    halo = SUBLANES

    @pl.when(j == 0)
    def _():
        x = x_ref[...]
        h_ref[...] = _rmsnorm(x, g_ref[...]).astype(BF16)
        o_ref[...] = x

    @pl.when(i == 0)
    def _():
        carry_g_ref[j] = jnp.zeros(carry_g_ref.shape[1:], F32)
        carry_v_ref[j] = jnp.zeros(carry_v_ref.shape[1:], F32)

    h = h_ref[...]

    def conv_branch(w_ref, cw_ref, cb_ref, z_ref, carry_ref):
        z = _dot(h, w_ref[...])
        z_ref[0:halo, :] = carry_ref[j]
        z_ref[halo:halo + tm, :] = z
        carry_ref[j] = z[tm - halo:tm, :]
        z1 = z_ref[halo - 1:halo - 1 + tm, :]
        z2 = z_ref[halo - 2:halo - 2 + tm, :]
        return cw_ref[0:1, :] * z2 + cw_ref[1:2, :] * z1 + cw_ref[2:3, :] * z + cb_ref[...]

    gate = conv_branch(wg_ref, cwg_ref, cbg_ref, zg_ref, carry_g_ref)
    val = conv_branch(wv_ref, cwv_ref, cbv_ref, zv_ref, carry_v_ref)
    act = (gate * jax.nn.sigmoid(gate) * val).astype(BF16)
    o_ref[...] += _dot(act, wd_ref[...])

    if final_norm:
        @pl.when(j == pl.num_programs(1) - 1)
        def _():
            o_ref[...] = _rmsnorm(o_ref[...], fn_ref[...])


def conv_glu_ffn(x, norm_g, w_up, conv_w, conv_b, w_down, final_g, final_norm):
    t_len, d = x.shape
    tm = min(FFN_ROW_TILE, t_len)
    tf = FFN_COL_TILE
    n_j = D_FF // tf
    w_up = w_up.astype(BF16)
    w_down = w_down.astype(BF16)
    conv_b = conv_b.reshape(1, 2 * D_FF)
    return pl.pallas_call(
        functools.partial(_ffn_kernel, final_norm=final_norm),
        grid=(t_len // tm, n_j),
        in_specs=[pl.BlockSpec((tm, d), lambda i, j: (i, 0)),
                  pl.BlockSpec((1, d), lambda i, j: (0, 0)),
                  pl.BlockSpec((d, tf), lambda i, j: (0, j)),
                  pl.BlockSpec((d, tf), lambda i, j: (0, n_j + j)),
                  pl.BlockSpec((3, tf), lambda i, j: (0, j)),
                  pl.BlockSpec((3, tf), lambda i, j: (0, n_j + j)),
                  pl.BlockSpec((1, tf), lambda i, j: (0, j)),
                  pl.BlockSpec((1, tf), lambda i, j: (0, n_j + j)),
                  pl.BlockSpec((tf, d), lambda i, j: (j, 0)),
                  pl.BlockSpec((1, d), lambda i, j: (0, 0))],
        out_specs=pl.BlockSpec((tm, d), lambda i, j: (i, 0)),
        out_shape=jax.ShapeDtypeStruct((t_len, d), F32),
        scratch_shapes=[pltpu.VMEM((tm, d), BF16),
                        pltpu.VMEM((tm + SUBLANES, tf), F32),
                        pltpu.VMEM((tm + SUBLANES, tf), F32),
                        pltpu.VMEM((n_j, SUBLANES, tf), F32),
                        pltpu.VMEM((n_j, SUBLANES, tf), F32)],
        compiler_params=_params(2),
        name="conv_glu_ffn",
    )(x, norm_g.reshape(1, d), w_up, w_up, conv_w, conv_w, conv_b, conv_b, w_down,
      final_g.reshape(1, d))


def even_mixer(x, norm, w_in, ln_g, ln_b, w_s, b_s, mu, w0, w2, a0, a2, g2, k_k, k_a, r_k, gn_g,
               gn_b, w_out):
    w_in = w_in.astype(BF16)
    p_a = norm_matmul(x, norm, w_in[:, :2 * A_WIDTH], F32)
    p_b = norm_matmul(x, norm, w_in[:, 2 * A_WIDTH:], F32)
    y_a = gmlp(p_a, ln_g, ln_b, w_s, b_s)
    r, k, v, kk, b, lw, g = rwkv_prep(p_b, mu, w0, w2, a0, a2, g2, k_k, k_a)
    y_b = rwkv_scan(r, k, v, kk, b, lw, g, r_k, gn_g, gn_b)
    return proj_residual(x, y_a, y_b, w_out)


def odd_mixer(x, norm, w_qkv, w_out):
    qkv = norm_matmul(x, norm, w_qkv.astype(BF16), BF16)
    outs, lses = [], []
    for _, dil in C_PATTERNS:
        o, l = dilated_attention_pattern(qkv, dil)
        outs.append(o)
        lses.append(l)
    return attn_combine_proj(x, outs, lses, w_out)


def kernel(x, ev_norm, ev_w_in, ev_ln_g, ev_ln_b, ev_w_s, ev_b_s, ev_mu, ev_w0, ev_w2, ev_a0, ev_a2, ev_g2, ev_k_k, ev_k_a, ev_r_k, ev_gn_g, ev_gn_b, ev_w_out, od_norm, od_w_qkv, od_w_out, ff_norm, ff_w_up, ff_conv_w, ff_conv_b, ff_w_down, final_norm):
    bsz, t_len, d = x.shape
    depth = ff_norm.shape[0]
    outs = []
    for bi in range(bsz):
        xb = x[bi]
        for layer in range(depth):
            j = layer // 2
            if layer % 2 == 0:
                xb = even_mixer(xb, ev_norm[j], ev_w_in[j], ev_ln_g[j], ev_ln_b[j], ev_w_s[j],
                                ev_b_s[j], ev_mu[j], ev_w0[j], ev_w2[j], ev_a0[j], ev_a2[j],
                                ev_g2[j], ev_k_k[j], ev_k_a[j], ev_r_k[j], ev_gn_g[j], ev_gn_b[j],
                                ev_w_out[j])
            else:
                xb = odd_mixer(xb, od_norm[j], od_w_qkv[j], od_w_out[j])
            xb = conv_glu_ffn(xb, ff_norm[layer], ff_w_up[layer], ff_conv_w[layer],
                              ff_conv_b[layer], ff_w_down[layer], final_norm,
                              final_norm=(layer == depth - 1))
        outs.append(xb)
    return jnp.stack(outs, axis=0)
```

```python
import functools

import jax
import jax.numpy as jnp
from jax import lax
from jax.experimental import pallas as pl
from jax.experimental.pallas import tpu as pltpu

F32 = jnp.float32
BF16 = jnp.bfloat16

D_MODEL = 1024
A_WIDTH = 512
A_GROUP = 128
A_GROUPS = 4
A_CHUNK = 128
LN_EPS = 1e-5
B_WIDTH = 512
B_HEAD = 64
LORA_W = 64
LORA_A = 64
LORA_G = 128
GN_EPS = 64e-5
SHIFT_W = 3 * B_WIDTH + LORA_W + LORA_A + LORA_G
C_HEAD = 64
C_HEADS = 16
C_PATTERNS = ((128, 1), (512, 4), (2048, 16))
NEG_INF = -1e30
D_FF = 2816
RMS_EPS = 1e-6

LANES = 128
SUBLANES = 8
HEADS_PER_BLOCK = LANES // B_HEAD
VMEM_LIMIT_BYTES = 52 * 1024 * 1024

ROW_TILE = 512
QKV_ROW_TILE = 256
FFN_ROW_TILE = 1024
FFN_COL_TILE = 256
PREP_ROW_TILE = 256
RWKV_CHUNK = 64
RWKV_CHUNKS_PER_STEP = 4
ATTN_BLOCK = 128
NEUMANN_LEVELS = 5


def _params(n_axes):
    return pltpu.CompilerParams(dimension_semantics=("arbitrary",) * n_axes,
                                vmem_limit_bytes=VMEM_LIMIT_BYTES)


def _rmsnorm(x, g):
    return x * lax.rsqrt(jnp.mean(x * x, axis=-1, keepdims=True) + RMS_EPS) * g


def _dot(a, b):
    return jnp.dot(a, b, preferred_element_type=F32)


def _dot_nt(a, b):
    return lax.dot_general(a, b, (((1,), (1,)), ((), ())), preferred_element_type=F32)


def _split2(x):
    hi = x.astype(BF16)
    lo = (x - hi.astype(F32)).astype(BF16)
    return hi, lo


def _split3(x):
    hi = x.astype(BF16)
    r1 = x - hi.astype(F32)
    mid = r1.astype(BF16)
    lo = (r1 - mid.astype(F32)).astype(BF16)
    return hi, mid, lo


def _mm3(a, b):
    ah, al = _split2(a)
    bh, bl = _split2(b)
    return _dot(ah, bh) + (_dot(al, bh) + _dot(ah, bl))


def _mm3_nt(a, b):
    ah, al = _split2(a)
    bh, bl = _split2(b)
    return _dot_nt(ah, bh) + (_dot_nt(al, bh) + _dot_nt(ah, bl))


def _exact_rhs(x, m_bf16):
    hi, mid, lo = _split3(x)
    return _dot(hi, m_bf16) + (_dot(mid, m_bf16) + _dot(lo, m_bf16))


def _exact_lhs(m_bf16, x):
    hi, mid, lo = _split3(x)
    return _dot(m_bf16, hi) + (_dot(m_bf16, mid) + _dot(m_bf16, lo))


def _norm_matmul_kernel(x_ref, g_ref, w_ref, o_ref):
    h = _rmsnorm(x_ref[...], g_ref[...]).astype(BF16)
    o_ref[...] = _dot(h, w_ref[...]).astype(o_ref.dtype)


def norm_matmul(x, g, w, out_dtype):
    t_len, d = x.shape
    n = w.shape[1]
    tm = min(ROW_TILE, t_len)
    return pl.pallas_call(
        _norm_matmul_kernel,
        grid=(t_len // tm,),
        in_specs=[pl.BlockSpec((tm, d), lambda i: (i, 0)),
                  pl.BlockSpec((1, d), lambda i: (0, 0)),
                  pl.BlockSpec((d, n), lambda i: (0, 0))],
        out_specs=pl.BlockSpec((tm, n), lambda i: (i, 0)),
        out_shape=jax.ShapeDtypeStruct((t_len, n), out_dtype),
        compiler_params=_params(1),
        name="norm_matmul",
    )(x, g.reshape(1, d), w)


def _norm_matmul_views_kernel(x_ref, g_ref, w_ref, *refs, dilations):
    o_refs, acc_ref = refs[:-1], refs[-1]
    tm = x_ref.shape[0]
    n = w_ref.shape[1]
    h = _rmsnorm(x_ref[...], g_ref[...]).astype(BF16)
    res = _dot(h, w_ref[...])
    for c in range(n // LANES):
        acc_ref[c] = res[:, c * LANES:(c + 1) * LANES]
    for dil, o_ref in zip(dilations, o_refs):
        if dil == 1:
            o_ref[...] = res.astype(o_ref.dtype)
            continue
        for r in range(dil):
            for c in range(n // LANES):
                rows = acc_ref[c, pl.ds(r, tm // dil, stride=dil), :]
                o_ref[:, r * n + c * LANES:r * n + (c + 1) * LANES] = rows.astype(o_ref.dtype)


def norm_matmul_views(x, g, w, dilations):
    t_len, d = x.shape
    n = w.shape[1]
    tm = min(QKV_ROW_TILE, t_len)
    return pl.pallas_call(
        functools.partial(_norm_matmul_views_kernel, dilations=dilations),
        grid=(t_len // tm,),
        in_specs=[pl.BlockSpec((tm, d), lambda i: (i, 0)),
                  pl.BlockSpec((1, d), lambda i: (0, 0)),
                  pl.BlockSpec((d, n), lambda i: (0, 0))],
        out_specs=[pl.BlockSpec((tm // dil, dil * n), lambda i: (i, 0)) for dil in dilations],
        out_shape=[jax.ShapeDtypeStruct((t_len // dil, dil * n), BF16) for dil in dilations],
        scratch_shapes=[pltpu.VMEM((n // LANES, tm, LANES), F32)],
        compiler_params=_params(1),
        name="norm_matmul_views",
    )(x, g.reshape(1, d), w)


def _gmlp_kernel(p_ref, lng_ref, lnb_ref, ws_ref, bs_ref, o_ref, *, n_chunks):
    row = lax.broadcasted_iota(jnp.int32, (A_CHUNK, A_CHUNK), 0)
    col = lax.broadcasted_iota(jnp.int32, (A_CHUNK, A_CHUNK), 1)
    causal = col <= row
    w_masked = [jnp.where(causal, ws_ref[g], 0.0).astype(BF16) for g in range(A_GROUPS)]
    for c in range(n_chunks):
        rows = slice(c * A_CHUNK, (c + 1) * A_CHUNK)
        u = p_ref[rows, :A_WIDTH]
        v = p_ref[rows, A_WIDTH:]
        mean = jnp.mean(v, axis=-1, keepdims=True)
        cen = v - mean
        var = jnp.mean(cen * cen, axis=-1, keepdims=True)
        vn = (cen * lax.rsqrt(var + LN_EPS) * lng_ref[...] + lnb_ref[...]).astype(BF16)
        for g in range(A_GROUPS):
            cols = slice(g * A_GROUP, (g + 1) * A_GROUP)
            mixed = _dot(w_masked[g], vn[:, cols]) + bs_ref[:, g:g + 1]
            o_ref[rows, cols] = (u[:, cols] * mixed).astype(o_ref.dtype)


def gmlp(p_a, ln_g, ln_b, w_s, b_s):
    t_len = p_a.shape[0]
    tm = min(ROW_TILE, t_len)
    return pl.pallas_call(
        functools.partial(_gmlp_kernel, n_chunks=tm // A_CHUNK),
        grid=(t_len // tm,),
        in_specs=[pl.BlockSpec((tm, 2 * A_WIDTH), lambda i: (i, 0)),
                  pl.BlockSpec((1, A_WIDTH), lambda i: (0, 0)),
                  pl.BlockSpec((1, A_WIDTH), lambda i: (0, 0)),
                  pl.BlockSpec((A_GROUPS, A_CHUNK, A_CHUNK), lambda i: (0, 0, 0)),
                  pl.BlockSpec((A_CHUNK, A_GROUPS), lambda i: (0, 0))],
        out_specs=pl.BlockSpec((tm, A_WIDTH), lambda i: (i, 0)),
        out_shape=jax.ShapeDtypeStruct((t_len, A_WIDTH), BF16),
        compiler_params=_params(1),
        name="gmlp",
    )(p_a, ln_g.reshape(1, A_WIDTH), ln_b.reshape(1, A_WIDTH), w_s, b_s.T)


def _rwkv_prep_kernel(p_ref, mu_ref, w0_ref, wa_ref, a0_ref, g2_ref, kkw_ref, kaw_ref, seg_ref,
                      r_o, k_o, v_o, kk_o, b_o, lw_o, g_o, carry_ref):
    tm = p_ref.shape[0]

    @pl.when(pl.program_id(0) == 0)
    def _():
        carry_ref[...] = jnp.zeros_like(carry_ref)

    p = p_ref[...]
    rowid = lax.broadcasted_iota(jnp.int32, (tm, 1), 0)
    prev = jnp.where(rowid == 0, carry_ref[0:1, :], pltpu.roll(p, 1, 0))
    carry_ref[0:1, :] = p[tm - 1:tm, :]
    ps = p + (prev - p) * mu_ref[...]

    r = ps[:, :B_WIDTH]
    k = ps[:, B_WIDTH:2 * B_WIDTH]
    v = ps[:, 2 * B_WIDTH:3 * B_WIDTH]
    lwa = ps[:, 3 * B_WIDTH:3 * B_WIDTH + LORA_W + LORA_A]
    lg = ps[:, 3 * B_WIDTH + LORA_W + LORA_A:]

    lane = lax.broadcasted_iota(jnp.int32, (1, LORA_W + LORA_A), 1)
    lora_in = jnp.where(lane < LORA_W, jnp.tanh(lwa), lwa).astype(BF16)
    lora = _dot(lora_in, wa_ref[...])
    neg = -(w0_ref[...] + lora[:, :B_WIDTH])
    softplus = jnp.maximum(neg, 0.0) + jnp.log1p(jnp.exp(-jnp.abs(neg)))
    w = -softplus - 0.5
    a = jax.nn.sigmoid(a0_ref[...] + lora[:, B_WIDTH:])
    g = _dot(jax.nn.sigmoid(lg).astype(BF16), g2_ref[...])

    kkr = k * kkw_ref[...]
    ss = _exact_rhs(kkr * kkr, seg_ref[...])
    kk = kkr * lax.rsqrt(jnp.maximum(ss, 1e-24))

    r_o[...] = r
    k_o[...] = k * (1.0 + (a - 1.0) * kaw_ref[...])
    v_o[...] = v
    kk_o[...] = kk
    b_o[...] = kk * a
    lw_o[...] = -jnp.exp(w)
    g_o[...] = g


def _head_segment_matrix():
    head = jnp.arange(B_WIDTH) // B_HEAD
    return (head[:, None] == head[None, :]).astype(BF16)


def rwkv_prep(p_b, mu, w0, w2, a0, a2, g2, k_k, k_a):
    t_len = p_b.shape[0]
    tm = min(PREP_ROW_TILE, t_len)
    zeros = jnp.zeros((LORA_W, B_WIDTH), F32)
    wa = jnp.concatenate([jnp.concatenate([w2, zeros], 1),
                          jnp.concatenate([zeros, a2], 1)], 0).astype(BF16)
    seg = _head_segment_matrix()
    row = lambda z: z.reshape(1, -1)
    full = lambda shape: pl.BlockSpec(shape, lambda i: (0,) * len(shape))
    out_sds = jax.ShapeDtypeStruct((t_len, B_WIDTH), F32)
    out_spec = pl.BlockSpec((tm, B_WIDTH), lambda i: (i, 0))
    return pl.pallas_call(
        _rwkv_prep_kernel,
        grid=(t_len // tm,),
        in_specs=[pl.BlockSpec((tm, SHIFT_W), lambda i: (i, 0)),
                  full((1, SHIFT_W)), full((1, B_WIDTH)), full((LORA_W + LORA_A, 2 * B_WIDTH)),
                  full((1, B_WIDTH)), full((LORA_G, B_WIDTH)), full((1, B_WIDTH)),
                  full((1, B_WIDTH)), full((B_WIDTH, B_WIDTH))],
        out_specs=[out_spec] * 7,
        out_shape=[out_sds] * 7,
        scratch_shapes=[pltpu.VMEM((SUBLANES, SHIFT_W), F32)],
        compiler_params=_params(1),
        name="rwkv_prep",
    )(p_b, row(mu), row(w0), wa, row(a0), g2.astype(BF16), row(k_k), row(k_a), seg)


def _rwkv_scan_kernel(r_ref, k_ref, v_ref, kk_ref, b_ref, lw_ref, g_ref, rk_ref, gng_ref, gnb_ref,
                      seg_ref, o_ref, h_ref, po_ref, q_ref, o2_ref, oacc_ref, *, n_chunks):
    c_len = RWKV_CHUNK
    n_blocks = B_WIDTH // LANES

    @pl.when(pl.program_id(0) == 0)
    def _():
        h_ref[...] = jnp.zeros_like(h_ref)

    lane = lax.broadcasted_iota(jnp.int32, (1, LANES), 1)
    head0 = lane < B_HEAD
    rho = lax.broadcasted_iota(jnp.int32, (LANES, LANES), 0)
    sig = lax.broadcasted_iota(jnp.int32, (LANES, LANES), 1)
    same_head = (rho // c_len) == (sig // c_len)
    strict = same_head & ((sig % c_len) < (rho % c_len))
    incl = same_head & ((sig % c_len) <= (rho % c_len))
    eye = rho == sig
    tm = n_chunks * c_len
    tri_r = lax.broadcasted_iota(jnp.int32, (tm, tm), 0)
    tri_c = lax.broadcasted_iota(jnp.int32, (tm, tm), 1)
    cum_mat = ((tri_r // c_len == tri_c // c_len) & (tri_c <= tri_r)).astype(BF16)

    def stack(x):
        return jnp.concatenate([jnp.where(head0, x, 0.0), jnp.where(head0, 0.0, x)], axis=0)

    lw_all = lw_ref[...]
    cum_all = _exact_lhs(cum_mat, lw_all)
    chains = []
    for c in range(n_chunks):
        rows = slice(c * c_len, (c + 1) * c_len)
        cum = cum_all[rows, :]
        total = cum[c_len - 1:c_len, :]
        grow = jnp.exp(cum)
        inv = jnp.exp(-cum)
        to_end = jnp.exp(total - cum)
        decay = jnp.exp(total)
        r_t = r_ref[rows, :] * grow
        kk_t = kk_ref[rows, :] * jnp.exp(cum - lw_all[rows, :])
        b_t = b_ref[rows, :] * inv
        k_t = k_ref[rows, :] * inv
        b_e = b_ref[rows, :] * to_end
        k_e = k_ref[rows, :] * to_end
        v = v_ref[rows, :]
        for blk in range(n_blocks):
            cols = slice(blk * LANES, (blk + 1) * LANES)
            kk_s = stack(kk_t[:, cols]).astype(BF16)
            r_s = stack(r_t[:, cols])
            a_all = _dot_nt(
                jnp.concatenate([kk_s, r_s.astype(BF16)], axis=0),
                jnp.concatenate([stack(b_t[:, cols]), stack(k_t[:, cols])], axis=0).astype(BF16))
            neg_l = -jnp.where(strict, a_all[:LANES, :LANES], 0.0)
            v_lhs = jnp.concatenate(
                [stack(k_e[:, cols]).T,
                 jnp.where(incl, a_all[LANES:, LANES:], 0.0),
                 jnp.where(strict, a_all[:LANES, LANES:], 0.0)], axis=0).astype(BF16)
            w_lhs = jnp.concatenate(
                [stack(b_e[:, cols]).T, jnp.where(incl, a_all[LANES:, :LANES], 0.0)],
                axis=0).astype(BF16)
            chains.append(dict(kk_s=kk_s, r_s=r_s, v_s=stack(v[:, cols]).astype(BF16),
                               v_lhs=v_lhs, w_lhs=w_lhs, decay=decay[:, cols],
                               power=neg_l.astype(BF16),
                               t_inv=jnp.where(eye, 1.0, 0.0) + neg_l))

    for ch in chains:
        ch["power"] = _dot(ch["power"], ch["power"]).astype(BF16)
    for level in range(1, NEUMANN_LEVELS):
        for ch in chains:
            both = _dot(jnp.concatenate([ch["power"], ch["t_inv"].astype(BF16)], axis=0),
                        ch["power"])
            ch["power"] = both[:LANES].astype(BF16)
            ch["t_inv"] = ch["t_inv"] + both[LANES:]
    for ch in chains:
        ch["t_inv"] = (ch["t_inv"] + _dot(ch["t_inv"].astype(BF16), ch["power"])).astype(BF16)

    for ch in chains:
        ch["xv"] = _dot(ch["v_lhs"], ch["v_s"])
    for ch in chains:
        rhs = jnp.concatenate([ch["kk_s"], ch["xv"][2 * LANES:].astype(BF16)], axis=1)
        ch["w12"] = _dot(ch["t_inv"], rhs).astype(BF16)
    for idx, ch in enumerate(chains):
        xw = _dot(ch["w_lhs"], ch["w12"])
        p_mat = jnp.where(eye, ch["decay"], 0.0) - xw[:LANES, :LANES]
        o1 = ch["r_s"] - xw[LANES:, :LANES]
        po_ref[idx] = jnp.concatenate([p_mat, o1], axis=0).astype(BF16)
        q_ref[idx] = ch["xv"][:LANES] - xw[:LANES, LANES:]
        o2_ref[idx] = ch["xv"][LANES:2 * LANES] - xw[LANES:, LANES:]

    states = [h_ref[blk] for blk in range(n_blocks)]
    for c in range(n_chunks):
        rows = slice(c * c_len, (c + 1) * c_len)
        pos = [_dot(po_ref[c * n_blocks + blk], states[blk].astype(BF16))
               for blk in range(n_blocks)]
        for blk in range(n_blocks):
            chain = c * n_blocks + blk
            states[blk] = pos[blk][:LANES] + q_ref[chain]
            o_st = pos[blk][LANES:] + o2_ref[chain]
            oacc_ref[rows, blk * LANES:(blk + 1) * LANES] = o_st[:c_len] + o_st[c_len:]
    for blk in range(n_blocks):
        h_ref[blk] = states[blk]

    o = oacc_ref[...]
    seg = seg_ref[...]
    mean = _exact_rhs(o, seg) * (1.0 / B_HEAD)
    cen = o - mean
    var = _exact_rhs(cen * cen, seg) * (1.0 / B_HEAD)
    normed = cen * lax.rsqrt(var + GN_EPS) * gng_ref[...] + gnb_ref[...]
    bonus = _exact_rhs(r_ref[...] * k_ref[...] * rk_ref[...], seg) * v_ref[...]
    o_ref[...] = ((normed + bonus) * g_ref[...]).astype(o_ref.dtype)


def rwkv_scan(r, k, v, kk, b, lw, g, r_k, gn_g, gn_b):
    t_len = r.shape[0]
    n_chunks = RWKV_CHUNKS_PER_STEP
    tm = n_chunks * RWKV_CHUNK
    n_blocks = B_WIDTH // LANES
    n_chains = n_chunks * n_blocks
    row = lambda z: z.reshape(1, B_WIDTH)
    tile = pl.BlockSpec((tm, B_WIDTH), lambda i: (i, 0))
    vec = pl.BlockSpec((1, B_WIDTH), lambda i: (0, 0))
    return pl.pallas_call(
        functools.partial(_rwkv_scan_kernel, n_chunks=n_chunks),
        grid=(t_len // tm,),
        in_specs=[tile] * 7 + [vec] * 3 + [pl.BlockSpec((B_WIDTH, B_WIDTH), lambda i: (0, 0))],
        out_specs=tile,
        out_shape=jax.ShapeDtypeStruct((t_len, B_WIDTH), BF16),
        scratch_shapes=[pltpu.VMEM((n_blocks, LANES, LANES), F32),
                        pltpu.VMEM((n_chains, 2 * LANES, LANES), BF16),
                        pltpu.VMEM((n_chains, LANES, LANES), F32),
                        pltpu.VMEM((n_chains, LANES, LANES), F32),
                        pltpu.VMEM((tm, B_WIDTH), F32)],
        compiler_params=_params(1),
        name="rwkv_scan",
    )(r, k, v, kk, b, lw, g, row(r_k), row(gn_g), row(gn_b), _head_segment_matrix())


def _proj_residual_kernel(x_ref, ya_ref, yb_ref, wa_ref, wb_ref, o_ref):
    o_ref[...] = x_ref[...] + _dot(ya_ref[...], wa_ref[...]) + _dot(yb_ref[...], wb_ref[...])


def proj_residual(x, y_a, y_b, w_out):
    t_len, d = x.shape
    tm = min(ROW_TILE, t_len)
    w = w_out.astype(BF16)
    half = pl.BlockSpec((tm, A_WIDTH), lambda i: (i, 0))
    wspec = pl.BlockSpec((A_WIDTH, d), lambda i: (0, 0))
    return pl.pallas_call(
        _proj_residual_kernel,
        grid=(t_len // tm,),
        in_specs=[pl.BlockSpec((tm, d), lambda i: (i, 0)), half, half, wspec, wspec],
        out_specs=pl.BlockSpec((tm, d), lambda i: (i, 0)),
        out_shape=jax.ShapeDtypeStruct((t_len, d), F32),
        compiler_params=_params(1),
        name="proj_residual",
    )(x, y_a, y_b, w[:A_WIDTH], w[A_WIDTH:])


def _attn_kernel(q_ref, kp_ref, kc_ref, vp_ref, vc_ref, o_ref, lse_ref, o_tmp, lse_tmp, *, dil):
    blk = ATTN_BLOCK
    i = pl.program_id(0)
    res = pl.program_id(1)
    row = lax.broadcasted_iota(jnp.int32, (2 * blk, 2 * blk), 0) % blk
    col = lax.broadcasted_iota(jnp.int32, (2 * blk, 2 * blk), 1)
    dist = row + blk - col
    valid = (dist >= 0) & (dist <= blk) & ((col >= blk) | (i > 0))
    lane = lax.broadcasted_iota(jnp.int32, (1, LANES), 1)
    head0 = lane < C_HEAD
    lse_acc = jnp.zeros((blk, LANES), F32)
    for pair in range(C_HEADS // HEADS_PER_BLOCK):
        cols = slice(pair * LANES, (pair + 1) * LANES)
        q2 = q_ref[:, cols] * (C_HEAD ** -0.5)
        zero = jnp.zeros_like(q2)
        qs = jnp.concatenate([jnp.where(head0, q2, zero), jnp.where(head0, zero, q2)], axis=0)
        k2 = jnp.concatenate([kp_ref[:, cols], kc_ref[:, cols]], axis=0)
        v2 = jnp.concatenate([vp_ref[:, cols], vc_ref[:, cols]], axis=0)
        s = jnp.where(valid, _dot_nt(qs, k2), NEG_INF)
        mx = jnp.max(s, axis=-1, keepdims=True)
        e = jnp.exp(s - mx)
        den = jnp.sum(e, axis=-1, keepdims=True)
        pv = _dot(e.astype(BF16), v2) / den
        lse = mx + jnp.log(den)
        o_tmp[:, cols] = jnp.where(head0, pv[:blk], pv[blk:])
        lse_acc = (lse_acc + jnp.where(lane == pair * HEADS_PER_BLOCK, lse[:blk], 0.0)
                   + jnp.where(lane == pair * HEADS_PER_BLOCK + 1, lse[blk:], 0.0))
    lse_tmp[...] = lse_acc
    for static_res in range(dil):
        @pl.when(res == static_res)
        def _():
            for c in range(o_ref.shape[0]):
                o_ref[c, pl.ds(static_res, blk, stride=dil), :] = o_tmp[:, c * LANES:(c + 1) * LANES]
            lse_ref[pl.ds(static_res, blk, stride=dil), :] = lse_tmp[...]


def dilated_attention_pattern(view, dil):
    sub = view.shape[0]
    t_len = sub * dil
    d = D_MODEL
    blk = ATTN_BLOCK
    cur = lambda part: pl.BlockSpec((blk, d), lambda i, r: (i, 3 * r + part))
    prv = lambda part: pl.BlockSpec((blk, d), lambda i, r: (jnp.maximum(i - 1, 0), 3 * r + part))
    return pl.pallas_call(
        functools.partial(_attn_kernel, dil=dil),
        grid=(sub // blk, dil),
        in_specs=[cur(0), prv(1), cur(1), prv(2), cur(2)],
        out_specs=[pl.BlockSpec((d // LANES, blk * dil, LANES), lambda i, r: (0, i, 0)),
                   pl.BlockSpec((blk * dil, LANES), lambda i, r: (i, 0))],
        out_shape=[jax.ShapeDtypeStruct((d // LANES, t_len, LANES), F32),
                   jax.ShapeDtypeStruct((t_len, LANES), F32)],
        scratch_shapes=[pltpu.VMEM((blk, d), F32), pltpu.VMEM((blk, LANES), F32)],
        compiler_params=_params(2),
        name=f"dilated_attn_d{dil}",
    )(view, view, view, view, view)


def _attn_combine_proj_kernel(x_ref, o1_ref, o2_ref, o3_ref, l1_ref, l2_ref, l3_ref, ex_ref, w_ref,
                              o_ref):
    lses = [l1_ref[...], l2_ref[...], l3_ref[...]]
    mx = jnp.maximum(jnp.maximum(lses[0], lses[1]), lses[2])
    es = [jnp.exp(l - mx) for l in lses]
    den = es[0] + es[1] + es[2]
    y = jnp.zeros(o_ref.shape, F32)
    for e, o in zip(es, (o1_ref, o2_ref, o3_ref)):
        o_full = jnp.concatenate([o[c] for c in range(o.shape[0])], axis=-1)
        y = y + _exact_rhs(e / den, ex_ref[...]) * o_full
    o_ref[...] = x_ref[...] + _dot(y.astype(BF16), w_ref[...])


def attn_combine_proj(x, outs, lses, w_out):
    t_len, d = x.shape
    tm = min(ROW_TILE, t_len)
    head_of_lane = jnp.arange(d) // C_HEAD
    expand = (jnp.arange(LANES)[:, None] == head_of_lane[None, :]).astype(BF16)
    wide = pl.BlockSpec((tm, d), lambda i: (i, 0))
    chunked = pl.BlockSpec((d // LANES, tm, LANES), lambda i: (0, i, 0))
    narrow = pl.BlockSpec((tm, LANES), lambda i: (i, 0))
    return pl.pallas_call(
        _attn_combine_proj_kernel,
        grid=(t_len // tm,),
        in_specs=[wide, chunked, chunked, chunked, narrow, narrow, narrow,
                  pl.BlockSpec((LANES, d), lambda i: (0, 0)),
                  pl.BlockSpec((d, d), lambda i: (0, 0))],
        out_specs=wide,
        out_shape=jax.ShapeDtypeStruct((t_len, d), F32),
        compiler_params=_params(1),
        name="attn_combine_proj",
    )(x, *outs, *lses, expand, w_out.astype(BF16))


def _ffn_kernel(x_ref, g_ref, wg_ref, wv_ref, cwg_ref, cwv_ref, cbg_ref, cbv_ref, wd_ref, fn_ref,
                o_ref, h_ref, zg_ref, zv_ref, carry_g_ref, carry_v_ref, *, final_norm):
    i = pl.program_id(0)
    j = pl.program_id(1)
    tm = x_ref.shape[0]
    halo = SUBLANES

    @pl.when(j == 0)
    def _():
        x = x_ref[...]
        h_ref[...] = _rmsnorm(x, g_ref[...]).astype(BF16)
        o_ref[...] = x

    @pl.when(i == 0)
    def _():
        carry_g_ref[j] = jnp.zeros(carry_g_ref.shape[1:], F32)
        carry_v_ref[j] = jnp.zeros(carry_v_ref.shape[1:], F32)

    h = h_ref[...]

    def conv_branch(w_ref, cw_ref, cb_ref, z_ref, carry_ref):
        z = _dot(h, w_ref[...])
        z_ref[0:halo, :] = carry_ref[j]
        z_ref[halo:halo + tm, :] = z
        carry_ref[j] = z[tm - halo:tm, :]
        z1 = z_ref[halo - 1:halo - 1 + tm, :]
        z2 = z_ref[halo - 2:halo - 2 + tm, :]
        return cw_ref[0:1, :] * z2 + cw_ref[1:2, :] * z1 + cw_ref[2:3, :] * z + cb_ref[...]

    gate = conv_branch(wg_ref, cwg_ref, cbg_ref, zg_ref, carry_g_ref)
    val = conv_branch(wv_ref, cwv_ref, cbv_ref, zv_ref, carry_v_ref)
    act = (gate * jax.nn.sigmoid(gate) * val).astype(BF16)
    o_ref[...] += _dot(act, wd_ref[...])

    if final_norm:
        @pl.when(j == pl.num_programs(1) - 1)
        def _():
            o_ref[...] = _rmsnorm(o_ref[...], fn_ref[...])


def conv_glu_ffn(x, norm_g, w_up, conv_w, conv_b, w_down, final_g, final_norm):
    t_len, d = x.shape
    tm = min(FFN_ROW_TILE, t_len)
    tf = FFN_COL_TILE
    n_j = D_FF // tf
    w_up = w_up.astype(BF16)
    w_down = w_down.astype(BF16)
    conv_b = conv_b.reshape(1, 2 * D_FF)
    return pl.pallas_call(
        functools.partial(_ffn_kernel, final_norm=final_norm),
        grid=(t_len // tm, n_j),
        in_specs=[pl.BlockSpec((tm, d), lambda i, j: (i, 0)),
                  pl.BlockSpec((1, d), lambda i, j: (0, 0)),
                  pl.BlockSpec((d, tf), lambda i, j: (0, j)),
                  pl.BlockSpec((d, tf), lambda i, j: (0, n_j + j)),
                  pl.BlockSpec((3, tf), lambda i, j: (0, j)),
                  pl.BlockSpec((3, tf), lambda i, j: (0, n_j + j)),
                  pl.BlockSpec((1, tf), lambda i, j: (0, j)),
                  pl.BlockSpec((1, tf), lambda i, j: (0, n_j + j)),
                  pl.BlockSpec((tf, d), lambda i, j: (j, 0)),
                  pl.BlockSpec((1, d), lambda i, j: (0, 0))],
        out_specs=pl.BlockSpec((tm, d), lambda i, j: (i, 0)),
        out_shape=jax.ShapeDtypeStruct((t_len, d), F32),
        scratch_shapes=[pltpu.VMEM((tm, d), BF16),
                        pltpu.VMEM((tm + SUBLANES, tf), F32),
                        pltpu.VMEM((tm + SUBLANES, tf), F32),
                        pltpu.VMEM((n_j, SUBLANES, tf), F32),
                        pltpu.VMEM((n_j, SUBLANES, tf), F32)],
        compiler_params=_params(2),
        name="conv_glu_ffn",
    )(x, norm_g.reshape(1, d), w_up, w_up, conv_w, conv_w, conv_b, conv_b, w_down,
      final_g.reshape(1, d))


def even_mixer(x, norm, w_in, ln_g, ln_b, w_s, b_s, mu, w0, w2, a0, a2, g2, k_k, k_a, r_k, gn_g,
               gn_b, w_out):
    w_in = w_in.astype(BF16)
    p_a = norm_matmul(x, norm, w_in[:, :2 * A_WIDTH], F32)
    p_b = norm_matmul(x, norm, w_in[:, 2 * A_WIDTH:], F32)
    y_a = gmlp(p_a, ln_g, ln_b, w_s, b_s)
    r, k, v, kk, b, lw, g = rwkv_prep(p_b, mu, w0, w2, a0, a2, g2, k_k, k_a)
    y_b = rwkv_scan(r, k, v, kk, b, lw, g, r_k, gn_g, gn_b)
    return proj_residual(x, y_a, y_b, w_out)


def odd_mixer(x, norm, w_qkv, w_out):
    dilations = tuple(dil for _, dil in C_PATTERNS)
    views = norm_matmul_views(x, norm, w_qkv.astype(BF16), dilations)
    outs, lses = [], []
    for view, dil in zip(views, dilations):
        o, l = dilated_attention_pattern(view, dil)
        outs.append(o)
        lses.append(l)
    return attn_combine_proj(x, outs, lses, w_out)


def kernel(x, ev_norm, ev_w_in, ev_ln_g, ev_ln_b, ev_w_s, ev_b_s, ev_mu, ev_w0, ev_w2, ev_a0, ev_a2, ev_g2, ev_k_k, ev_k_a, ev_r_k, ev_gn_g, ev_gn_b, ev_w_out, od_norm, od_w_qkv, od_w_out, ff_norm, ff_w_up, ff_conv_w, ff_conv_b, ff_w_down, final_norm):
    bsz, t_len, d = x.shape
    depth = ff_norm.shape[0]
    outs = []
    for bi in range(bsz):
        xb = x[bi]
        for layer in range(depth):
            j = layer // 2
            if layer % 2 == 0:
                xb = even_mixer(xb, ev_norm[j], ev_w_in[j], ev_ln_g[j], ev_ln_b[j], ev_w_s[j],
                                ev_b_s[j], ev_mu[j], ev_w0[j], ev_w2[j], ev_a0[j], ev_a2[j],
                                ev_g2[j], ev_k_k[j], ev_k_a[j], ev_r_k[j], ev_gn_g[j], ev_gn_b[j],
                                ev_w_out[j])
            else:
                xb = odd_mixer(xb, od_norm[j], od_w_qkv[j], od_w_out[j])
            xb = conv_glu_ffn(xb, ff_norm[layer], ff_w_up[layer], ff_conv_w[layer],
                              ff_conv_b[layer], ff_w_down[layer], final_norm,
                              final_norm=(layer == depth - 1))
        outs.append(xb)
    return jnp.stack(outs, axis=0)
```

```python
import functools

import jax
import jax.numpy as jnp
from jax import lax
from jax.experimental import pallas as pl
from jax.experimental.pallas import tpu as pltpu

F32 = jnp.float32
BF16 = jnp.bfloat16

D_MODEL = 1024
A_WIDTH = 512
A_GROUP = 128
A_GROUPS = 4
A_CHUNK = 128
LN_EPS = 1e-5
B_WIDTH = 512
B_HEAD = 64
LORA_W = 64
LORA_A = 64
LORA_G = 128
GN_EPS = 64e-5
SHIFT_W = 3 * B_WIDTH + LORA_W + LORA_A + LORA_G
C_HEAD = 64
C_HEADS = 16
C_PATTERNS = ((128, 1), (512, 4), (2048, 16))
NEG_INF = -1e30
D_FF = 2816
RMS_EPS = 1e-6

LANES = 128
SUBLANES = 8
HEADS_PER_BLOCK = LANES // B_HEAD
VMEM_LIMIT_BYTES = 52 * 1024 * 1024

ROW_TILE = 512
QKV_ROW_TILE = 256
FFN_ROW_TILE = 512
FFN_COL_TILE = 256
PREP_ROW_TILE = 256
RWKV_CHUNK = 64
RWKV_CHUNKS_PER_STEP = 4
ATTN_BLOCK = 128
NEUMANN_LEVELS = 5


def _params(n_axes):
    return pltpu.CompilerParams(dimension_semantics=("arbitrary",) * n_axes,
                                vmem_limit_bytes=VMEM_LIMIT_BYTES)


def _rmsnorm(x, g):
    return x * lax.rsqrt(jnp.mean(x * x, axis=-1, keepdims=True) + RMS_EPS) * g


def _dot(a, b):
    return jnp.dot(a, b, preferred_element_type=F32)


def _dot_nt(a, b):
    return lax.dot_general(a, b, (((1,), (1,)), ((), ())), preferred_element_type=F32)


def _split2(x):
    hi = x.astype(BF16)
    lo = (x - hi.astype(F32)).astype(BF16)
    return hi, lo


def _split3(x):
    hi = x.astype(BF16)
    r1 = x - hi.astype(F32)
    mid = r1.astype(BF16)
    lo = (r1 - mid.astype(F32)).astype(BF16)
    return hi, mid, lo


def _mm3(a, b):
    ah, al = _split2(a)
    bh, bl = _split2(b)
    return _dot(ah, bh) + (_dot(al, bh) + _dot(ah, bl))


def _mm3_nt(a, b):
    ah, al = _split2(a)
    bh, bl = _split2(b)
    return _dot_nt(ah, bh) + (_dot_nt(al, bh) + _dot_nt(ah, bl))


def _exact_rhs(x, m_bf16):
    hi, mid, lo = _split3(x)
    return _dot(hi, m_bf16) + (_dot(mid, m_bf16) + _dot(lo, m_bf16))


def _exact_lhs(m_bf16, x):
    hi, mid, lo = _split3(x)
    return _dot(m_bf16, hi) + (_dot(m_bf16, mid) + _dot(m_bf16, lo))


def _norm_matmul_kernel(x_ref, g_ref, w_ref, o_ref):
    h = _rmsnorm(x_ref[...], g_ref[...]).astype(BF16)
    o_ref[...] = _dot(h, w_ref[...]).astype(o_ref.dtype)


def norm_matmul(x, g, w, out_dtype):
    t_len, d = x.shape
    n = w.shape[1]
    tm = min(ROW_TILE, t_len)
    return pl.pallas_call(
        _norm_matmul_kernel,
        grid=(t_len // tm,),
        in_specs=[pl.BlockSpec((tm, d), lambda i: (i, 0)),
                  pl.BlockSpec((1, d), lambda i: (0, 0)),
                  pl.BlockSpec((d, n), lambda i: (0, 0))],
        out_specs=pl.BlockSpec((tm, n), lambda i: (i, 0)),
        out_shape=jax.ShapeDtypeStruct((t_len, n), out_dtype),
        compiler_params=_params(1),
        name="norm_matmul",
    )(x, g.reshape(1, d), w)


def _norm_matmul_views_kernel(x_ref, g_ref, w_ref, *refs, dilations):
    o_refs, acc_ref = refs[:-1], refs[-1]
    tm = x_ref.shape[0]
    n = w_ref.shape[1]
    h = _rmsnorm(x_ref[...], g_ref[...]).astype(BF16)
    res = _dot(h, w_ref[...])
    for c in range(n // LANES):
        acc_ref[c] = res[:, c * LANES:(c + 1) * LANES]
    for dil, o_ref in zip(dilations, o_refs):
        if dil == 1:
            o_ref[...] = res.astype(o_ref.dtype)
            continue
        for r in range(dil):
            for c in range(n // LANES):
                rows = acc_ref[c, pl.ds(r, tm // dil, stride=dil), :]
                o_ref[:, r * n + c * LANES:r * n + (c + 1) * LANES] = rows.astype(o_ref.dtype)


def norm_matmul_views(x, g, w, dilations):
    t_len, d = x.shape
    n = w.shape[1]
    tm = min(QKV_ROW_TILE, t_len)
    return pl.pallas_call(
        functools.partial(_norm_matmul_views_kernel, dilations=dilations),
        grid=(t_len // tm,),
        in_specs=[pl.BlockSpec((tm, d), lambda i: (i, 0)),
                  pl.BlockSpec((1, d), lambda i: (0, 0)),
                  pl.BlockSpec((d, n), lambda i: (0, 0))],
        out_specs=[pl.BlockSpec((tm // dil, dil * n), lambda i: (i, 0)) for dil in dilations],
        out_shape=[jax.ShapeDtypeStruct((t_len // dil, dil * n), BF16) for dil in dilations],
        scratch_shapes=[pltpu.VMEM((n // LANES, tm, LANES), F32)],
        compiler_params=_params(1),
        name="norm_matmul_views",
    )(x, g.reshape(1, d), w)


def _gmlp_kernel(p_ref, lng_ref, lnb_ref, ws_ref, bs_ref, o_ref, *, n_chunks):
    row = lax.broadcasted_iota(jnp.int32, (A_CHUNK, A_CHUNK), 0)
    col = lax.broadcasted_iota(jnp.int32, (A_CHUNK, A_CHUNK), 1)
    causal = col <= row
    w_masked = [jnp.where(causal, ws_ref[g], 0.0).astype(BF16) for g in range(A_GROUPS)]
    for c in range(n_chunks):
        rows = slice(c * A_CHUNK, (c + 1) * A_CHUNK)
        u = p_ref[rows, :A_WIDTH]
        v = p_ref[rows, A_WIDTH:]
        mean = jnp.mean(v, axis=-1, keepdims=True)
        cen = v - mean
        var = jnp.mean(cen * cen, axis=-1, keepdims=True)
        vn = (cen * lax.rsqrt(var + LN_EPS) * lng_ref[...] + lnb_ref[...]).astype(BF16)
        for g in range(A_GROUPS):
            cols = slice(g * A_GROUP, (g + 1) * A_GROUP)
            mixed = _dot(w_masked[g], vn[:, cols]) + bs_ref[:, g:g + 1]
            o_ref[rows, cols] = (u[:, cols] * mixed).astype(o_ref.dtype)


def gmlp(p_a, ln_g, ln_b, w_s, b_s):
    t_len = p_a.shape[0]
    tm = min(ROW_TILE, t_len)
    return pl.pallas_call(
        functools.partial(_gmlp_kernel, n_chunks=tm // A_CHUNK),
        grid=(t_len // tm,),
        in_specs=[pl.BlockSpec((tm, 2 * A_WIDTH), lambda i: (i, 0)),
                  pl.BlockSpec((1, A_WIDTH), lambda i: (0, 0)),
                  pl.BlockSpec((1, A_WIDTH), lambda i: (0, 0)),
                  pl.BlockSpec((A_GROUPS, A_CHUNK, A_CHUNK), lambda i: (0, 0, 0)),
                  pl.BlockSpec((A_CHUNK, A_GROUPS), lambda i: (0, 0))],
        out_specs=pl.BlockSpec((tm, A_WIDTH), lambda i: (i, 0)),
        out_shape=jax.ShapeDtypeStruct((t_len, A_WIDTH), BF16),
        compiler_params=_params(1),
        name="gmlp",
    )(p_a, ln_g.reshape(1, A_WIDTH), ln_b.reshape(1, A_WIDTH), w_s, b_s.T)


def _rwkv_prep_kernel(p_ref, mu_ref, w0_ref, wa_ref, a0_ref, g2_ref, kkw_ref, kaw_ref, seg_ref,
                      r_o, k_o, v_o, kk_o, b_o, lw_o, g_o, carry_ref):
    tm = p_ref.shape[0]

    @pl.when(pl.program_id(0) == 0)
    def _():
        carry_ref[...] = jnp.zeros_like(carry_ref)

    p = p_ref[...]
    rowid = lax.broadcasted_iota(jnp.int32, (tm, 1), 0)
    prev = jnp.where(rowid == 0, carry_ref[0:1, :], pltpu.roll(p, 1, 0))
    carry_ref[0:1, :] = p[tm - 1:tm, :]
    ps = p + (prev - p) * mu_ref[...]

    r = ps[:, :B_WIDTH]
    k = ps[:, B_WIDTH:2 * B_WIDTH]
    v = ps[:, 2 * B_WIDTH:3 * B_WIDTH]
    lwa = ps[:, 3 * B_WIDTH:3 * B_WIDTH + LORA_W + LORA_A]
    lg = ps[:, 3 * B_WIDTH + LORA_W + LORA_A:]

    lane = lax.broadcasted_iota(jnp.int32, (1, LORA_W + LORA_A), 1)
    lora_in = jnp.where(lane < LORA_W, jnp.tanh(lwa), lwa).astype(BF16)
    lora = _dot(lora_in, wa_ref[...])
    neg = -(w0_ref[...] + lora[:, :B_WIDTH])
    softplus = jnp.maximum(neg, 0.0) + jnp.log1p(jnp.exp(-jnp.abs(neg)))
    w = -softplus - 0.5
    a = jax.nn.sigmoid(a0_ref[...] + lora[:, B_WIDTH:])
    g = _dot(jax.nn.sigmoid(lg).astype(BF16), g2_ref[...])

    kkr = k * kkw_ref[...]
    ss = _exact_rhs(kkr * kkr, seg_ref[...])
    kk = kkr * lax.rsqrt(jnp.maximum(ss, 1e-24))

    r_o[...] = r
    k_o[...] = k * (1.0 + (a - 1.0) * kaw_ref[...])
    v_o[...] = v
    kk_o[...] = kk
    b_o[...] = kk * a
    lw_o[...] = -jnp.exp(w)
    g_o[...] = g


def _head_segment_matrix():
    head = jnp.arange(B_WIDTH) // B_HEAD
    return (head[:, None] == head[None, :]).astype(BF16)


def rwkv_prep(p_b, mu, w0, w2, a0, a2, g2, k_k, k_a):
    t_len = p_b.shape[0]
    tm = min(PREP_ROW_TILE, t_len)
    zeros = jnp.zeros((LORA_W, B_WIDTH), F32)
    wa = jnp.concatenate([jnp.concatenate([w2, zeros], 1),
                          jnp.concatenate([zeros, a2], 1)], 0).astype(BF16)
    seg = _head_segment_matrix()
    row = lambda z: z.reshape(1, -1)
    full = lambda shape: pl.BlockSpec(shape, lambda i: (0,) * len(shape))
    out_sds = jax.ShapeDtypeStruct((t_len, B_WIDTH), F32)
    out_spec = pl.BlockSpec((tm, B_WIDTH), lambda i: (i, 0))
    return pl.pallas_call(
        _rwkv_prep_kernel,
        grid=(t_len // tm,),
        in_specs=[pl.BlockSpec((tm, SHIFT_W), lambda i: (i, 0)),
                  full((1, SHIFT_W)), full((1, B_WIDTH)), full((LORA_W + LORA_A, 2 * B_WIDTH)),
                  full((1, B_WIDTH)), full((LORA_G, B_WIDTH)), full((1, B_WIDTH)),
                  full((1, B_WIDTH)), full((B_WIDTH, B_WIDTH))],
        out_specs=[out_spec] * 7,
        out_shape=[out_sds] * 7,
        scratch_shapes=[pltpu.VMEM((SUBLANES, SHIFT_W), F32)],
        compiler_params=_params(1),
        name="rwkv_prep",
    )(p_b, row(mu), row(w0), wa, row(a0), g2.astype(BF16), row(k_k), row(k_a), seg)


def _rwkv_scan_kernel(r_ref, k_ref, v_ref, kk_ref, b_ref, lw_ref, g_ref, rk_ref, gng_ref, gnb_ref,
                      seg_ref, o_ref, h_ref, po_ref, q_ref, o2_ref, oacc_ref, *, n_chunks):
    c_len = RWKV_CHUNK
    n_blocks = B_WIDTH // LANES

    @pl.when(pl.program_id(0) == 0)
    def _():
        h_ref[...] = jnp.zeros_like(h_ref)

    lane = lax.broadcasted_iota(jnp.int32, (1, LANES), 1)
    head0 = lane < B_HEAD
    rho = lax.broadcasted_iota(jnp.int32, (LANES, LANES), 0)
    sig = lax.broadcasted_iota(jnp.int32, (LANES, LANES), 1)
    same_head = (rho // c_len) == (sig // c_len)
    strict = same_head & ((sig % c_len) < (rho % c_len))
    incl = same_head & ((sig % c_len) <= (rho % c_len))
    eye = rho == sig
    tm = n_chunks * c_len
    tri_r = lax.broadcasted_iota(jnp.int32, (tm, tm), 0)
    tri_c = lax.broadcasted_iota(jnp.int32, (tm, tm), 1)
    cum_mat = ((tri_r // c_len == tri_c // c_len) & (tri_c <= tri_r)).astype(BF16)

    def stack(x):
        return jnp.concatenate([jnp.where(head0, x, 0.0), jnp.where(head0, 0.0, x)], axis=0)

    lw_all = lw_ref[...]
    cum_all = _exact_lhs(cum_mat, lw_all)
    chains = []
    for c in range(n_chunks):
        rows = slice(c * c_len, (c + 1) * c_len)
        cum = cum_all[rows, :]
        total = cum[c_len - 1:c_len, :]
        grow = jnp.exp(cum)
        inv = jnp.exp(-cum)
        to_end = jnp.exp(total - cum)
        decay = jnp.exp(total)
        r_t = r_ref[rows, :] * grow
        kk_t = kk_ref[rows, :] * jnp.exp(cum - lw_all[rows, :])
        b_t = b_ref[rows, :] * inv
        k_t = k_ref[rows, :] * inv
        b_e = b_ref[rows, :] * to_end
        k_e = k_ref[rows, :] * to_end
        v = v_ref[rows, :]
        for blk in range(n_blocks):
            cols = slice(blk * LANES, (blk + 1) * LANES)
            kk_s = stack(kk_t[:, cols]).astype(BF16)
            r_s = stack(r_t[:, cols])
            a_all = _dot_nt(
                jnp.concatenate([kk_s, r_s.astype(BF16)], axis=0),
                jnp.concatenate([stack(b_t[:, cols]), stack(k_t[:, cols])], axis=0).astype(BF16))
            neg_l = -jnp.where(strict, a_all[:LANES, :LANES], 0.0)
            v_lhs = jnp.concatenate(
                [stack(k_e[:, cols]).T,
                 jnp.where(incl, a_all[LANES:, LANES:], 0.0),
                 jnp.where(strict, a_all[:LANES, LANES:], 0.0)], axis=0).astype(BF16)
            w_lhs = jnp.concatenate(
                [stack(b_e[:, cols]).T, jnp.where(incl, a_all[LANES:, :LANES], 0.0)],
                axis=0).astype(BF16)
            chains.append(dict(kk_s=kk_s, r_s=r_s, v_s=stack(v[:, cols]).astype(BF16),
                               v_lhs=v_lhs, w_lhs=w_lhs, decay=decay[:, cols],
                               power=neg_l.astype(BF16),
                               t_inv=jnp.where(eye, 1.0, 0.0) + neg_l))

    for ch in chains:
        ch["power"] = _dot(ch["power"], ch["power"]).astype(BF16)
    for level in range(1, NEUMANN_LEVELS):
        for ch in chains:
            both = _dot(jnp.concatenate([ch["power"], ch["t_inv"].astype(BF16)], axis=0),
                        ch["power"])
            ch["power"] = both[:LANES].astype(BF16)
            ch["t_inv"] = ch["t_inv"] + both[LANES:]
    for ch in chains:
        ch["t_inv"] = (ch["t_inv"] + _dot(ch["t_inv"].astype(BF16), ch["power"])).astype(BF16)

    for ch in chains:
        ch["xv"] = _dot(ch["v_lhs"], ch["v_s"])
    for ch in chains:
        rhs = jnp.concatenate([ch["kk_s"], ch["xv"][2 * LANES:].astype(BF16)], axis=1)
        ch["w12"] = _dot(ch["t_inv"], rhs).astype(BF16)
    for idx, ch in enumerate(chains):
        xw = _dot(ch["w_lhs"], ch["w12"])
        p_mat = jnp.where(eye, ch["decay"], 0.0) - xw[:LANES, :LANES]
        o1 = ch["r_s"] - xw[LANES:, :LANES]
        po_ref[idx] = jnp.concatenate([p_mat, o1], axis=0).astype(BF16)
        q_ref[idx] = ch["xv"][:LANES] - xw[:LANES, LANES:]
        o2_ref[idx] = ch["xv"][LANES:2 * LANES] - xw[LANES:, LANES:]

    states = [h_ref[blk] for blk in range(n_blocks)]
    for c in range(n_chunks):
        rows = slice(c * c_len, (c + 1) * c_len)
        pos = [_dot(po_ref[c * n_blocks + blk], states[blk].astype(BF16))
               for blk in range(n_blocks)]
        for blk in range(n_blocks):
            chain = c * n_blocks + blk
            states[blk] = pos[blk][:LANES] + q_ref[chain]
            o_st = pos[blk][LANES:] + o2_ref[chain]
            oacc_ref[rows, blk * LANES:(blk + 1) * LANES] = o_st[:c_len] + o_st[c_len:]
    for blk in range(n_blocks):
        h_ref[blk] = states[blk]

    o = oacc_ref[...]
    seg = seg_ref[...]
    mean = _exact_rhs(o, seg) * (1.0 / B_HEAD)
    cen = o - mean
    var = _exact_rhs(cen * cen, seg) * (1.0 / B_HEAD)
    normed = cen * lax.rsqrt(var + GN_EPS) * gng_ref[...] + gnb_ref[...]
    bonus = _exact_rhs(r_ref[...] * k_ref[...] * rk_ref[...], seg) * v_ref[...]
    o_ref[...] = ((normed + bonus) * g_ref[...]).astype(o_ref.dtype)


def rwkv_scan(r, k, v, kk, b, lw, g, r_k, gn_g, gn_b):
    t_len = r.shape[0]
    n_chunks = RWKV_CHUNKS_PER_STEP
    tm = n_chunks * RWKV_CHUNK
    n_blocks = B_WIDTH // LANES
    n_chains = n_chunks * n_blocks
    row = lambda z: z.reshape(1, B_WIDTH)
    tile = pl.BlockSpec((tm, B_WIDTH), lambda i: (i, 0))
    vec = pl.BlockSpec((1, B_WIDTH), lambda i: (0, 0))
    return pl.pallas_call(
        functools.partial(_rwkv_scan_kernel, n_chunks=n_chunks),
        grid=(t_len // tm,),
        in_specs=[tile] * 7 + [vec] * 3 + [pl.BlockSpec((B_WIDTH, B_WIDTH), lambda i: (0, 0))],
        out_specs=tile,
        out_shape=jax.ShapeDtypeStruct((t_len, B_WIDTH), BF16),
        scratch_shapes=[pltpu.VMEM((n_blocks, LANES, LANES), F32),
                        pltpu.VMEM((n_chains, 2 * LANES, LANES), BF16),
                        pltpu.VMEM((n_chains, LANES, LANES), F32),
                        pltpu.VMEM((n_chains, LANES, LANES), F32),
                        pltpu.VMEM((tm, B_WIDTH), F32)],
        compiler_params=_params(1),
        name="rwkv_scan",
    )(r, k, v, kk, b, lw, g, row(r_k), row(gn_g), row(gn_b), _head_segment_matrix())


def _proj_residual_kernel(x_ref, ya_ref, yb_ref, wa_ref, wb_ref, o_ref):
    o_ref[...] = x_ref[...] + _dot(ya_ref[...], wa_ref[...]) + _dot(yb_ref[...], wb_ref[...])


def proj_residual(x, y_a, y_b, w_out):
    t_len, d = x.shape
    tm = min(ROW_TILE, t_len)
    w = w_out.astype(BF16)
    half = pl.BlockSpec((tm, A_WIDTH), lambda i: (i, 0))
    wspec = pl.BlockSpec((A_WIDTH, d), lambda i: (0, 0))
    return pl.pallas_call(
        _proj_residual_kernel,
        grid=(t_len // tm,),
        in_specs=[pl.BlockSpec((tm, d), lambda i: (i, 0)), half, half, wspec, wspec],
        out_specs=pl.BlockSpec((tm, d), lambda i: (i, 0)),
        out_shape=jax.ShapeDtypeStruct((t_len, d), F32),
        compiler_params=_params(1),
        name="proj_residual",
    )(x, y_a, y_b, w[:A_WIDTH], w[A_WIDTH:])


def _attn_kernel(q_ref, kp_ref, kc_ref, vp_ref, vc_ref, o_ref, lse_ref, o_tmp, lse_tmp, *, dil):
    blk = ATTN_BLOCK
    i = pl.program_id(0)
    res = pl.program_id(1)
    row = lax.broadcasted_iota(jnp.int32, (2 * blk, 2 * blk), 0) % blk
    col = lax.broadcasted_iota(jnp.int32, (2 * blk, 2 * blk), 1)
    dist = row + blk - col
    valid = (dist >= 0) & (dist <= blk) & ((col >= blk) | (i > 0))
    lane = lax.broadcasted_iota(jnp.int32, (1, LANES), 1)
    head0 = lane < C_HEAD
    lse_acc = jnp.zeros((blk, LANES), F32)
    for pair in range(C_HEADS // HEADS_PER_BLOCK):
        cols = slice(pair * LANES, (pair + 1) * LANES)
        q2 = q_ref[:, cols] * (C_HEAD ** -0.5)
        zero = jnp.zeros_like(q2)
        qs = jnp.concatenate([jnp.where(head0, q2, zero), jnp.where(head0, zero, q2)], axis=0)
        k2 = jnp.concatenate([kp_ref[:, cols], kc_ref[:, cols]], axis=0)
        v2 = jnp.concatenate([vp_ref[:, cols], vc_ref[:, cols]], axis=0)
        s = jnp.where(valid, _dot_nt(qs, k2), NEG_INF)
        mx = jnp.max(s, axis=-1, keepdims=True)
        e = jnp.exp(s - mx)
        den = jnp.sum(e, axis=-1, keepdims=True)
        pv = _dot(e.astype(BF16), v2) / den
        lse = mx + jnp.log(den)
        o_tmp[:, cols] = jnp.where(head0, pv[:blk], pv[blk:])
        lse_acc = (lse_acc + jnp.where(lane == pair * HEADS_PER_BLOCK, lse[:blk], 0.0)
                   + jnp.where(lane == pair * HEADS_PER_BLOCK + 1, lse[blk:], 0.0))
    lse_tmp[...] = lse_acc
    for static_res in range(dil):
        @pl.when(res == static_res)
        def _():
            for c in range(o_ref.shape[0]):
                o_ref[c, pl.ds(static_res, blk, stride=dil), :] = o_tmp[:, c * LANES:(c + 1) * LANES]
            lse_ref[pl.ds(static_res, blk, stride=dil), :] = lse_tmp[...]


def dilated_attention_pattern(view, dil):
    sub = view.shape[0]
    t_len = sub * dil
    d = D_MODEL
    blk = ATTN_BLOCK
    cur = lambda part: pl.BlockSpec((blk, d), lambda i, r: (i, 3 * r + part))
    prv = lambda part: pl.BlockSpec((blk, d), lambda i, r: (jnp.maximum(i - 1, 0), 3 * r + part))
    return pl.pallas_call(
        functools.partial(_attn_kernel, dil=dil),
        grid=(sub // blk, dil),
        in_specs=[cur(0), prv(1), cur(1), prv(2), cur(2)],
        out_specs=[pl.BlockSpec((d // LANES, blk * dil, LANES), lambda i, r: (0, i, 0)),
                   pl.BlockSpec((blk * dil, LANES), lambda i, r: (i, 0))],
        out_shape=[jax.ShapeDtypeStruct((d // LANES, t_len, LANES), F32),
                   jax.ShapeDtypeStruct((t_len, LANES), F32)],
        scratch_shapes=[pltpu.VMEM((blk, d), F32), pltpu.VMEM((blk, LANES), F32)],
        compiler_params=_params(2),
        name=f"dilated_attn_d{dil}",
    )(view, view, view, view, view)


def _attn_combine_proj_kernel(x_ref, o1_ref, o2_ref, o3_ref, l1_ref, l2_ref, l3_ref, ex_ref, w_ref,
                              o_ref):
    lses = [l1_ref[...], l2_ref[...], l3_ref[...]]
    mx = jnp.maximum(jnp.maximum(lses[0], lses[1]), lses[2])
    es = [jnp.exp(l - mx) for l in lses]
    den = es[0] + es[1] + es[2]
    y = jnp.zeros(o_ref.shape, F32)
    for e, o in zip(es, (o1_ref, o2_ref, o3_ref)):
        o_full = jnp.concatenate([o[c] for c in range(o.shape[0])], axis=-1)
        y = y + _exact_rhs(e / den, ex_ref[...]) * o_full
    o_ref[...] = x_ref[...] + _dot(y.astype(BF16), w_ref[...])


def attn_combine_proj(x, outs, lses, w_out):
    t_len, d = x.shape
    tm = min(ROW_TILE, t_len)
    head_of_lane = jnp.arange(d) // C_HEAD
    expand = (jnp.arange(LANES)[:, None] == head_of_lane[None, :]).astype(BF16)
    wide = pl.BlockSpec((tm, d), lambda i: (i, 0))
    chunked = pl.BlockSpec((d // LANES, tm, LANES), lambda i: (0, i, 0))
    narrow = pl.BlockSpec((tm, LANES), lambda i: (i, 0))
    return pl.pallas_call(
        _attn_combine_proj_kernel,
        grid=(t_len // tm,),
        in_specs=[wide, chunked, chunked, chunked, narrow, narrow, narrow,
                  pl.BlockSpec((LANES, d), lambda i: (0, 0)),
                  pl.BlockSpec((d, d), lambda i: (0, 0))],
        out_specs=wide,
        out_shape=jax.ShapeDtypeStruct((t_len, d), F32),
        compiler_params=_params(1),
        name="attn_combine_proj",
    )(x, *outs, *lses, expand, w_out.astype(BF16))


def _ffn_kernel(x_ref, g_ref, wg_ref, wv_ref, cwg_ref, cwv_ref, cbg_ref, cbv_ref, wd_ref, fn_ref,
                o_ref, h_ref, zg0_ref, zg1_ref, zv0_ref, zv1_ref, carry_g_ref, carry_v_ref, *,
                final_norm):
    tm = x_ref.shape[0]
    n_j = wg_ref.shape[0]
    halo = SUBLANES
    zg_refs = (zg0_ref, zg1_ref)
    zv_refs = (zv0_ref, zv1_ref)

    @pl.when(pl.program_id(0) == 0)
    def _():
        carry_g_ref[...] = jnp.zeros_like(carry_g_ref)
        carry_v_ref[...] = jnp.zeros_like(carry_v_ref)

    x = x_ref[...]
    h_ref[...] = _rmsnorm(x, g_ref[...]).astype(BF16)
    o_ref[...] = x

    def up(j, slot):
        h = h_ref[...]
        for w_ref, z_ref, carry_ref in ((wg_ref, zg_refs[slot], carry_g_ref),
                                        (wv_ref, zv_refs[slot], carry_v_ref)):
            z = _dot(h, w_ref[j])
            z_ref[0:halo, :] = carry_ref[j]
            z_ref[halo:halo + tm, :] = z
            carry_ref[j] = z[tm - halo:tm, :]

    def finish(j, slot):
        def conv(z_ref, cw_ref, cb_ref):
            z = z_ref[halo:halo + tm, :]
            z1 = z_ref[halo - 1:halo - 1 + tm, :]
            z2 = z_ref[halo - 2:halo - 2 + tm, :]
            cw = cw_ref[j]
            return cw[0:1, :] * z2 + cw[1:2, :] * z1 + cw[2:3, :] * z + cb_ref[j]

        gate = conv(zg_refs[slot], cwg_ref, cbg_ref)
        val = conv(zv_refs[slot], cwv_ref, cbv_ref)
        act = (gate * jax.nn.sigmoid(gate) * val).astype(BF16)
        o_ref[...] += _dot(act, wd_ref[j])

    up(0, 0)

    def body(pair, carry):
        j = 2 * pair
        up(j + 1, 1)
        finish(j, 0)
        up(j + 2, 0)
        finish(j + 1, 1)
        return carry

    assert n_j % 2 == 1
    lax.fori_loop(0, (n_j - 1) // 2, body, 0)
    finish(n_j - 1, 0)

    if final_norm:
        o_ref[...] = _rmsnorm(o_ref[...], fn_ref[...])


def conv_glu_ffn(x, norm_g, w_up, conv_w, conv_b, w_down, final_g, final_norm):
    t_len, d = x.shape
    tm = min(FFN_ROW_TILE, t_len)
    tf = FFN_COL_TILE
    n_j = D_FF // tf
    tiles = lambda w: w.reshape(w.shape[0], 2, n_j, tf).transpose(1, 2, 0, 3)
    w_up_t = tiles(w_up.astype(BF16))
    conv_w_t = tiles(conv_w)
    conv_b_t = tiles(conv_b.reshape(1, 2 * D_FF))
    w_down_t = w_down.astype(BF16).reshape(n_j, tf, d)
    full = lambda shape: pl.BlockSpec(shape, lambda i: (0,) * len(shape))
    return pl.pallas_call(
        functools.partial(_ffn_kernel, final_norm=final_norm),
        grid=(t_len // tm,),
        in_specs=[pl.BlockSpec((tm, d), lambda i: (i, 0)),
                  full((1, d)),
                  full((n_j, d, tf)), full((n_j, d, tf)),
                  full((n_j, 3, tf)), full((n_j, 3, tf)),
                  full((n_j, 1, tf)), full((n_j, 1, tf)),
                  full((n_j, tf, d)),
                  full((1, d))],
        out_specs=pl.BlockSpec((tm, d), lambda i: (i, 0)),
        out_shape=jax.ShapeDtypeStruct((t_len, d), F32),
        scratch_shapes=[pltpu.VMEM((tm, d), BF16)]
                       + [pltpu.VMEM((tm + SUBLANES, tf), F32)] * 4
                       + [pltpu.VMEM((n_j, SUBLANES, tf), F32),
                        pltpu.VMEM((n_j, SUBLANES, tf), F32)],
        compiler_params=_params(1),
        name="conv_glu_ffn",
    )(x, norm_g.reshape(1, d), w_up_t[0], w_up_t[1], conv_w_t[0], conv_w_t[1], conv_b_t[0],
      conv_b_t[1], w_down_t, final_g.reshape(1, d))


def even_mixer(x, norm, w_in, ln_g, ln_b, w_s, b_s, mu, w0, w2, a0, a2, g2, k_k, k_a, r_k, gn_g,
               gn_b, w_out):
    w_in = w_in.astype(BF16)
    p_a = norm_matmul(x, norm, w_in[:, :2 * A_WIDTH], F32)
    p_b = norm_matmul(x, norm, w_in[:, 2 * A_WIDTH:], F32)
    y_a = gmlp(p_a, ln_g, ln_b, w_s, b_s)
    r, k, v, kk, b, lw, g = rwkv_prep(p_b, mu, w0, w2, a0, a2, g2, k_k, k_a)
    y_b = rwkv_scan(r, k, v, kk, b, lw, g, r_k, gn_g, gn_b)
    return proj_residual(x, y_a, y_b, w_out)


def odd_mixer(x, norm, w_qkv, w_out):
    dilations = tuple(dil for _, dil in C_PATTERNS)
    views = norm_matmul_views(x, norm, w_qkv.astype(BF16), dilations)
    outs, lses = [], []
    for view, dil in zip(views, dilations):
        o, l = dilated_attention_pattern(view, dil)
        outs.append(o)
        lses.append(l)
    return attn_combine_proj(x, outs, lses, w_out)


def kernel(x, ev_norm, ev_w_in, ev_ln_g, ev_ln_b, ev_w_s, ev_b_s, ev_mu, ev_w0, ev_w2, ev_a0, ev_a2, ev_g2, ev_k_k, ev_k_a, ev_r_k, ev_gn_g, ev_gn_b, ev_w_out, od_norm, od_w_qkv, od_w_out, ff_norm, ff_w_up, ff_conv_w, ff_conv_b, ff_w_down, final_norm):
    bsz, t_len, d = x.shape
    depth = ff_norm.shape[0]
    outs = []
    for bi in range(bsz):
        xb = x[bi]
        for layer in range(depth):
            j = layer // 2
            if layer % 2 == 0:
                xb = even_mixer(xb, ev_norm[j], ev_w_in[j], ev_ln_g[j], ev_ln_b[j], ev_w_s[j],
                                ev_b_s[j], ev_mu[j], ev_w0[j], ev_w2[j], ev_a0[j], ev_a2[j],
                                ev_g2[j], ev_k_k[j], ev_k_a[j], ev_r_k[j], ev_gn_g[j], ev_gn_b[j],
                                ev_w_out[j])
            else:
                xb = odd_mixer(xb, od_norm[j], od_w_qkv[j], od_w_out[j])
            xb = conv_glu_ffn(xb, ff_norm[layer], ff_w_up[layer], ff_conv_w[layer],
                              ff_conv_b[layer], ff_w_down[layer], final_norm,
                              final_norm=(layer == depth - 1))
        outs.append(xb)
    return jnp.stack(outs, axis=0)
```

```python
import functools

import jax
import jax.numpy as jnp
from jax import lax
from jax.experimental import pallas as pl
from jax.experimental.pallas import tpu as pltpu

F32 = jnp.float32
BF16 = jnp.bfloat16

D_MODEL = 1024
A_WIDTH = 512
A_GROUP = 128
A_GROUPS = 4
A_CHUNK = 128
LN_EPS = 1e-5
B_WIDTH = 512
B_HEAD = 64
LORA_W = 64
LORA_A = 64
LORA_G = 128
GN_EPS = 64e-5
SHIFT_W = 3 * B_WIDTH + LORA_W + LORA_A + LORA_G
C_HEAD = 64
C_HEADS = 16
C_PATTERNS = ((128, 1), (512, 4), (2048, 16))
NEG_INF = -1e30
D_FF = 2816
RMS_EPS = 1e-6

LANES = 128
SUBLANES = 8
HEADS_PER_BLOCK = LANES // B_HEAD
VMEM_LIMIT_BYTES = 52 * 1024 * 1024

ROW_TILE = 512
QKV_ROW_TILE = 256
FFN_ROW_TILE = 512
FFN_COL_TILE = 256
RWKV_CHUNK = 64
RWKV_CHUNKS_PER_STEP = 4
ATTN_BLOCK = 128
NEUMANN_LEVELS = 5


def _params(n_axes):
    return pltpu.CompilerParams(dimension_semantics=("arbitrary",) * n_axes,
                                vmem_limit_bytes=VMEM_LIMIT_BYTES)


def _rmsnorm(x, g):
    return x * lax.rsqrt(jnp.mean(x * x, axis=-1, keepdims=True) + RMS_EPS) * g


def _dot(a, b):
    return jnp.dot(a, b, preferred_element_type=F32)


def _dot_nt(a, b):
    return lax.dot_general(a, b, (((1,), (1,)), ((), ())), preferred_element_type=F32)


def _split3(x):
    hi = x.astype(BF16)
    r1 = x - hi.astype(F32)
    mid = r1.astype(BF16)
    lo = (r1 - mid.astype(F32)).astype(BF16)
    return hi, mid, lo


def _exact_rhs(x, m_bf16):
    hi, mid, lo = _split3(x)
    return _dot(hi, m_bf16) + (_dot(mid, m_bf16) + _dot(lo, m_bf16))


def _exact_lhs(m_bf16, x):
    hi, mid, lo = _split3(x)
    return _dot(m_bf16, hi) + (_dot(m_bf16, mid) + _dot(m_bf16, lo))


def _norm_matmul_kernel(x_ref, g_ref, w_ref, o_ref):
    h = _rmsnorm(x_ref[...], g_ref[...]).astype(BF16)
    o_ref[...] = _dot(h, w_ref[...]).astype(o_ref.dtype)


def norm_matmul(x, g, w, out_dtype):
    t_len, d = x.shape
    n = w.shape[1]
    tm = min(ROW_TILE, t_len)
    return pl.pallas_call(
        _norm_matmul_kernel,
        grid=(t_len // tm,),
        in_specs=[pl.BlockSpec((tm, d), lambda i: (i, 0)),
                  pl.BlockSpec((1, d), lambda i: (0, 0)),
                  pl.BlockSpec((d, n), lambda i: (0, 0))],
        out_specs=pl.BlockSpec((tm, n), lambda i: (i, 0)),
        out_shape=jax.ShapeDtypeStruct((t_len, n), out_dtype),
        compiler_params=_params(1),
        name="norm_matmul",
    )(x, g.reshape(1, d), w)


def _norm_matmul_views_kernel(x_ref, g_ref, w_ref, *refs, dilations):
    o_refs, acc_ref = refs[:-1], refs[-1]
    tm = x_ref.shape[0]
    n = w_ref.shape[1]
    h = _rmsnorm(x_ref[...], g_ref[...]).astype(BF16)
    res = _dot(h, w_ref[...])
    for c in range(n // LANES):
        acc_ref[c] = res[:, c * LANES:(c + 1) * LANES]
    for dil, o_ref in zip(dilations, o_refs):
        if dil == 1:
            o_ref[...] = res.astype(o_ref.dtype)
            continue
        for r in range(dil):
            for c in range(n // LANES):
                rows = acc_ref[c, pl.ds(r, tm // dil, stride=dil), :]
                o_ref[:, r * n + c * LANES:r * n + (c + 1) * LANES] = rows.astype(o_ref.dtype)


def norm_matmul_views(x, g, w, dilations):
    t_len, d = x.shape
    n = w.shape[1]
    tm = min(QKV_ROW_TILE, t_len)
    return pl.pallas_call(
        functools.partial(_norm_matmul_views_kernel, dilations=dilations),
        grid=(t_len // tm,),
        in_specs=[pl.BlockSpec((tm, d), lambda i: (i, 0)),
                  pl.BlockSpec((1, d), lambda i: (0, 0)),
                  pl.BlockSpec((d, n), lambda i: (0, 0))],
        out_specs=[pl.BlockSpec((tm // dil, dil * n), lambda i: (i, 0)) for dil in dilations],
        out_shape=[jax.ShapeDtypeStruct((t_len // dil, dil * n), BF16) for dil in dilations],
        scratch_shapes=[pltpu.VMEM((n // LANES, tm, LANES), F32)],
        compiler_params=_params(1),
        name="norm_matmul_views",
    )(x, g.reshape(1, d), w)


def _gmlp_kernel(p_ref, lng_ref, lnb_ref, ws_ref, bs_ref, o_ref, *, n_chunks):
    row = lax.broadcasted_iota(jnp.int32, (A_CHUNK, A_CHUNK), 0)
    col = lax.broadcasted_iota(jnp.int32, (A_CHUNK, A_CHUNK), 1)
    causal = col <= row
    w_masked = [jnp.where(causal, ws_ref[g], 0.0).astype(BF16) for g in range(A_GROUPS)]
    for c in range(n_chunks):
        rows = slice(c * A_CHUNK, (c + 1) * A_CHUNK)
        u = p_ref[rows, :A_WIDTH]
        v = p_ref[rows, A_WIDTH:]
        mean = jnp.mean(v, axis=-1, keepdims=True)
        cen = v - mean
        var = jnp.mean(cen * cen, axis=-1, keepdims=True)
        vn = (cen * lax.rsqrt(var + LN_EPS) * lng_ref[...] + lnb_ref[...]).astype(BF16)
        for g in range(A_GROUPS):
            cols = slice(g * A_GROUP, (g + 1) * A_GROUP)
            mixed = _dot(w_masked[g], vn[:, cols]) + bs_ref[:, g:g + 1]
            o_ref[rows, cols] = (u[:, cols] * mixed).astype(o_ref.dtype)


def gmlp(p_a, ln_g, ln_b, w_s, b_s):
    t_len = p_a.shape[0]
    tm = min(ROW_TILE, t_len)
    return pl.pallas_call(
        functools.partial(_gmlp_kernel, n_chunks=tm // A_CHUNK),
        grid=(t_len // tm,),
        in_specs=[pl.BlockSpec((tm, 2 * A_WIDTH), lambda i: (i, 0)),
                  pl.BlockSpec((1, A_WIDTH), lambda i: (0, 0)),
                  pl.BlockSpec((1, A_WIDTH), lambda i: (0, 0)),
                  pl.BlockSpec((A_GROUPS, A_CHUNK, A_CHUNK), lambda i: (0, 0, 0)),
                  pl.BlockSpec((A_CHUNK, A_GROUPS), lambda i: (0, 0))],
        out_specs=pl.BlockSpec((tm, A_WIDTH), lambda i: (i, 0)),
        out_shape=jax.ShapeDtypeStruct((t_len, A_WIDTH), BF16),
        compiler_params=_params(1),
        name="gmlp",
    )(p_a, ln_g.reshape(1, A_WIDTH), ln_b.reshape(1, A_WIDTH), w_s, b_s.T)


def _head_sums(x):
    lane = lax.broadcasted_iota(jnp.int32, (1, LANES), 1)
    head0 = lane < B_HEAD
    outs = []
    for blk in range(x.shape[1] // LANES):
        xb = x[:, blk * LANES:(blk + 1) * LANES]
        s0 = jnp.sum(jnp.where(head0, xb, 0.0), axis=-1, keepdims=True)
        s1 = jnp.sum(jnp.where(head0, 0.0, xb), axis=-1, keepdims=True)
        outs.append(jnp.where(head0, s0, s1))
    return jnp.concatenate(outs, axis=-1)


def _rwkv_prep_tile(p, prev_row, prm, buf):
    mu_ref, w0_ref, wa_ref, a0_ref, g2_ref, kkw_ref, kaw_ref = prm
    r_o, k_o, v_o, kk_o, b_o, lw_o, g_o = buf
    tm = p.shape[0]
    rowid = lax.broadcasted_iota(jnp.int32, (tm, 1), 0)
    prev = jnp.where(rowid == 0, prev_row, pltpu.roll(p, 1, 0))
    ps = p + (prev - p) * mu_ref[...]

    r = ps[:, :B_WIDTH]
    k = ps[:, B_WIDTH:2 * B_WIDTH]
    v = ps[:, 2 * B_WIDTH:3 * B_WIDTH]
    lwa = ps[:, 3 * B_WIDTH:3 * B_WIDTH + LORA_W + LORA_A]
    lg = ps[:, 3 * B_WIDTH + LORA_W + LORA_A:]

    lane = lax.broadcasted_iota(jnp.int32, (1, LORA_W + LORA_A), 1)
    lora_in = jnp.where(lane < LORA_W, jnp.tanh(lwa), lwa).astype(BF16)
    lora = _dot(lora_in, wa_ref[...])
    neg = -(w0_ref[...] + lora[:, :B_WIDTH])
    softplus = jnp.maximum(neg, 0.0) + jnp.log1p(jnp.exp(-jnp.abs(neg)))
    w = -softplus - 0.5
    a = jax.nn.sigmoid(a0_ref[...] + lora[:, B_WIDTH:])
    g = _dot(jax.nn.sigmoid(lg).astype(BF16), g2_ref[...])

    kkr = k * kkw_ref[...]
    kk = kkr * lax.rsqrt(jnp.maximum(_head_sums(kkr * kkr), 1e-24))

    r_o[...] = r
    k_o[...] = k * (1.0 + (a - 1.0) * kaw_ref[...])
    v_o[...] = v
    kk_o[...] = kk
    b_o[...] = kk * a
    lw_o[...] = -jnp.exp(w)
    g_o[...] = g


def _rwkv_scan_tile(buf, prm, o_ref, out_rows, h_ref, po_ref, q_ref, o2_ref, oacc_ref, n_chunks):
    r_ref, k_ref, v_ref, kk_ref, b_ref, lw_ref, g_ref = buf
    rk_ref, gng_ref, gnb_ref = prm
    c_len = RWKV_CHUNK
    n_blocks = B_WIDTH // LANES

    lane = lax.broadcasted_iota(jnp.int32, (1, LANES), 1)
    head0 = lane < B_HEAD
    rho = lax.broadcasted_iota(jnp.int32, (LANES, LANES), 0)
    sig = lax.broadcasted_iota(jnp.int32, (LANES, LANES), 1)
    same_head = (rho // c_len) == (sig // c_len)
    strict = same_head & ((sig % c_len) < (rho % c_len))
    incl = same_head & ((sig % c_len) <= (rho % c_len))
    eye = rho == sig
    tm = n_chunks * c_len
    tri_r = lax.broadcasted_iota(jnp.int32, (tm, tm), 0)
    tri_c = lax.broadcasted_iota(jnp.int32, (tm, tm), 1)
    cum_mat = ((tri_r // c_len == tri_c // c_len) & (tri_c <= tri_r)).astype(BF16)

    def stack(x):
        return jnp.concatenate([jnp.where(head0, x, 0.0), jnp.where(head0, 0.0, x)], axis=0)

    lw_all = lw_ref[...]
    cum_all = _exact_lhs(cum_mat, lw_all)
    chains = []
    for c in range(n_chunks):
        rows = slice(c * c_len, (c + 1) * c_len)
        cum = cum_all[rows, :]
        total = cum[c_len - 1:c_len, :]
        grow = jnp.exp(cum)
        inv = jnp.exp(-cum)
        to_end = jnp.exp(total - cum)
        decay = jnp.exp(total)
        r_t = r_ref[rows, :] * grow
        kk_t = kk_ref[rows, :] * jnp.exp(cum - lw_all[rows, :])
        b_t = b_ref[rows, :] * inv
        k_t = k_ref[rows, :] * inv
        b_e = b_ref[rows, :] * to_end
        k_e = k_ref[rows, :] * to_end
        v = v_ref[rows, :]
        for blk in range(n_blocks):
            cols = slice(blk * LANES, (blk + 1) * LANES)
            kk_s = stack(kk_t[:, cols]).astype(BF16)
            r_s = stack(r_t[:, cols])
            a_all = _dot_nt(
                jnp.concatenate([kk_s, r_s.astype(BF16)], axis=0),
                jnp.concatenate([stack(b_t[:, cols]), stack(k_t[:, cols])], axis=0).astype(BF16))
            neg_l = -jnp.where(strict, a_all[:LANES, :LANES], 0.0)
            v_lhs = jnp.concatenate(
                [stack(k_e[:, cols]).T,
                 jnp.where(incl, a_all[LANES:, LANES:], 0.0),
                 jnp.where(strict, a_all[:LANES, LANES:], 0.0)], axis=0).astype(BF16)
            w_lhs = jnp.concatenate(
                [stack(b_e[:, cols]).T, jnp.where(incl, a_all[LANES:, :LANES], 0.0)],
                axis=0).astype(BF16)
            chains.append(dict(kk_s=kk_s, r_s=r_s, v_s=stack(v[:, cols]).astype(BF16),
                               v_lhs=v_lhs, w_lhs=w_lhs, decay=decay[:, cols],
                               power=neg_l.astype(BF16),
                               t_inv=jnp.where(eye, 1.0, 0.0) + neg_l))

    for ch in chains:
        ch["power"] = _dot(ch["power"], ch["power"]).astype(BF16)
    for level in range(1, NEUMANN_LEVELS):
        for ch in chains:
            both = _dot(jnp.concatenate([ch["power"], ch["t_inv"].astype(BF16)], axis=0),
                        ch["power"])
            ch["power"] = both[:LANES].astype(BF16)
            ch["t_inv"] = ch["t_inv"] + both[LANES:]
    for ch in chains:
        ch["t_inv"] = (ch["t_inv"] + _dot(ch["t_inv"].astype(BF16), ch["power"])).astype(BF16)

    for ch in chains:
        ch["xv"] = _dot(ch["v_lhs"], ch["v_s"])
    for ch in chains:
        rhs = jnp.concatenate([ch["kk_s"], ch["xv"][2 * LANES:].astype(BF16)], axis=1)
        ch["w12"] = _dot(ch["t_inv"], rhs).astype(BF16)
    for idx, ch in enumerate(chains):
        xw = _dot(ch["w_lhs"], ch["w12"])
        p_mat = jnp.where(eye, ch["decay"], 0.0) - xw[:LANES, :LANES]
        o1 = ch["r_s"] - xw[LANES:, :LANES]
        po_ref[idx] = jnp.concatenate([p_mat, o1], axis=0).astype(BF16)
        q_ref[idx] = ch["xv"][:LANES] - xw[:LANES, LANES:]
        o2_ref[idx] = ch["xv"][LANES:2 * LANES] - xw[LANES:, LANES:]

    states = [h_ref[blk] for blk in range(n_blocks)]
    for c in range(n_chunks):
        rows = slice(c * c_len, (c + 1) * c_len)
        pos = [_dot(po_ref[c * n_blocks + blk], states[blk].astype(BF16))
               for blk in range(n_blocks)]
        for blk in range(n_blocks):
            chain = c * n_blocks + blk
            states[blk] = pos[blk][:LANES] + q_ref[chain]
            o_st = pos[blk][LANES:] + o2_ref[chain]
            oacc_ref[rows, blk * LANES:(blk + 1) * LANES] = o_st[:c_len] + o_st[c_len:]
    for blk in range(n_blocks):
        h_ref[blk] = states[blk]

    o = oacc_ref[...]
    mean = _head_sums(o) * (1.0 / B_HEAD)
    cen = o - mean
    var = _head_sums(cen * cen) * (1.0 / B_HEAD)
    normed = cen * lax.rsqrt(var + GN_EPS) * gng_ref[...] + gnb_ref[...]
    bonus = _head_sums(r_ref[...] * k_ref[...] * rk_ref[...]) * v_ref[...]
    o_ref[out_rows, :] = ((normed + bonus) * g_ref[...]).astype(o_ref.dtype)


def _rwkv_kernel(*refs, n_chunks):
    p_ref, pn_ref = refs[0:2]
    prep_prm = refs[2:9]
    scan_prm = refs[9:12]
    o_ref = refs[12]
    buf_a = refs[13:20]
    buf_b = refs[20:27]
    h_ref, po_ref, q_ref, o2_ref, oacc_ref = refs[27:32]
    tm = n_chunks * RWKV_CHUNK
    scan_scratch = (h_ref, po_ref, q_ref, o2_ref, oacc_ref, n_chunks)

    @pl.when(pl.program_id(0) == 0)
    def _():
        h_ref[...] = jnp.zeros_like(h_ref)
        _rwkv_prep_tile(p_ref[0:tm, :], jnp.zeros((1, SHIFT_W), F32), prep_prm, buf_a)

    _rwkv_prep_tile(p_ref[tm:2 * tm, :], p_ref[tm - 1:tm, :], prep_prm, buf_b)
    _rwkv_scan_tile(buf_a, scan_prm, o_ref, slice(0, tm), *scan_scratch)
    _rwkv_prep_tile(pn_ref[...], p_ref[2 * tm - 1:2 * tm, :], prep_prm, buf_a)
    _rwkv_scan_tile(buf_b, scan_prm, o_ref, slice(tm, 2 * tm), *scan_scratch)


def rwkv_mixer(p_b, mu, w0, w2, a0, a2, g2, k_k, k_a, r_k, gn_g, gn_b):
    t_len = p_b.shape[0]
    n_chunks = RWKV_CHUNKS_PER_STEP
    tm = n_chunks * RWKV_CHUNK
    n_tiles = t_len // tm
    n_blocks = B_WIDTH // LANES
    n_chains = n_chunks * n_blocks
    zeros = jnp.zeros((LORA_W, B_WIDTH), F32)
    wa = jnp.concatenate([jnp.concatenate([w2, zeros], 1),
                          jnp.concatenate([zeros, a2], 1)], 0).astype(BF16)
    row = lambda z: z.reshape(1, -1)
    full = lambda shape: pl.BlockSpec(shape, lambda i: (0,) * len(shape))
    vec = full((1, B_WIDTH))
    tile_buf = [pltpu.VMEM((tm, B_WIDTH), F32)] * 7
    return pl.pallas_call(
        functools.partial(_rwkv_kernel, n_chunks=n_chunks),
        grid=(n_tiles // 2,),
        in_specs=[pl.BlockSpec((2 * tm, SHIFT_W), lambda i: (i, 0)),
                  pl.BlockSpec((tm, SHIFT_W), lambda i: (jnp.minimum(2 * i + 2, n_tiles - 1), 0)),
                  full((1, SHIFT_W)), vec, full((LORA_W + LORA_A, 2 * B_WIDTH)), vec,
                  full((LORA_G, B_WIDTH)), vec, vec,
                  vec, vec, vec],
        out_specs=pl.BlockSpec((2 * tm, B_WIDTH), lambda i: (i, 0)),
        out_shape=jax.ShapeDtypeStruct((t_len, B_WIDTH), BF16),
        scratch_shapes=tile_buf + tile_buf
                       + [pltpu.VMEM((n_blocks, LANES, LANES), F32),
                          pltpu.VMEM((n_chains, 2 * LANES, LANES), BF16),
                          pltpu.VMEM((n_chains, LANES, LANES), F32),
                          pltpu.VMEM((n_chains, LANES, LANES), F32),
                          pltpu.VMEM((tm, B_WIDTH), F32)],
        compiler_params=_params(1),
        name="rwkv_mixer",
    )(p_b, p_b, row(mu), row(w0), wa, row(a0), g2.astype(BF16), row(k_k), row(k_a),
      row(r_k), row(gn_g), row(gn_b))


def _proj_residual_kernel(x_ref, ya_ref, yb_ref, wa_ref, wb_ref, o_ref):
    o_ref[...] = x_ref[...] + _dot(ya_ref[...], wa_ref[...]) + _dot(yb_ref[...], wb_ref[...])


def proj_residual(x, y_a, y_b, w_out):
    t_len, d = x.shape
    tm = min(ROW_TILE, t_len)
    w = w_out.astype(BF16)
    half = pl.BlockSpec((tm, A_WIDTH), lambda i: (i, 0))
    wspec = pl.BlockSpec((A_WIDTH, d), lambda i: (0, 0))
    return pl.pallas_call(
        _proj_residual_kernel,
        grid=(t_len // tm,),
        in_specs=[pl.BlockSpec((tm, d), lambda i: (i, 0)), half, half, wspec, wspec],
        out_specs=pl.BlockSpec((tm, d), lambda i: (i, 0)),
        out_shape=jax.ShapeDtypeStruct((t_len, d), F32),
        compiler_params=_params(1),
        name="proj_residual",
    )(x, y_a, y_b, w[:A_WIDTH], w[A_WIDTH:])


def _attn_kernel(q_ref, kp_ref, kc_ref, vp_ref, vc_ref, o_ref, lse_ref, o_tmp, lse_tmp, *, dil):
    blk = ATTN_BLOCK
    i = pl.program_id(0)
    res = pl.program_id(1)
    row = lax.broadcasted_iota(jnp.int32, (2 * blk, 2 * blk), 0) % blk
    col = lax.broadcasted_iota(jnp.int32, (2 * blk, 2 * blk), 1)
    dist = row + blk - col
    valid = (dist >= 0) & (dist <= blk) & ((col >= blk) | (i > 0))
    lane = lax.broadcasted_iota(jnp.int32, (1, LANES), 1)
    head0 = lane < C_HEAD
    lse_acc = jnp.zeros((blk, LANES), F32)
    for pair in range(C_HEADS // HEADS_PER_BLOCK):
        cols = slice(pair * LANES, (pair + 1) * LANES)
        q2 = q_ref[:, cols] * (C_HEAD ** -0.5)
        zero = jnp.zeros_like(q2)
        qs = jnp.concatenate([jnp.where(head0, q2, zero), jnp.where(head0, zero, q2)], axis=0)
        k2 = jnp.concatenate([kp_ref[:, cols], kc_ref[:, cols]], axis=0)
        v2 = jnp.concatenate([vp_ref[:, cols], vc_ref[:, cols]], axis=0)
        s = jnp.where(valid, _dot_nt(qs, k2), NEG_INF)
        mx = jnp.max(s, axis=-1, keepdims=True)
        e = jnp.exp(s - mx)
        den = jnp.sum(e, axis=-1, keepdims=True)
        pv = _dot(e.astype(BF16), v2) / den
        lse = mx + jnp.log(den)
        o_tmp[:, cols] = jnp.where(head0, pv[:blk], pv[blk:])
        lse_acc = (lse_acc + jnp.where(lane == pair * HEADS_PER_BLOCK, lse[:blk], 0.0)
                   + jnp.where(lane == pair * HEADS_PER_BLOCK + 1, lse[blk:], 0.0))
    lse_tmp[...] = lse_acc
    for static_res in range(dil):
        @pl.when(res == static_res)
        def _():
            for c in range(o_ref.shape[0]):
                o_ref[c, pl.ds(static_res, blk, stride=dil), :] = o_tmp[:, c * LANES:(c + 1) * LANES]
            lse_ref[pl.ds(static_res, blk, stride=dil), :] = lse_tmp[...]


def dilated_attention_pattern(view, dil):
    sub = view.shape[0]
    t_len = sub * dil
    d = D_MODEL
    blk = ATTN_BLOCK
    cur = lambda part: pl.BlockSpec((blk, d), lambda i, r: (i, 3 * r + part))
    prv = lambda part: pl.BlockSpec((blk, d), lambda i, r: (jnp.maximum(i - 1, 0), 3 * r + part))
    return pl.pallas_call(
        functools.partial(_attn_kernel, dil=dil),
        grid=(sub // blk, dil),
        in_specs=[cur(0), prv(1), cur(1), prv(2), cur(2)],
        out_specs=[pl.BlockSpec((d // LANES, blk * dil, LANES), lambda i, r: (0, i, 0)),
                   pl.BlockSpec((blk * dil, LANES), lambda i, r: (i, 0))],
        out_shape=[jax.ShapeDtypeStruct((d // LANES, t_len, LANES), F32),
                   jax.ShapeDtypeStruct((t_len, LANES), F32)],
        scratch_shapes=[pltpu.VMEM((blk, d), F32), pltpu.VMEM((blk, LANES), F32)],
        compiler_params=_params(2),
        name=f"dilated_attn_d{dil}",
    )(view, view, view, view, view)


def _attn_combine_proj_kernel(x_ref, o1_ref, o2_ref, o3_ref, l1_ref, l2_ref, l3_ref, ex_ref, w_ref,
                              o_ref):
    lses = [l1_ref[...], l2_ref[...], l3_ref[...]]
    mx = jnp.maximum(jnp.maximum(lses[0], lses[1]), lses[2])
    es = [jnp.exp(l - mx) for l in lses]
    den = es[0] + es[1] + es[2]
    y = jnp.zeros(o_ref.shape, F32)
    for e, o in zip(es, (o1_ref, o2_ref, o3_ref)):
        o_full = jnp.concatenate([o[c] for c in range(o.shape[0])], axis=-1)
        y = y + _exact_rhs(e / den, ex_ref[...]) * o_full
    o_ref[...] = x_ref[...] + _dot(y.astype(BF16), w_ref[...])


def attn_combine_proj(x, outs, lses, w_out):
    t_len, d = x.shape
    tm = min(ROW_TILE, t_len)
    head_of_lane = jnp.arange(d) // C_HEAD
    expand = (jnp.arange(LANES)[:, None] == head_of_lane[None, :]).astype(BF16)
    wide = pl.BlockSpec((tm, d), lambda i: (i, 0))
    chunked = pl.BlockSpec((d // LANES, tm, LANES), lambda i: (0, i, 0))
    narrow = pl.BlockSpec((tm, LANES), lambda i: (i, 0))
    return pl.pallas_call(
        _attn_combine_proj_kernel,
        grid=(t_len // tm,),
        in_specs=[wide, chunked, chunked, chunked, narrow, narrow, narrow,
                  pl.BlockSpec((LANES, d), lambda i: (0, 0)),
                  pl.BlockSpec((d, d), lambda i: (0, 0))],
        out_specs=wide,
        out_shape=jax.ShapeDtypeStruct((t_len, d), F32),
        compiler_params=_params(1),
        name="attn_combine_proj",
    )(x, *outs, *lses, expand, w_out.astype(BF16))


def _ffn_kernel(x_ref, g_ref, wg_ref, wv_ref, cwg_ref, cwv_ref, cbg_ref, cbv_ref, wd_ref, fn_ref,
                o_ref, h_ref, zg0_ref, zg1_ref, zv0_ref, zv1_ref, carry_g_ref, carry_v_ref, *,
                final_norm):
    tm = x_ref.shape[0]
    n_j = wg_ref.shape[0]
    halo = SUBLANES
    zg_refs = (zg0_ref, zg1_ref)
    zv_refs = (zv0_ref, zv1_ref)

    @pl.when(pl.program_id(0) == 0)
    def _():
        carry_g_ref[...] = jnp.zeros_like(carry_g_ref)
        carry_v_ref[...] = jnp.zeros_like(carry_v_ref)

    x = x_ref[...]
    h_ref[...] = _rmsnorm(x, g_ref[...]).astype(BF16)
    o_ref[...] = x

    def up(j, slot):
        h = h_ref[...]
        for w_ref, z_ref, carry_ref in ((wg_ref, zg_refs[slot], carry_g_ref),
                                        (wv_ref, zv_refs[slot], carry_v_ref)):
            z = _dot(h, w_ref[j])
            z_ref[0:halo, :] = carry_ref[j]
            z_ref[halo:halo + tm, :] = z
            carry_ref[j] = z[tm - halo:tm, :]

    def finish(j, slot):
        def conv(z_ref, cw_ref, cb_ref):
            z = z_ref[halo:halo + tm, :]
            z1 = z_ref[halo - 1:halo - 1 + tm, :]
            z2 = z_ref[halo - 2:halo - 2 + tm, :]
            cw = cw_ref[j]
            return cw[0:1, :] * z2 + cw[1:2, :] * z1 + cw[2:3, :] * z + cb_ref[j]

        gate = conv(zg_refs[slot], cwg_ref, cbg_ref)
        val = conv(zv_refs[slot], cwv_ref, cbv_ref)
        act = (gate * jax.nn.sigmoid(gate) * val).astype(BF16)
        o_ref[...] += _dot(act, wd_ref[j])

    up(0, 0)

    def body(pair, carry):
        j = 2 * pair
        up(j + 1, 1)
        finish(j, 0)
        up(j + 2, 0)
        finish(j + 1, 1)
        return carry

    assert n_j % 2 == 1
    lax.fori_loop(0, (n_j - 1) // 2, body, 0)
    finish(n_j - 1, 0)

    if final_norm:
        o_ref[...] = _rmsnorm(o_ref[...], fn_ref[...])


def conv_glu_ffn(x, norm_g, w_up, conv_w, conv_b, w_down, final_g, final_norm):
    t_len, d = x.shape
    tm = min(FFN_ROW_TILE, t_len)
    tf = FFN_COL_TILE
    n_j = D_FF // tf
    tiles = lambda w: w.reshape(w.shape[0], 2, n_j, tf).transpose(1, 2, 0, 3)
    w_up_t = tiles(w_up.astype(BF16))
    conv_w_t = tiles(conv_w)
    conv_b_t = tiles(conv_b.reshape(1, 2 * D_FF))
    w_down_t = w_down.astype(BF16).reshape(n_j, tf, d)
    full = lambda shape: pl.BlockSpec(shape, lambda i: (0,) * len(shape))
    return pl.pallas_call(
        functools.partial(_ffn_kernel, final_norm=final_norm),
        grid=(t_len // tm,),
        in_specs=[pl.BlockSpec((tm, d), lambda i: (i, 0)),
                  full((1, d)),
                  full((n_j, d, tf)), full((n_j, d, tf)),
                  full((n_j, 3, tf)), full((n_j, 3, tf)),
                  full((n_j, 1, tf)), full((n_j, 1, tf)),
                  full((n_j, tf, d)),
                  full((1, d))],
        out_specs=pl.BlockSpec((tm, d), lambda i: (i, 0)),
        out_shape=jax.ShapeDtypeStruct((t_len, d), F32),
        scratch_shapes=[pltpu.VMEM((tm, d), BF16)]
                       + [pltpu.VMEM((tm + SUBLANES, tf), F32)] * 4
                       + [pltpu.VMEM((n_j, SUBLANES, tf), F32),
                        pltpu.VMEM((n_j, SUBLANES, tf), F32)],
        compiler_params=_params(1),
        name="conv_glu_ffn",
    )(x, norm_g.reshape(1, d), w_up_t[0], w_up_t[1], conv_w_t[0], conv_w_t[1], conv_b_t[0],
      conv_b_t[1], w_down_t, final_g.reshape(1, d))


def even_mixer(x, norm, w_in, ln_g, ln_b, w_s, b_s, mu, w0, w2, a0, a2, g2, k_k, k_a, r_k, gn_g,
               gn_b, w_out):
    w_in = w_in.astype(BF16)
    p_a = norm_matmul(x, norm, w_in[:, :2 * A_WIDTH], F32)
    p_b = norm_matmul(x, norm, w_in[:, 2 * A_WIDTH:], F32)
    y_a = gmlp(p_a, ln_g, ln_b, w_s, b_s)
    y_b = rwkv_mixer(p_b, mu, w0, w2, a0, a2, g2, k_k, k_a, r_k, gn_g, gn_b)
    return proj_residual(x, y_a, y_b, w_out)


def odd_mixer(x, norm, w_qkv, w_out):
    dilations = tuple(dil for _, dil in C_PATTERNS)
    views = norm_matmul_views(x, norm, w_qkv.astype(BF16), dilations)
    outs, lses = [], []
    for view, dil in zip(views, dilations):
        o, l = dilated_attention_pattern(view, dil)
        outs.append(o)
        lses.append(l)
    return attn_combine_proj(x, outs, lses, w_out)


def kernel(x, ev_norm, ev_w_in, ev_ln_g, ev_ln_b, ev_w_s, ev_b_s, ev_mu, ev_w0, ev_w2, ev_a0, ev_a2, ev_g2, ev_k_k, ev_k_a, ev_r_k, ev_gn_g, ev_gn_b, ev_w_out, od_norm, od_w_qkv, od_w_out, ff_norm, ff_w_up, ff_conv_w, ff_conv_b, ff_w_down, final_norm):
    bsz, t_len, d = x.shape
    depth = ff_norm.shape[0]
    outs = []
    for bi in range(bsz):
        xb = x[bi]
        for layer in range(depth):
            j = layer // 2
            if layer % 2 == 0:
                xb = even_mixer(xb, ev_norm[j], ev_w_in[j], ev_ln_g[j], ev_ln_b[j], ev_w_s[j],
                                ev_b_s[j], ev_mu[j], ev_w0[j], ev_w2[j], ev_a0[j], ev_a2[j],
                                ev_g2[j], ev_k_k[j], ev_k_a[j], ev_r_k[j], ev_gn_g[j], ev_gn_b[j],
                                ev_w_out[j])
            else:
                xb = odd_mixer(xb, od_norm[j], od_w_qkv[j], od_w_out[j])
            xb = conv_glu_ffn(xb, ff_norm[layer], ff_w_up[layer], ff_conv_w[layer],
                              ff_conv_b[layer], ff_w_down[layer], final_norm,
                              final_norm=(layer == depth - 1))
        outs.append(xb)
    return jnp.stack(outs, axis=0)
```

```python
import functools

import jax
import jax.numpy as jnp
from jax import lax
from jax.experimental import pallas as pl
from jax.experimental.pallas import tpu as pltpu

F32 = jnp.float32
BF16 = jnp.bfloat16

D_MODEL = 1024
A_WIDTH = 512
A_GROUP = 128
A_GROUPS = 4
A_CHUNK = 128
LN_EPS = 1e-5
B_WIDTH = 512
B_HEAD = 64
LORA_W = 64
LORA_A = 64
LORA_G = 128
GN_EPS = 64e-5
SHIFT_W = 3 * B_WIDTH + LORA_W + LORA_A + LORA_G
C_HEAD = 64
C_HEADS = 16
C_PATTERNS = ((128, 1), (512, 4), (2048, 16))
NEG_INF = -1e30
D_FF = 2816
RMS_EPS = 1e-6

LANES = 128
SUBLANES = 8
HEADS_PER_BLOCK = LANES // B_HEAD
VMEM_LIMIT_BYTES = 52 * 1024 * 1024

ROW_TILE = 512
QKV_ROW_TILE = 256
FFN_ROW_TILE = 512
FFN_COL_TILE = 256
RWKV_CHUNK = 64
RWKV_CHUNKS_PER_STEP = 4
ATTN_BLOCK = 128
NEUMANN_LEVELS = 5


def _params(n_axes):
    return pltpu.CompilerParams(dimension_semantics=("arbitrary",) * n_axes,
                                vmem_limit_bytes=VMEM_LIMIT_BYTES)


def _rmsnorm(x, g):
    return x * lax.rsqrt(jnp.mean(x * x, axis=-1, keepdims=True) + RMS_EPS) * g


def _dot(a, b):
    return jnp.dot(a, b, preferred_element_type=F32)


def _dot_nt(a, b):
    return lax.dot_general(a, b, (((1,), (1,)), ((), ())), preferred_element_type=F32)


def _split3(x):
    hi = x.astype(BF16)
    r1 = x - hi.astype(F32)
    mid = r1.astype(BF16)
    lo = (r1 - mid.astype(F32)).astype(BF16)
    return hi, mid, lo


def _exact_rhs(x, m_bf16):
    hi, mid, lo = _split3(x)
    return _dot(hi, m_bf16) + (_dot(mid, m_bf16) + _dot(lo, m_bf16))


def _exact_lhs(m_bf16, x):
    hi, mid, lo = _split3(x)
    return _dot(m_bf16, hi) + (_dot(m_bf16, mid) + _dot(m_bf16, lo))


def _norm_matmul_views_kernel(x_ref, g_ref, w_ref, *o_refs, dilations):
    tm = x_ref.shape[0]
    n = w_ref.shape[1]
    h = _rmsnorm(x_ref[...], g_ref[...]).astype(BF16)
    res = _dot(h, w_ref[...]).astype(BF16)
    out_row = lax.broadcasted_iota(jnp.int32, (tm, tm), 0)
    in_row = lax.broadcasted_iota(jnp.int32, (tm, tm), 1)
    for dil, o_ref in zip(dilations, o_refs):
        if dil == 1:
            o_ref[...] = res
            continue
        per = tm // dil
        assert per & (per - 1) == 0
        shift = per.bit_length() - 1
        src_row = (out_row & (per - 1)) * dil + (out_row >> shift)
        grouped = _dot((in_row == src_row).astype(BF16), res).astype(BF16)
        for r in range(dil):
            o_ref[:, r * n:(r + 1) * n] = grouped[r * per:(r + 1) * per, :]


def norm_matmul_views(x, g, w, dilations):
    t_len, d = x.shape
    n = w.shape[1]
    tm = min(QKV_ROW_TILE, t_len)
    return pl.pallas_call(
        functools.partial(_norm_matmul_views_kernel, dilations=dilations),
        grid=(t_len // tm,),
        in_specs=[pl.BlockSpec((tm, d), lambda i: (i, 0)),
                  pl.BlockSpec((1, d), lambda i: (0, 0)),
                  pl.BlockSpec((d, n), lambda i: (0, 0))],
        out_specs=[pl.BlockSpec((tm // dil, dil * n), lambda i: (i, 0)) for dil in dilations],
        out_shape=[jax.ShapeDtypeStruct((t_len // dil, dil * n), BF16) for dil in dilations],
        compiler_params=_params(1),
        name="norm_matmul_views",
    )(x, g.reshape(1, d), w)


def _even_in_kernel(x_ref, g_ref, w_ref, lng_ref, lnb_ref, ws_ref, bs_ref, ya_ref, pb_ref):
    tm = x_ref.shape[0]
    h = _rmsnorm(x_ref[...], g_ref[...]).astype(BF16)
    p = _dot(h, w_ref[...])
    pb_ref[...] = p[:, 2 * A_WIDTH:]

    row = lax.broadcasted_iota(jnp.int32, (A_CHUNK, A_CHUNK), 0)
    col = lax.broadcasted_iota(jnp.int32, (A_CHUNK, A_CHUNK), 1)
    causal = col <= row
    w_masked = [jnp.where(causal, ws_ref[g], 0.0).astype(BF16) for g in range(A_GROUPS)]
    for c in range(tm // A_CHUNK):
        rows = slice(c * A_CHUNK, (c + 1) * A_CHUNK)
        u = p[rows, :A_WIDTH]
        v = p[rows, A_WIDTH:2 * A_WIDTH]
        mean = jnp.mean(v, axis=-1, keepdims=True)
        cen = v - mean
        var = jnp.mean(cen * cen, axis=-1, keepdims=True)
        vn = (cen * lax.rsqrt(var + LN_EPS) * lng_ref[...] + lnb_ref[...]).astype(BF16)
        for g in range(A_GROUPS):
            cols = slice(g * A_GROUP, (g + 1) * A_GROUP)
            mixed = _dot(w_masked[g], vn[:, cols]) + bs_ref[:, g:g + 1]
            ya_ref[rows, cols] = (u[:, cols] * mixed).astype(ya_ref.dtype)


def even_in(x, g, w_in, ln_g, ln_b, w_s, b_s):
    t_len, d = x.shape
    n = w_in.shape[1]
    tm = min(ROW_TILE, t_len)
    full = lambda shape: pl.BlockSpec(shape, lambda i: (0,) * len(shape))
    return pl.pallas_call(
        _even_in_kernel,
        grid=(t_len // tm,),
        in_specs=[pl.BlockSpec((tm, d), lambda i: (i, 0)),
                  full((1, d)), full((d, n)), full((1, A_WIDTH)), full((1, A_WIDTH)),
                  full((A_GROUPS, A_CHUNK, A_CHUNK)), full((A_CHUNK, A_GROUPS))],
        out_specs=[pl.BlockSpec((tm, A_WIDTH), lambda i: (i, 0)),
                   pl.BlockSpec((tm, SHIFT_W), lambda i: (i, 0))],
        out_shape=[jax.ShapeDtypeStruct((t_len, A_WIDTH), BF16),
                   jax.ShapeDtypeStruct((t_len, SHIFT_W), F32)],
        compiler_params=_params(1),
        name="even_in",
    )(x, g.reshape(1, d), w_in.astype(BF16), ln_g.reshape(1, A_WIDTH), ln_b.reshape(1, A_WIDTH),
      w_s, b_s.T)


def _head_sums(x):
    lane = lax.broadcasted_iota(jnp.int32, (1, LANES), 1)
    head0 = lane < B_HEAD
    outs = []
    for blk in range(x.shape[1] // LANES):
        xb = x[:, blk * LANES:(blk + 1) * LANES]
        s0 = jnp.sum(jnp.where(head0, xb, 0.0), axis=-1, keepdims=True)
        s1 = jnp.sum(jnp.where(head0, 0.0, xb), axis=-1, keepdims=True)
        outs.append(jnp.where(head0, s0, s1))
    return jnp.concatenate(outs, axis=-1)


def _rwkv_prep_tile(p, prev_row, prm, buf):
    mu_ref, w0_ref, wa_ref, a0_ref, g2_ref, kkw_ref, kaw_ref = prm
    r_o, k_o, v_o, kk_o, b_o, lw_o, g_o = buf
    tm = p.shape[0]
    rowid = lax.broadcasted_iota(jnp.int32, (tm, 1), 0)
    prev = jnp.where(rowid == 0, prev_row, pltpu.roll(p, 1, 0))
    ps = p + (prev - p) * mu_ref[...]

    r = ps[:, :B_WIDTH]
    k = ps[:, B_WIDTH:2 * B_WIDTH]
    v = ps[:, 2 * B_WIDTH:3 * B_WIDTH]
    lwa = ps[:, 3 * B_WIDTH:3 * B_WIDTH + LORA_W + LORA_A]
    lg = ps[:, 3 * B_WIDTH + LORA_W + LORA_A:]

    lane = lax.broadcasted_iota(jnp.int32, (1, LORA_W + LORA_A), 1)
    lora_in = jnp.where(lane < LORA_W, jnp.tanh(lwa), lwa).astype(BF16)
    lora = _dot(lora_in, wa_ref[...])
    neg = -(w0_ref[...] + lora[:, :B_WIDTH])
    softplus = jnp.maximum(neg, 0.0) + jnp.log1p(jnp.exp(-jnp.abs(neg)))
    w = -softplus - 0.5
    a = jax.nn.sigmoid(a0_ref[...] + lora[:, B_WIDTH:])
    g = _dot(jax.nn.sigmoid(lg).astype(BF16), g2_ref[...])

    kkr = k * kkw_ref[...]
    kk = kkr * lax.rsqrt(jnp.maximum(_head_sums(kkr * kkr), 1e-24))

    r_o[...] = r
    k_o[...] = k * (1.0 + (a - 1.0) * kaw_ref[...])
    v_o[...] = v
    kk_o[...] = kk
    b_o[...] = kk * a
    lw_o[...] = -jnp.exp(w)
    g_o[...] = g


def _rwkv_scan_tile(buf, prm, o_ref, out_rows, h_ref, po_ref, q_ref, o2_ref, oacc_ref, n_chunks):
    r_ref, k_ref, v_ref, kk_ref, b_ref, lw_ref, g_ref = buf
    rk_ref, gng_ref, gnb_ref = prm
    c_len = RWKV_CHUNK
    n_blocks = B_WIDTH // LANES

    lane = lax.broadcasted_iota(jnp.int32, (1, LANES), 1)
    head0 = lane < B_HEAD
    rho = lax.broadcasted_iota(jnp.int32, (LANES, LANES), 0)
    sig = lax.broadcasted_iota(jnp.int32, (LANES, LANES), 1)
    same_head = (rho // c_len) == (sig // c_len)
    strict = same_head & ((sig % c_len) < (rho % c_len))
    incl = same_head & ((sig % c_len) <= (rho % c_len))
    eye = rho == sig
    tm = n_chunks * c_len
    tri_r = lax.broadcasted_iota(jnp.int32, (tm, tm), 0)
    tri_c = lax.broadcasted_iota(jnp.int32, (tm, tm), 1)
    cum_mat = ((tri_r // c_len == tri_c // c_len) & (tri_c <= tri_r)).astype(BF16)

    def stack(x):
        return jnp.concatenate([jnp.where(head0, x, 0.0), jnp.where(head0, 0.0, x)], axis=0)

    lw_all = lw_ref[...]
    cum_all = _exact_lhs(cum_mat, lw_all)
    chains = []
    for c in range(n_chunks):
        rows = slice(c * c_len, (c + 1) * c_len)
        cum = cum_all[rows, :]
        total = cum[c_len - 1:c_len, :]
        grow = jnp.exp(cum)
        inv = jnp.exp(-cum)
        to_end = jnp.exp(total - cum)
        decay = jnp.exp(total)
        r_t = r_ref[rows, :] * grow
        kk_t = kk_ref[rows, :] * jnp.exp(cum - lw_all[rows, :])
        b_t = b_ref[rows, :] * inv
        k_t = k_ref[rows, :] * inv
        b_e = b_ref[rows, :] * to_end
        k_e = k_ref[rows, :] * to_end
        v = v_ref[rows, :]
        for blk in range(n_blocks):
            cols = slice(blk * LANES, (blk + 1) * LANES)
            kk_s = stack(kk_t[:, cols]).astype(BF16)
            r_s = stack(r_t[:, cols])
            a_all = _dot_nt(
                jnp.concatenate([kk_s, r_s.astype(BF16)], axis=0),
                jnp.concatenate([stack(b_t[:, cols]), stack(k_t[:, cols])], axis=0).astype(BF16))
            neg_l = -jnp.where(strict, a_all[:LANES, :LANES], 0.0)
            v_lhs = jnp.concatenate(
                [stack(k_e[:, cols]).T,
                 jnp.where(incl, a_all[LANES:, LANES:], 0.0),
                 jnp.where(strict, a_all[:LANES, LANES:], 0.0)], axis=0).astype(BF16)
            w_lhs = jnp.concatenate(
                [stack(b_e[:, cols]).T, jnp.where(incl, a_all[LANES:, :LANES], 0.0)],
                axis=0).astype(BF16)
            chains.append(dict(kk_s=kk_s, r_s=r_s, v_s=stack(v[:, cols]).astype(BF16),
                               v_lhs=v_lhs, w_lhs=w_lhs, decay=decay[:, cols],
                               power=neg_l.astype(BF16),
                               t_inv=jnp.where(eye, 1.0, 0.0) + neg_l))

    for ch in chains:
        ch["power"] = _dot(ch["power"], ch["power"]).astype(BF16)
    for level in range(1, NEUMANN_LEVELS):
        for ch in chains:
            both = _dot(jnp.concatenate([ch["power"], ch["t_inv"].astype(BF16)], axis=0),
                        ch["power"])
            ch["power"] = both[:LANES].astype(BF16)
            ch["t_inv"] = ch["t_inv"] + both[LANES:]
    for ch in chains:
        ch["t_inv"] = (ch["t_inv"] + _dot(ch["t_inv"].astype(BF16), ch["power"])).astype(BF16)

    for ch in chains:
        ch["xv"] = _dot(ch["v_lhs"], ch["v_s"])
    for ch in chains:
        rhs = jnp.concatenate([ch["kk_s"], ch["xv"][2 * LANES:].astype(BF16)], axis=1)
        ch["w12"] = _dot(ch["t_inv"], rhs).astype(BF16)
    for idx, ch in enumerate(chains):
        xw = _dot(ch["w_lhs"], ch["w12"])
        p_mat = jnp.where(eye, ch["decay"], 0.0) - xw[:LANES, :LANES]
        o1 = ch["r_s"] - xw[LANES:, :LANES]
        po_ref[idx] = jnp.concatenate([p_mat, o1], axis=0).astype(BF16)
        q_ref[idx] = ch["xv"][:LANES] - xw[:LANES, LANES:]
        o2_ref[idx] = ch["xv"][LANES:2 * LANES] - xw[LANES:, LANES:]

    states = [h_ref[blk] for blk in range(n_blocks)]
    for c in range(n_chunks):
        rows = slice(c * c_len, (c + 1) * c_len)
        pos = [_dot(po_ref[c * n_blocks + blk], states[blk].astype(BF16))
               for blk in range(n_blocks)]
        for blk in range(n_blocks):
            chain = c * n_blocks + blk
            states[blk] = pos[blk][:LANES] + q_ref[chain]
            o_st = pos[blk][LANES:] + o2_ref[chain]
            oacc_ref[rows, blk * LANES:(blk + 1) * LANES] = o_st[:c_len] + o_st[c_len:]
    for blk in range(n_blocks):
        h_ref[blk] = states[blk]

    o = oacc_ref[...]
    mean = _head_sums(o) * (1.0 / B_HEAD)
    cen = o - mean
    var = _head_sums(cen * cen) * (1.0 / B_HEAD)
    normed = cen * lax.rsqrt(var + GN_EPS) * gng_ref[...] + gnb_ref[...]
    bonus = _head_sums(r_ref[...] * k_ref[...] * rk_ref[...]) * v_ref[...]
    o_ref[out_rows, :] = ((normed + bonus) * g_ref[...]).astype(o_ref.dtype)


def _rwkv_kernel(*refs, n_chunks):
    p_ref, pn_ref = refs[0:2]
    prep_prm = refs[2:9]
    scan_prm = refs[9:12]
    o_ref = refs[12]
    buf_a = refs[13:20]
    buf_b = refs[20:27]
    h_ref, po_ref, q_ref, o2_ref, oacc_ref = refs[27:32]
    tm = n_chunks * RWKV_CHUNK
    scan_scratch = (h_ref, po_ref, q_ref, o2_ref, oacc_ref, n_chunks)

    @pl.when(pl.program_id(0) == 0)
    def _():
        h_ref[...] = jnp.zeros_like(h_ref)
        _rwkv_prep_tile(p_ref[0:tm, :], jnp.zeros((1, SHIFT_W), F32), prep_prm, buf_a)

    _rwkv_prep_tile(p_ref[tm:2 * tm, :], p_ref[tm - 1:tm, :], prep_prm, buf_b)
    _rwkv_scan_tile(buf_a, scan_prm, o_ref, slice(0, tm), *scan_scratch)
    _rwkv_prep_tile(pn_ref[...], p_ref[2 * tm - 1:2 * tm, :], prep_prm, buf_a)
    _rwkv_scan_tile(buf_b, scan_prm, o_ref, slice(tm, 2 * tm), *scan_scratch)


def rwkv_mixer(p_b, mu, w0, w2, a0, a2, g2, k_k, k_a, r_k, gn_g, gn_b):
    t_len = p_b.shape[0]
    n_chunks = RWKV_CHUNKS_PER_STEP
    tm = n_chunks * RWKV_CHUNK
    n_tiles = t_len // tm
    n_blocks = B_WIDTH // LANES
    n_chains = n_chunks * n_blocks
    zeros = jnp.zeros((LORA_W, B_WIDTH), F32)
    wa = jnp.concatenate([jnp.concatenate([w2, zeros], 1),
                          jnp.concatenate([zeros, a2], 1)], 0).astype(BF16)
    row = lambda z: z.reshape(1, -1)
    full = lambda shape: pl.BlockSpec(shape, lambda i: (0,) * len(shape))
    vec = full((1, B_WIDTH))
    tile_buf = [pltpu.VMEM((tm, B_WIDTH), F32)] * 7
    return pl.pallas_call(
        functools.partial(_rwkv_kernel, n_chunks=n_chunks),
        grid=(n_tiles // 2,),
        in_specs=[pl.BlockSpec((2 * tm, SHIFT_W), lambda i: (i, 0)),
                  pl.BlockSpec((tm, SHIFT_W), lambda i: (jnp.minimum(2 * i + 2, n_tiles - 1), 0)),
                  full((1, SHIFT_W)), vec, full((LORA_W + LORA_A, 2 * B_WIDTH)), vec,
                  full((LORA_G, B_WIDTH)), vec, vec,
                  vec, vec, vec],
        out_specs=pl.BlockSpec((2 * tm, B_WIDTH), lambda i: (i, 0)),
        out_shape=jax.ShapeDtypeStruct((t_len, B_WIDTH), BF16),
        scratch_shapes=tile_buf + tile_buf
                       + [pltpu.VMEM((n_blocks, LANES, LANES), F32),
                          pltpu.VMEM((n_chains, 2 * LANES, LANES), BF16),
                          pltpu.VMEM((n_chains, LANES, LANES), F32),
                          pltpu.VMEM((n_chains, LANES, LANES), F32),
                          pltpu.VMEM((tm, B_WIDTH), F32)],
        compiler_params=_params(1),
        name="rwkv_mixer",
    )(p_b, p_b, row(mu), row(w0), wa, row(a0), g2.astype(BF16), row(k_k), row(k_a),
      row(r_k), row(gn_g), row(gn_b))


def _proj_residual_kernel(x_ref, ya_ref, yb_ref, wa_ref, wb_ref, o_ref):
    o_ref[...] = x_ref[...] + _dot(ya_ref[...], wa_ref[...]) + _dot(yb_ref[...], wb_ref[...])


def proj_residual(x, y_a, y_b, w_out):
    t_len, d = x.shape
    tm = min(ROW_TILE, t_len)
    w = w_out.astype(BF16)
    half = pl.BlockSpec((tm, A_WIDTH), lambda i: (i, 0))
    wspec = pl.BlockSpec((A_WIDTH, d), lambda i: (0, 0))
    return pl.pallas_call(
        _proj_residual_kernel,
        grid=(t_len // tm,),
        in_specs=[pl.BlockSpec((tm, d), lambda i: (i, 0)), half, half, wspec, wspec],
        out_specs=pl.BlockSpec((tm, d), lambda i: (i, 0)),
        out_shape=jax.ShapeDtypeStruct((t_len, d), F32),
        compiler_params=_params(1),
        name="proj_residual",
    )(x, y_a, y_b, w[:A_WIDTH], w[A_WIDTH:])


def _attn_kernel(q_ref, kp_ref, kc_ref, vp_ref, vc_ref, o_ref, lse_ref, o_tmp, lse_tmp, *, dil):
    blk = ATTN_BLOCK
    i = pl.program_id(0)
    res = pl.program_id(1)
    row = lax.broadcasted_iota(jnp.int32, (2 * blk, 2 * blk), 0) % blk
    col = lax.broadcasted_iota(jnp.int32, (2 * blk, 2 * blk), 1)
    dist = row + blk - col
    valid = (dist >= 0) & (dist <= blk) & ((col >= blk) | (i > 0))
    lane = lax.broadcasted_iota(jnp.int32, (1, LANES), 1)
    head0 = lane < C_HEAD
    lse_acc = jnp.zeros((blk, LANES), F32)
    for pair in range(C_HEADS // HEADS_PER_BLOCK):
        cols = slice(pair * LANES, (pair + 1) * LANES)
        q2 = q_ref[:, cols] * (C_HEAD ** -0.5)
        zero = jnp.zeros_like(q2)
        qs = jnp.concatenate([jnp.where(head0, q2, zero), jnp.where(head0, zero, q2)], axis=0)
        k2 = jnp.concatenate([kp_ref[:, cols], kc_ref[:, cols]], axis=0)
        v2 = jnp.concatenate([vp_ref[:, cols], vc_ref[:, cols]], axis=0)
        s = jnp.where(valid, _dot_nt(qs, k2), NEG_INF)
        mx = jnp.max(s, axis=-1, keepdims=True)
        e = jnp.exp(s - mx)
        den = jnp.sum(e, axis=-1, keepdims=True)
        pv = _dot(e.astype(BF16), v2) / den
        lse = mx + jnp.log(den)
        o_tmp[:, cols] = jnp.where(head0, pv[:blk], pv[blk:])
        lse_acc = (lse_acc + jnp.where(lane == pair * HEADS_PER_BLOCK, lse[:blk], 0.0)
                   + jnp.where(lane == pair * HEADS_PER_BLOCK + 1, lse[blk:], 0.0))
    lse_tmp[...] = lse_acc
    for static_res in range(dil):
        @pl.when(res == static_res)
        def _():
            for c in range(o_ref.shape[0]):
                o_ref[c, pl.ds(static_res, blk, stride=dil), :] = o_tmp[:, c * LANES:(c + 1) * LANES]
            lse_ref[pl.ds(static_res, blk, stride=dil), :] = lse_tmp[...]


def dilated_attention_pattern(view, dil):
    sub = view.shape[0]
    t_len = sub * dil
    d = D_MODEL
    blk = ATTN_BLOCK
    cur = lambda part: pl.BlockSpec((blk, d), lambda i, r: (i, 3 * r + part))
    prv = lambda part: pl.BlockSpec((blk, d), lambda i, r: (jnp.maximum(i - 1, 0), 3 * r + part))
    return pl.pallas_call(
        functools.partial(_attn_kernel, dil=dil),
        grid=(sub // blk, dil),
        in_specs=[cur(0), prv(1), cur(1), prv(2), cur(2)],
        out_specs=[pl.BlockSpec((d // LANES, blk * dil, LANES), lambda i, r: (0, i, 0)),
                   pl.BlockSpec((blk * dil, LANES), lambda i, r: (i, 0))],
        out_shape=[jax.ShapeDtypeStruct((d // LANES, t_len, LANES), F32),
                   jax.ShapeDtypeStruct((t_len, LANES), F32)],
        scratch_shapes=[pltpu.VMEM((blk, d), F32), pltpu.VMEM((blk, LANES), F32)],
        compiler_params=_params(2),
        name=f"dilated_attn_d{dil}",
    )(view, view, view, view, view)


def _attn_combine_proj_kernel(x_ref, o1_ref, o2_ref, o3_ref, l1_ref, l2_ref, l3_ref, ex_ref, w_ref,
                              o_ref):
    lses = [l1_ref[...], l2_ref[...], l3_ref[...]]
    mx = jnp.maximum(jnp.maximum(lses[0], lses[1]), lses[2])
    es = [jnp.exp(l - mx) for l in lses]
    den = es[0] + es[1] + es[2]
    y = jnp.zeros(o_ref.shape, F32)
    for e, o in zip(es, (o1_ref, o2_ref, o3_ref)):
        o_full = jnp.concatenate([o[c] for c in range(o.shape[0])], axis=-1)
        y = y + _exact_rhs(e / den, ex_ref[...]) * o_full
    o_ref[...] = x_ref[...] + _dot(y.astype(BF16), w_ref[...])


def attn_combine_proj(x, outs, lses, w_out):
    t_len, d = x.shape
    tm = min(ROW_TILE, t_len)
    head_of_lane = jnp.arange(d) // C_HEAD
    expand = (jnp.arange(LANES)[:, None] == head_of_lane[None, :]).astype(BF16)
    wide = pl.BlockSpec((tm, d), lambda i: (i, 0))
    chunked = pl.BlockSpec((d // LANES, tm, LANES), lambda i: (0, i, 0))
    narrow = pl.BlockSpec((tm, LANES), lambda i: (i, 0))
    return pl.pallas_call(
        _attn_combine_proj_kernel,
        grid=(t_len // tm,),
        in_specs=[wide, chunked, chunked, chunked, narrow, narrow, narrow,
                  pl.BlockSpec((LANES, d), lambda i: (0, 0)),
                  pl.BlockSpec((d, d), lambda i: (0, 0))],
        out_specs=wide,
        out_shape=jax.ShapeDtypeStruct((t_len, d), F32),
        compiler_params=_params(1),
        name="attn_combine_proj",
    )(x, *outs, *lses, expand, w_out.astype(BF16))


def _ffn_kernel(x_ref, g_ref, wu_ref, cw_ref, cb_ref, wd_ref, fn_ref,
                o_ref, h_ref, zg0_ref, zg1_ref, zv0_ref, zv1_ref, carry_g_ref, carry_v_ref, *,
                final_norm):
    tm = x_ref.shape[0]
    n_j, tf = wd_ref.shape[0], wd_ref.shape[1]
    halo = SUBLANES
    zg_refs = (zg0_ref, zg1_ref)
    zv_refs = (zv0_ref, zv1_ref)

    def tile_cols(j, base):
        if isinstance(j, int):
            return pl.ds(base + j * tf, tf)
        return pl.ds(pl.multiple_of(base + j * tf, tf), tf)

    @pl.when(pl.program_id(0) == 0)
    def _():
        carry_g_ref[...] = jnp.zeros_like(carry_g_ref)
        carry_v_ref[...] = jnp.zeros_like(carry_v_ref)

    x = x_ref[...]
    h_ref[...] = _rmsnorm(x, g_ref[...]).astype(BF16)
    o_ref[...] = x

    def up(j, slot):
        h = h_ref[...]
        for base, z_ref, carry_ref in ((0, zg_refs[slot], carry_g_ref),
                                       (D_FF, zv_refs[slot], carry_v_ref)):
            z = _dot(h, wu_ref[:, tile_cols(j, base)])
            z_ref[0:halo, :] = carry_ref[j]
            z_ref[halo:halo + tm, :] = z
            carry_ref[j] = z[tm - halo:tm, :]

    def finish(j, slot):
        def conv(z_ref, base):
            z = z_ref[halo:halo + tm, :]
            z1 = z_ref[halo - 1:halo - 1 + tm, :]
            z2 = z_ref[halo - 2:halo - 2 + tm, :]
            cw = cw_ref[:, tile_cols(j, base)]
            return (cw[0:1, :] * z2 + cw[1:2, :] * z1 + cw[2:3, :] * z
                    + cb_ref[:, tile_cols(j, base)])

        gate = conv(zg_refs[slot], 0)
        val = conv(zv_refs[slot], D_FF)
        act = (gate * jax.nn.sigmoid(gate) * val).astype(BF16)
        o_ref[...] += _dot(act, wd_ref[j])

    up(0, 0)

    def body(pair, carry):
        j = 2 * pair
        up(j + 1, 1)
        finish(j, 0)
        up(j + 2, 0)
        finish(j + 1, 1)
        return carry

    assert n_j % 2 == 1
    lax.fori_loop(0, (n_j - 1) // 2, body, 0)
    finish(n_j - 1, 0)

    if final_norm:
        o_ref[...] = _rmsnorm(o_ref[...], fn_ref[...])


def conv_glu_ffn(x, norm_g, w_up, conv_w, conv_b, w_down, final_g, final_norm):
    t_len, d = x.shape
    tm = min(FFN_ROW_TILE, t_len)
    tf = FFN_COL_TILE
    n_j = D_FF // tf
    full = lambda shape: pl.BlockSpec(shape, lambda i: (0,) * len(shape),
                                      pipeline_mode=pl.Buffered(1))
    return pl.pallas_call(
        functools.partial(_ffn_kernel, final_norm=final_norm),
        grid=(t_len // tm,),
        in_specs=[pl.BlockSpec((tm, d), lambda i: (i, 0)),
                  full((1, d)),
                  full((d, 2 * D_FF)),
                  full((3, 2 * D_FF)),
                  full((1, 2 * D_FF)),
                  full((n_j, tf, d)),
                  full((1, d))],
        out_specs=pl.BlockSpec((tm, d), lambda i: (i, 0)),
        out_shape=jax.ShapeDtypeStruct((t_len, d), F32),
        scratch_shapes=[pltpu.VMEM((tm, d), BF16)]
                       + [pltpu.VMEM((tm + SUBLANES, tf), F32)] * 4
                       + [pltpu.VMEM((n_j, SUBLANES, tf), F32),
                        pltpu.VMEM((n_j, SUBLANES, tf), F32)],
        compiler_params=_params(1),
        name="conv_glu_ffn",
    )(x, norm_g.reshape(1, d), w_up.astype(BF16), conv_w, conv_b.reshape(1, 2 * D_FF),
      w_down.astype(BF16).reshape(n_j, tf, d), final_g.reshape(1, d))


def even_mixer(x, norm, w_in, ln_g, ln_b, w_s, b_s, mu, w0, w2, a0, a2, g2, k_k, k_a, r_k, gn_g,
               gn_b, w_out):
    y_a, p_b = even_in(x, norm, w_in, ln_g, ln_b, w_s, b_s)
    y_b = rwkv_mixer(p_b, mu, w0, w2, a0, a2, g2, k_k, k_a, r_k, gn_g, gn_b)
    return proj_residual(x, y_a, y_b, w_out)


def odd_mixer(x, norm, w_qkv, w_out):
    dilations = tuple(dil for _, dil in C_PATTERNS)
    views = norm_matmul_views(x, norm, w_qkv.astype(BF16), dilations)
    outs, lses = [], []
    for view, dil in zip(views, dilations):
        o, l = dilated_attention_pattern(view, dil)
        outs.append(o)
        lses.append(l)
    return attn_combine_proj(x, outs, lses, w_out)


def kernel(x, ev_norm, ev_w_in, ev_ln_g, ev_ln_b, ev_w_s, ev_b_s, ev_mu, ev_w0, ev_w2, ev_a0, ev_a2, ev_g2, ev_k_k, ev_k_a, ev_r_k, ev_gn_g, ev_gn_b, ev_w_out, od_norm, od_w_qkv, od_w_out, ff_norm, ff_w_up, ff_conv_w, ff_conv_b, ff_w_down, final_norm):
    bsz, t_len, d = x.shape
    depth = ff_norm.shape[0]
    outs = []
    for bi in range(bsz):
        xb = x[bi]
        for layer in range(depth):
            j = layer // 2
            if layer % 2 == 0:
                xb = even_mixer(xb, ev_norm[j], ev_w_in[j], ev_ln_g[j], ev_ln_b[j], ev_w_s[j],
                                ev_b_s[j], ev_mu[j], ev_w0[j], ev_w2[j], ev_a0[j], ev_a2[j],
                                ev_g2[j], ev_k_k[j], ev_k_a[j], ev_r_k[j], ev_gn_g[j], ev_gn_b[j],
                                ev_w_out[j])
            else:
                xb = odd_mixer(xb, od_norm[j], od_w_qkv[j], od_w_out[j])
            xb = conv_glu_ffn(xb, ff_norm[layer], ff_w_up[layer], ff_conv_w[layer],
                              ff_conv_b[layer], ff_w_down[layer], final_norm,
                              final_norm=(layer == depth - 1))
        outs.append(xb)
    return jnp.stack(outs, axis=0)
```

```python
import functools

import jax
import jax.numpy as jnp
from jax import lax
from jax.experimental import pallas as pl
from jax.experimental.pallas import tpu as pltpu

F32 = jnp.float32
BF16 = jnp.bfloat16

D_MODEL = 1024
A_WIDTH = 512
A_GROUP = 128
A_GROUPS = 4
A_CHUNK = 128
LN_EPS = 1e-5
B_WIDTH = 512
B_HEAD = 64
LORA_W = 64
LORA_A = 64
LORA_G = 128
GN_EPS = 64e-5
SHIFT_W = 3 * B_WIDTH + LORA_W + LORA_A + LORA_G
C_HEAD = 64
C_HEADS = 16
C_PATTERNS = ((128, 1), (512, 4), (2048, 16))
NEG_INF = -1e30
D_FF = 2816
RMS_EPS = 1e-6

LANES = 128
SUBLANES = 8
HEADS_PER_BLOCK = LANES // B_HEAD
VMEM_LIMIT_BYTES = 52 * 1024 * 1024

ROW_TILE = 512
QKV_ROW_TILE = 256
FFN_ROW_TILE = 512
FFN_COL_TILE = 256
RWKV_CHUNK = 64
RWKV_CHUNKS_PER_STEP = 4
ATTN_BLOCK = 128
NEUMANN_LEVELS = 5


def _params(n_axes):
    return pltpu.CompilerParams(dimension_semantics=("arbitrary",) * n_axes,
                                vmem_limit_bytes=VMEM_LIMIT_BYTES)


def _rmsnorm(x, g):
    return x * lax.rsqrt(jnp.mean(x * x, axis=-1, keepdims=True) + RMS_EPS) * g


def _dot(a, b):
    return jnp.dot(a, b, preferred_element_type=F32)


def _dot_nt(a, b):
    return lax.dot_general(a, b, (((1,), (1,)), ((), ())), preferred_element_type=F32)


def _split3(x):
    hi = x.astype(BF16)
    r1 = x - hi.astype(F32)
    mid = r1.astype(BF16)
    lo = (r1 - mid.astype(F32)).astype(BF16)
    return hi, mid, lo


def _exact_rhs(x, m_bf16):
    hi, mid, lo = _split3(x)
    return _dot(hi, m_bf16) + (_dot(mid, m_bf16) + _dot(lo, m_bf16))


def _exact_lhs(m_bf16, x):
    hi, mid, lo = _split3(x)
    return _dot(m_bf16, hi) + (_dot(m_bf16, mid) + _dot(m_bf16, lo))


def _norm_matmul_views_kernel(x_ref, g_ref, w_ref, *o_refs, dilations):
    tm = x_ref.shape[0]
    n = w_ref.shape[1]
    h = _rmsnorm(x_ref[...], g_ref[...]).astype(BF16)
    res = _dot(h, w_ref[...]).astype(BF16)
    out_row = lax.broadcasted_iota(jnp.int32, (tm, tm), 0)
    in_row = lax.broadcasted_iota(jnp.int32, (tm, tm), 1)
    for dil, o_ref in zip(dilations, o_refs):
        if dil == 1:
            o_ref[...] = res
            continue
        per = tm // dil
        assert per & (per - 1) == 0
        shift = per.bit_length() - 1
        src_row = (out_row & (per - 1)) * dil + (out_row >> shift)
        grouped = _dot((in_row == src_row).astype(BF16), res).astype(BF16)
        for r in range(dil):
            o_ref[:, r * n:(r + 1) * n] = grouped[r * per:(r + 1) * per, :]


def norm_matmul_views(x, g, w, dilations):
    t_len, d = x.shape
    n = w.shape[1]
    tm = min(QKV_ROW_TILE, t_len)
    return pl.pallas_call(
        functools.partial(_norm_matmul_views_kernel, dilations=dilations),
        grid=(t_len // tm,),
        in_specs=[pl.BlockSpec((tm, d), lambda i: (i, 0)),
                  pl.BlockSpec((1, d), lambda i: (0, 0)),
                  pl.BlockSpec((d, n), lambda i: (0, 0))],
        out_specs=[pl.BlockSpec((tm // dil, dil * n), lambda i: (i, 0)) for dil in dilations],
        out_shape=[jax.ShapeDtypeStruct((t_len // dil, dil * n), BF16) for dil in dilations],
        compiler_params=_params(1),
        name="norm_matmul_views",
    )(x, g.reshape(1, d), w)


def _even_in_kernel(x_ref, g_ref, w_ref, lng_ref, lnb_ref, ws_ref, bs_ref, ya_ref, pb_ref):
    tm = x_ref.shape[0]
    h = _rmsnorm(x_ref[...], g_ref[...]).astype(BF16)
    p = _dot(h, w_ref[...])
    pb_ref[...] = p[:, 2 * A_WIDTH:]

    row = lax.broadcasted_iota(jnp.int32, (A_CHUNK, A_CHUNK), 0)
    col = lax.broadcasted_iota(jnp.int32, (A_CHUNK, A_CHUNK), 1)
    causal = col <= row
    w_masked = [jnp.where(causal, ws_ref[g], 0.0).astype(BF16) for g in range(A_GROUPS)]
    for c in range(tm // A_CHUNK):
        rows = slice(c * A_CHUNK, (c + 1) * A_CHUNK)
        u = p[rows, :A_WIDTH]
        v = p[rows, A_WIDTH:2 * A_WIDTH]
        mean = jnp.mean(v, axis=-1, keepdims=True)
        cen = v - mean
        var = jnp.mean(cen * cen, axis=-1, keepdims=True)
        vn = (cen * lax.rsqrt(var + LN_EPS) * lng_ref[...] + lnb_ref[...]).astype(BF16)
        for g in range(A_GROUPS):
            cols = slice(g * A_GROUP, (g + 1) * A_GROUP)
            mixed = _dot(w_masked[g], vn[:, cols]) + bs_ref[:, g:g + 1]
            ya_ref[rows, cols] = (u[:, cols] * mixed).astype(ya_ref.dtype)


def even_in(x, g, w_in, ln_g, ln_b, w_s, b_s):
    t_len, d = x.shape
    n = w_in.shape[1]
    tm = min(ROW_TILE, t_len)
    full = lambda shape: pl.BlockSpec(shape, lambda i: (0,) * len(shape))
    return pl.pallas_call(
        _even_in_kernel,
        grid=(t_len // tm,),
        in_specs=[pl.BlockSpec((tm, d), lambda i: (i, 0)),
                  full((1, d)), full((d, n)), full((1, A_WIDTH)), full((1, A_WIDTH)),
                  full((A_GROUPS, A_CHUNK, A_CHUNK)), full((A_CHUNK, A_GROUPS))],
        out_specs=[pl.BlockSpec((tm, A_WIDTH), lambda i: (i, 0)),
                   pl.BlockSpec((tm, SHIFT_W), lambda i: (i, 0))],
        out_shape=[jax.ShapeDtypeStruct((t_len, A_WIDTH), BF16),
                   jax.ShapeDtypeStruct((t_len, SHIFT_W), F32)],
        compiler_params=_params(1),
        name="even_in",
    )(x, g.reshape(1, d), w_in.astype(BF16), ln_g.reshape(1, A_WIDTH), ln_b.reshape(1, A_WIDTH),
      w_s, b_s.T)


def _head_sums(x):
    lane = lax.broadcasted_iota(jnp.int32, (1, LANES), 1)
    head0 = lane < B_HEAD
    outs = []
    for blk in range(x.shape[1] // LANES):
        xb = x[:, blk * LANES:(blk + 1) * LANES]
        s0 = jnp.sum(jnp.where(head0, xb, 0.0), axis=-1, keepdims=True)
        s1 = jnp.sum(jnp.where(head0, 0.0, xb), axis=-1, keepdims=True)
        outs.append(jnp.where(head0, s0, s1))
    return jnp.concatenate(outs, axis=-1)


def _rwkv_prep_tile(p, prev_row, prm, buf):
    mu_ref, w0_ref, wa_ref, a0_ref, g2_ref, kkw_ref, kaw_ref = prm
    r_o, k_o, v_o, kk_o, b_o, lw_o, g_o = buf
    tm = p.shape[0]
    rowid = lax.broadcasted_iota(jnp.int32, (tm, 1), 0)
    prev = jnp.where(rowid == 0, prev_row, pltpu.roll(p, 1, 0))
    ps = p + (prev - p) * mu_ref[...]

    r = ps[:, :B_WIDTH]
    k = ps[:, B_WIDTH:2 * B_WIDTH]
    v = ps[:, 2 * B_WIDTH:3 * B_WIDTH]
    lwa = ps[:, 3 * B_WIDTH:3 * B_WIDTH + LORA_W + LORA_A]
    lg = ps[:, 3 * B_WIDTH + LORA_W + LORA_A:]

    lane = lax.broadcasted_iota(jnp.int32, (1, LORA_W + LORA_A), 1)
    lora_in = jnp.where(lane < LORA_W, jnp.tanh(lwa), lwa).astype(BF16)
    lora = _dot(lora_in, wa_ref[...])
    neg = -(w0_ref[...] + lora[:, :B_WIDTH])
    softplus = jnp.maximum(neg, 0.0) + jnp.log1p(jnp.exp(-jnp.abs(neg)))
    w = -softplus - 0.5
    a = jax.nn.sigmoid(a0_ref[...] + lora[:, B_WIDTH:])
    g = _dot(jax.nn.sigmoid(lg).astype(BF16), g2_ref[...])

    kkr = k * kkw_ref[...]
    kk = kkr * lax.rsqrt(jnp.maximum(_head_sums(kkr * kkr), 1e-24))

    r_o[...] = r
    k_o[...] = k * (1.0 + (a - 1.0) * kaw_ref[...])
    v_o[...] = v
    kk_o[...] = kk
    b_o[...] = kk * a
    lw_o[...] = -jnp.exp(w)
    g_o[...] = g


def _rwkv_scan_tile(buf, prm, o_ref, out_rows, h_ref, po_ref, q_ref, o2_ref, oacc_ref, n_chunks):
    r_ref, k_ref, v_ref, kk_ref, b_ref, lw_ref, g_ref = buf
    rk_ref, gng_ref, gnb_ref = prm
    c_len = RWKV_CHUNK
    n_blocks = B_WIDTH // LANES

    lane = lax.broadcasted_iota(jnp.int32, (1, LANES), 1)
    head0 = lane < B_HEAD
    rho = lax.broadcasted_iota(jnp.int32, (LANES, LANES), 0)
    sig = lax.broadcasted_iota(jnp.int32, (LANES, LANES), 1)
    same_head = (rho // c_len) == (sig // c_len)
    strict = same_head & ((sig % c_len) < (rho % c_len))
    incl = same_head & ((sig % c_len) <= (rho % c_len))
    eye = rho == sig
    tm = n_chunks * c_len
    tri_r = lax.broadcasted_iota(jnp.int32, (tm, tm), 0)
    tri_c = lax.broadcasted_iota(jnp.int32, (tm, tm), 1)
    cum_mat = ((tri_r // c_len == tri_c // c_len) & (tri_c <= tri_r)).astype(BF16)

    def stack(x):
        return jnp.concatenate([jnp.where(head0, x, 0.0), jnp.where(head0, 0.0, x)], axis=0)

    lw_all = lw_ref[...]
    cum_all = _exact_lhs(cum_mat, lw_all)
    chains = []
    for c in range(n_chunks):
        rows = slice(c * c_len, (c + 1) * c_len)
        cum = cum_all[rows, :]
        total = cum[c_len - 1:c_len, :]
        grow = jnp.exp(cum)
        inv = jnp.exp(-cum)
        to_end = jnp.exp(total - cum)
        decay = jnp.exp(total)
        r_t = r_ref[rows, :] * grow
        kk_t = kk_ref[rows, :] * jnp.exp(cum - lw_all[rows, :])
        b_t = b_ref[rows, :] * inv
        k_t = k_ref[rows, :] * inv
        b_e = b_ref[rows, :] * to_end
        k_e = k_ref[rows, :] * to_end
        v = v_ref[rows, :]
        for blk in range(n_blocks):
            cols = slice(blk * LANES, (blk + 1) * LANES)
            kk_s = stack(kk_t[:, cols]).astype(BF16)
            r_s = stack(r_t[:, cols])
            a_all = _dot_nt(
                jnp.concatenate([kk_s, r_s.astype(BF16)], axis=0),
                jnp.concatenate([stack(b_t[:, cols]), stack(k_t[:, cols])], axis=0).astype(BF16))
            neg_l = -jnp.where(strict, a_all[:LANES, :LANES], 0.0)
            v_lhs = jnp.concatenate(
                [stack(k_e[:, cols]).T,
                 jnp.where(incl, a_all[LANES:, LANES:], 0.0),
                 jnp.where(strict, a_all[:LANES, LANES:], 0.0)], axis=0).astype(BF16)
            w_lhs = jnp.concatenate(
                [stack(b_e[:, cols]).T, jnp.where(incl, a_all[LANES:, :LANES], 0.0)],
                axis=0).astype(BF16)
            chains.append(dict(kk_s=kk_s, r_s=r_s, v_s=stack(v[:, cols]).astype(BF16),
                               v_lhs=v_lhs, w_lhs=w_lhs, decay=decay[:, cols],
                               power=neg_l.astype(BF16),
                               t_inv=jnp.where(eye, 1.0, 0.0) + neg_l))

    for ch in chains:
        ch["power"] = _dot(ch["power"], ch["power"]).astype(BF16)
    for level in range(1, NEUMANN_LEVELS):
        for ch in chains:
            both = _dot(jnp.concatenate([ch["power"], ch["t_inv"].astype(BF16)], axis=0),
                        ch["power"])
            ch["power"] = both[:LANES].astype(BF16)
            ch["t_inv"] = ch["t_inv"] + both[LANES:]
    for ch in chains:
        ch["t_inv"] = (ch["t_inv"] + _dot(ch["t_inv"].astype(BF16), ch["power"])).astype(BF16)

    for ch in chains:
        ch["xv"] = _dot(ch["v_lhs"], ch["v_s"])
    for ch in chains:
        rhs = jnp.concatenate([ch["kk_s"], ch["xv"][2 * LANES:].astype(BF16)], axis=1)
        ch["w12"] = _dot(ch["t_inv"], rhs).astype(BF16)
    for idx, ch in enumerate(chains):
        xw = _dot(ch["w_lhs"], ch["w12"])
        p_mat = jnp.where(eye, ch["decay"], 0.0) - xw[:LANES, :LANES]
        o1 = ch["r_s"] - xw[LANES:, :LANES]
        po_ref[idx] = jnp.concatenate([p_mat, o1], axis=0).astype(BF16)
        q_ref[idx] = ch["xv"][:LANES] - xw[:LANES, LANES:]
        o2_ref[idx] = ch["xv"][LANES:2 * LANES] - xw[LANES:, LANES:]

    states = [h_ref[blk] for blk in range(n_blocks)]
    for c in range(n_chunks):
        rows = slice(c * c_len, (c + 1) * c_len)
        pos = [_dot(po_ref[c * n_blocks + blk], states[blk].astype(BF16))
               for blk in range(n_blocks)]
        for blk in range(n_blocks):
            chain = c * n_blocks + blk
            states[blk] = pos[blk][:LANES] + q_ref[chain]
            o_st = pos[blk][LANES:] + o2_ref[chain]
            oacc_ref[rows, blk * LANES:(blk + 1) * LANES] = o_st[:c_len] + o_st[c_len:]
    for blk in range(n_blocks):
        h_ref[blk] = states[blk]

    o = oacc_ref[...]
    mean = _head_sums(o) * (1.0 / B_HEAD)
    cen = o - mean
    var = _head_sums(cen * cen) * (1.0 / B_HEAD)
    normed = cen * lax.rsqrt(var + GN_EPS) * gng_ref[...] + gnb_ref[...]
    bonus = _head_sums(r_ref[...] * k_ref[...] * rk_ref[...]) * v_ref[...]
    o_ref[out_rows, :] = ((normed + bonus) * g_ref[...]).astype(o_ref.dtype)


def _rwkv_kernel(*refs, n_chunks):
    p_ref, pn_ref = refs[0:2]
    prep_prm = refs[2:9]
    scan_prm = refs[9:12]
    o_ref = refs[12]
    buf_a = refs[13:20]
    buf_b = refs[20:27]
    h_ref, po_ref, q_ref, o2_ref, oacc_ref = refs[27:32]
    tm = n_chunks * RWKV_CHUNK
    scan_scratch = (h_ref, po_ref, q_ref, o2_ref, oacc_ref, n_chunks)

    @pl.when(pl.program_id(0) == 0)
    def _():
        h_ref[...] = jnp.zeros_like(h_ref)
        _rwkv_prep_tile(p_ref[0:tm, :], jnp.zeros((1, SHIFT_W), F32), prep_prm, buf_a)

    _rwkv_prep_tile(p_ref[tm:2 * tm, :], p_ref[tm - 1:tm, :], prep_prm, buf_b)
    _rwkv_scan_tile(buf_a, scan_prm, o_ref, slice(0, tm), *scan_scratch)
    _rwkv_prep_tile(pn_ref[...], p_ref[2 * tm - 1:2 * tm, :], prep_prm, buf_a)
    _rwkv_scan_tile(buf_b, scan_prm, o_ref, slice(tm, 2 * tm), *scan_scratch)


def rwkv_mixer(p_b, mu, w0, w2, a0, a2, g2, k_k, k_a, r_k, gn_g, gn_b):
    t_len = p_b.shape[0]
    n_chunks = RWKV_CHUNKS_PER_STEP
    tm = n_chunks * RWKV_CHUNK
    n_tiles = t_len // tm
    n_blocks = B_WIDTH // LANES
    n_chains = n_chunks * n_blocks
    zeros = jnp.zeros((LORA_W, B_WIDTH), F32)
    wa = jnp.concatenate([jnp.concatenate([w2, zeros], 1),
                          jnp.concatenate([zeros, a2], 1)], 0).astype(BF16)
    row = lambda z: z.reshape(1, -1)
    full = lambda shape: pl.BlockSpec(shape, lambda i: (0,) * len(shape))
    vec = full((1, B_WIDTH))
    tile_buf = [pltpu.VMEM((tm, B_WIDTH), F32)] * 7
    return pl.pallas_call(
        functools.partial(_rwkv_kernel, n_chunks=n_chunks),
        grid=(n_tiles // 2,),
        in_specs=[pl.BlockSpec((2 * tm, SHIFT_W), lambda i: (i, 0)),
                  pl.BlockSpec((tm, SHIFT_W), lambda i: (jnp.minimum(2 * i + 2, n_tiles - 1), 0)),
                  full((1, SHIFT_W)), vec, full((LORA_W + LORA_A, 2 * B_WIDTH)), vec,
                  full((LORA_G, B_WIDTH)), vec, vec,
                  vec, vec, vec],
        out_specs=pl.BlockSpec((2 * tm, B_WIDTH), lambda i: (i, 0)),
        out_shape=jax.ShapeDtypeStruct((t_len, B_WIDTH), BF16),
        scratch_shapes=tile_buf + tile_buf
                       + [pltpu.VMEM((n_blocks, LANES, LANES), F32),
                          pltpu.VMEM((n_chains, 2 * LANES, LANES), BF16),
                          pltpu.VMEM((n_chains, LANES, LANES), F32),
                          pltpu.VMEM((n_chains, LANES, LANES), F32),
                          pltpu.VMEM((tm, B_WIDTH), F32)],
        compiler_params=_params(1),
        name="rwkv_mixer",
    )(p_b, p_b, row(mu), row(w0), wa, row(a0), g2.astype(BF16), row(k_k), row(k_a),
      row(r_k), row(gn_g), row(gn_b))


def _proj_residual_kernel(x_ref, ya_ref, yb_ref, w_ref, o_ref):
    o_ref[...] = (x_ref[...] + _dot(ya_ref[...], w_ref[:A_WIDTH, :])
                  + _dot(yb_ref[...], w_ref[A_WIDTH:, :]))


def proj_residual(x, y_a, y_b, w_out):
    t_len, d = x.shape
    tm = min(ROW_TILE, t_len)
    half = pl.BlockSpec((tm, A_WIDTH), lambda i: (i, 0))
    return pl.pallas_call(
        _proj_residual_kernel,
        grid=(t_len // tm,),
        in_specs=[pl.BlockSpec((tm, d), lambda i: (i, 0)), half, half,
                  pl.BlockSpec((d, d), lambda i: (0, 0))],
        out_specs=pl.BlockSpec((tm, d), lambda i: (i, 0)),
        out_shape=jax.ShapeDtypeStruct((t_len, d), F32),
        compiler_params=_params(1),
        name="proj_residual",
    )(x, y_a, y_b, w_out.astype(BF16))


def _attn_kernel(q_ref, kp_ref, kc_ref, vp_ref, vc_ref, o_ref, lse_ref, o_tmp, lse_tmp, *, dil):
    blk = ATTN_BLOCK
    half = blk // 2
    span = blk + half
    assert half == C_HEAD
    i = pl.program_id(0)
    res = pl.program_id(1)
    lane = lax.broadcasted_iota(jnp.int32, (1, LANES), 1)
    head0 = lane < C_HEAD
    key = lax.broadcasted_iota(jnp.int32, (span, LANES), 0)
    qry = lax.broadcasted_iota(jnp.int32, (span, LANES), 1) % half
    dist = qry + blk - key
    in_window = (dist >= 0) & (dist <= blk)
    valid = [in_window & ((key >= blk - sub * half) | (i > 0)) for sub in range(2)]
    swap_heads = lambda z: pltpu.roll(z, C_HEAD, 1)
    n_pairs = C_HEADS // HEADS_PER_BLOCK
    units = [(pair, sub) for pair in range(n_pairs) for sub in range(2)]
    scores = []
    for pair, sub in units:
        cols = slice(pair * LANES, (pair + 1) * LANES)
        qsub = q_ref[sub * half:(sub + 1) * half, cols] * (C_HEAD ** -0.5)
        zero = jnp.zeros_like(qsub)
        qs = jnp.concatenate([jnp.where(head0, qsub, zero), jnp.where(head0, zero, qsub)], axis=0)
        ks = jnp.concatenate([kp_ref[sub * half:, cols], kc_ref[:(sub + 1) * half, cols]], axis=0)
        scores.append(jnp.where(valid[sub], _dot_nt(ks, qs), NEG_INF))
    probs, lses = [], []
    for st in scores:
        mx = jnp.max(st, axis=0, keepdims=True)
        e = jnp.exp(st - mx)
        den = jnp.sum(e, axis=0, keepdims=True)
        probs.append((e * (1.0 / den)).astype(BF16))
        lses.append(mx + jnp.log(den))
    for (pair, sub), prob in zip(units, probs):
        cols = slice(pair * LANES, (pair + 1) * LANES)
        vs = jnp.concatenate([vp_ref[sub * half:, cols], vc_ref[:(sub + 1) * half, cols]], axis=0)
        pv = lax.dot_general(prob, vs, (((0,), (0,)), ((), ())),
                             preferred_element_type=F32)
        o_tmp[sub * half:(sub + 1) * half, cols] = jnp.where(head0, pv[:half], pv[half:])
    lse_rows = []
    for pair in range(n_pairs):
        l0, l1 = lses[2 * pair], lses[2 * pair + 1]
        lse_rows.append(jnp.where(head0, l0, swap_heads(l1)))
        lse_rows.append(jnp.where(head0, swap_heads(l0), l1))
    lse_t = jnp.concatenate(lse_rows + [jnp.zeros((LANES - C_HEADS, LANES), F32)], axis=0)
    lse_tmp[...] = lse_t.T
    for static_res in range(dil):
        @pl.when(res == static_res)
        def _():
            for c in range(o_ref.shape[0]):
                o_ref[c, pl.ds(static_res, blk, stride=dil), :] = o_tmp[:, c * LANES:(c + 1) * LANES]
            lse_ref[pl.ds(static_res, blk, stride=dil), :] = lse_tmp[...]


def dilated_attention_pattern(view, dil):
    sub = view.shape[0]
    t_len = sub * dil
    d = D_MODEL
    blk = ATTN_BLOCK
    cur = lambda part: pl.BlockSpec((blk, d), lambda i, r: (i, 3 * r + part))
    prv = lambda part: pl.BlockSpec((blk, d), lambda i, r: (jnp.maximum(i - 1, 0), 3 * r + part))
    return pl.pallas_call(
        functools.partial(_attn_kernel, dil=dil),
        grid=(sub // blk, dil),
        in_specs=[cur(0), prv(1), cur(1), prv(2), cur(2)],
        out_specs=[pl.BlockSpec((d // LANES, blk * dil, LANES), lambda i, r: (0, i, 0)),
                   pl.BlockSpec((blk * dil, LANES), lambda i, r: (i, 0))],
        out_shape=[jax.ShapeDtypeStruct((d // LANES, t_len, LANES), F32),
                   jax.ShapeDtypeStruct((t_len, LANES), F32)],
        scratch_shapes=[pltpu.VMEM((blk, d), F32), pltpu.VMEM((blk, LANES), F32)],
        compiler_params=_params(2),
        name=f"dilated_attn_d{dil}",
    )(view, view, view, view, view)


def _attn_combine_proj_kernel(x_ref, o1_ref, o2_ref, o3_ref, l1_ref, l2_ref, l3_ref, ex_ref, w_ref,
                              o_ref):
    lses = [l1_ref[...], l2_ref[...], l3_ref[...]]
    mx = jnp.maximum(jnp.maximum(lses[0], lses[1]), lses[2])
    es = [jnp.exp(l - mx) for l in lses]
    den = es[0] + es[1] + es[2]
    y = jnp.zeros(o_ref.shape, F32)
    for e, o in zip(es, (o1_ref, o2_ref, o3_ref)):
        o_full = jnp.concatenate([o[c] for c in range(o.shape[0])], axis=-1)
        y = y + _exact_rhs(e / den, ex_ref[...]) * o_full
    o_ref[...] = x_ref[...] + _dot(y.astype(BF16), w_ref[...])


def attn_combine_proj(x, outs, lses, w_out):
    t_len, d = x.shape
    tm = min(ROW_TILE, t_len)
    head_of_lane = jnp.arange(d) // C_HEAD
    expand = (jnp.arange(LANES)[:, None] == head_of_lane[None, :]).astype(BF16)
    wide = pl.BlockSpec((tm, d), lambda i: (i, 0))
    chunked = pl.BlockSpec((d // LANES, tm, LANES), lambda i: (0, i, 0))
    narrow = pl.BlockSpec((tm, LANES), lambda i: (i, 0))
    return pl.pallas_call(
        _attn_combine_proj_kernel,
        grid=(t_len // tm,),
        in_specs=[wide, chunked, chunked, chunked, narrow, narrow, narrow,
                  pl.BlockSpec((LANES, d), lambda i: (0, 0)),
                  pl.BlockSpec((d, d), lambda i: (0, 0))],
        out_specs=wide,
        out_shape=jax.ShapeDtypeStruct((t_len, d), F32),
        compiler_params=_params(1),
        name="attn_combine_proj",
    )(x, *outs, *lses, expand, w_out.astype(BF16))


def _ffn_kernel(x_ref, g_ref, wu_ref, cw_ref, cb_ref, wd_ref, fn_ref,
                o_ref, h_ref, zg0_ref, zg1_ref, zv0_ref, zv1_ref, carry_g_ref, carry_v_ref, *,
                final_norm):
    tm = x_ref.shape[0]
    n_j, tf = wd_ref.shape[0], wd_ref.shape[1]
    halo = SUBLANES
    zg_refs = (zg0_ref, zg1_ref)
    zv_refs = (zv0_ref, zv1_ref)

    def tile_cols(j, base):
        if isinstance(j, int):
            return pl.ds(base + j * tf, tf)
        return pl.ds(pl.multiple_of(base + j * tf, tf), tf)

    @pl.when(pl.program_id(0) == 0)
    def _():
        carry_g_ref[...] = jnp.zeros_like(carry_g_ref)
        carry_v_ref[...] = jnp.zeros_like(carry_v_ref)

    x = x_ref[...]
    h_ref[...] = _rmsnorm(x, g_ref[...]).astype(BF16)
    o_ref[...] = x

    def up(j, slot):
        h = h_ref[...]
        for base, z_ref, carry_ref in ((0, zg_refs[slot], carry_g_ref),
                                       (D_FF, zv_refs[slot], carry_v_ref)):
            z = _dot(h, wu_ref[:, tile_cols(j, base)])
            z_ref[0:halo, :] = carry_ref[j]
            z_ref[halo:halo + tm, :] = z
            carry_ref[j] = z[tm - halo:tm, :]

    def finish(j, slot):
        def conv(z_ref, base):
            z = z_ref[halo:halo + tm, :]
            z1 = z_ref[halo - 1:halo - 1 + tm, :]
            z2 = z_ref[halo - 2:halo - 2 + tm, :]
            cw = cw_ref[:, tile_cols(j, base)]
            return (cw[0:1, :] * z2 + cw[1:2, :] * z1 + cw[2:3, :] * z
                    + cb_ref[:, tile_cols(j, base)])

        gate = conv(zg_refs[slot], 0)
        val = conv(zv_refs[slot], D_FF)
        act = (gate * jax.nn.sigmoid(gate) * val).astype(BF16)
        o_ref[...] += _dot(act, wd_ref[j])

    up(0, 0)

    def body(pair, carry):
        j = 2 * pair
        up(j + 1, 1)
        finish(j, 0)
        up(j + 2, 0)
        finish(j + 1, 1)
        return carry

    assert n_j % 2 == 1
    lax.fori_loop(0, (n_j - 1) // 2, body, 0)
    finish(n_j - 1, 0)

    if final_norm:
        o_ref[...] = _rmsnorm(o_ref[...], fn_ref[...])


def conv_glu_ffn(x, norm_g, w_up, conv_w, conv_b, w_down, final_g, final_norm):
    t_len, d = x.shape
    tm = min(FFN_ROW_TILE, t_len)
    tf = FFN_COL_TILE
    n_j = D_FF // tf
    full = lambda shape: pl.BlockSpec(shape, lambda i: (0,) * len(shape),
                                      pipeline_mode=pl.Buffered(1))
    return pl.pallas_call(
        functools.partial(_ffn_kernel, final_norm=final_norm),
        grid=(t_len // tm,),
        in_specs=[pl.BlockSpec((tm, d), lambda i: (i, 0)),
                  full((1, d)),
                  full((d, 2 * D_FF)),
                  full((3, 2 * D_FF)),
                  full((1, 2 * D_FF)),
                  full((n_j, tf, d)),
                  full((1, d))],
        out_specs=pl.BlockSpec((tm, d), lambda i: (i, 0)),
        out_shape=jax.ShapeDtypeStruct((t_len, d), F32),
        scratch_shapes=[pltpu.VMEM((tm, d), BF16)]
                       + [pltpu.VMEM((tm + SUBLANES, tf), F32)] * 4
                       + [pltpu.VMEM((n_j, SUBLANES, tf), F32),
                        pltpu.VMEM((n_j, SUBLANES, tf), F32)],
        compiler_params=_params(1),
        name="conv_glu_ffn",
    )(x, norm_g.reshape(1, d), w_up.astype(BF16), conv_w, conv_b.reshape(1, 2 * D_FF),
      w_down.astype(BF16).reshape(n_j, tf, d), final_g.reshape(1, d))


def even_mixer(x, norm, w_in, ln_g, ln_b, w_s, b_s, mu, w0, w2, a0, a2, g2, k_k, k_a, r_k, gn_g,
               gn_b, w_out):
    y_a, p_b = even_in(x, norm, w_in, ln_g, ln_b, w_s, b_s)
    y_b = rwkv_mixer(p_b, mu, w0, w2, a0, a2, g2, k_k, k_a, r_k, gn_g, gn_b)
    return proj_residual(x, y_a, y_b, w_out)


def odd_mixer(x, norm, w_qkv, w_out):
    dilations = tuple(dil for _, dil in C_PATTERNS)
    views = norm_matmul_views(x, norm, w_qkv.astype(BF16), dilations)
    outs, lses = [], []
    for view, dil in zip(views, dilations):
        o, l = dilated_attention_pattern(view, dil)
        outs.append(o)
        lses.append(l)
    return attn_combine_proj(x, outs, lses, w_out)


def kernel(x, ev_norm, ev_w_in, ev_ln_g, ev_ln_b, ev_w_s, ev_b_s, ev_mu, ev_w0, ev_w2, ev_a0, ev_a2, ev_g2, ev_k_k, ev_k_a, ev_r_k, ev_gn_g, ev_gn_b, ev_w_out, od_norm, od_w_qkv, od_w_out, ff_norm, ff_w_up, ff_conv_w, ff_conv_b, ff_w_down, final_norm):
    bsz, t_len, d = x.shape
    depth = ff_norm.shape[0]
    outs = []
    for bi in range(bsz):
        xb = x[bi]
        for layer in range(depth):
            j = layer // 2
            if layer % 2 == 0:
                xb = even_mixer(xb, ev_norm[j], ev_w_in[j], ev_ln_g[j], ev_ln_b[j], ev_w_s[j],
                                ev_b_s[j], ev_mu[j], ev_w0[j], ev_w2[j], ev_a0[j], ev_a2[j],
                                ev_g2[j], ev_k_k[j], ev_k_a[j], ev_r_k[j], ev_gn_g[j], ev_gn_b[j],
                                ev_w_out[j])
            else:
                xb = odd_mixer(xb, od_norm[j], od_w_qkv[j], od_w_out[j])
            xb = conv_glu_ffn(xb, ff_norm[layer], ff_w_up[layer], ff_conv_w[layer],
                              ff_conv_b[layer], ff_w_down[layer], final_norm,
                              final_norm=(layer == depth - 1))
        outs.append(xb)
    return jnp.stack(outs, axis=0)
```

```python
import functools

import jax
import jax.numpy as jnp
from jax import lax
from jax.experimental import pallas as pl
from jax.experimental.pallas import tpu as pltpu

F32 = jnp.float32
BF16 = jnp.bfloat16

D_MODEL = 1024
A_WIDTH = 512
A_GROUP = 128
A_GROUPS = 4
A_CHUNK = 128
LN_EPS = 1e-5
B_WIDTH = 512
B_HEAD = 64
LORA_W = 64
LORA_A = 64
LORA_G = 128
GN_EPS = 64e-5
SHIFT_W = 3 * B_WIDTH + LORA_W + LORA_A + LORA_G
C_HEAD = 64
C_HEADS = 16
C_PATTERNS = ((128, 1), (512, 4), (2048, 16))
NEG_INF = -1e30
D_FF = 2816
RMS_EPS = 1e-6

LANES = 128
SUBLANES = 8
HEADS_PER_BLOCK = LANES // B_HEAD
VMEM_LIMIT_BYTES = 52 * 1024 * 1024

ROW_TILE = 512
QKV_ROW_TILE = 256
FFN_ROW_TILE = 512
FFN_COL_TILE = 256
RWKV_CHUNK = 64
RWKV_CHUNKS_PER_STEP = 4
ATTN_BLOCK = 128
ATTN_BLOCKS_PER_STEP = 2
NEUMANN_LEVELS = 5


def _params(n_axes):
    return pltpu.CompilerParams(dimension_semantics=("arbitrary",) * n_axes,
                                vmem_limit_bytes=VMEM_LIMIT_BYTES)


def _rmsnorm(x, g):
    return x * lax.rsqrt(jnp.mean(x * x, axis=-1, keepdims=True) + RMS_EPS) * g


def _dot(a, b):
    return jnp.dot(a, b, preferred_element_type=F32)


def _dot_nt(a, b):
    return lax.dot_general(a, b, (((1,), (1,)), ((), ())), preferred_element_type=F32)


def _split3(x):
    hi = x.astype(BF16)
    r1 = x - hi.astype(F32)
    mid = r1.astype(BF16)
    lo = (r1 - mid.astype(F32)).astype(BF16)
    return hi, mid, lo


def _exact_rhs(x, m_bf16):
    hi, mid, lo = _split3(x)
    return _dot(hi, m_bf16) + (_dot(mid, m_bf16) + _dot(lo, m_bf16))


def _exact_lhs(m_bf16, x):
    hi, mid, lo = _split3(x)
    return _dot(m_bf16, hi) + (_dot(m_bf16, mid) + _dot(m_bf16, lo))


def _norm_matmul_views_kernel(x_ref, g_ref, w_ref, *o_refs, dilations):
    tm = x_ref.shape[0]
    n = w_ref.shape[1]
    h = _rmsnorm(x_ref[...], g_ref[...]).astype(BF16)
    res = _dot(h, w_ref[...]).astype(BF16)
    out_row = lax.broadcasted_iota(jnp.int32, (tm, tm), 0)
    in_row = lax.broadcasted_iota(jnp.int32, (tm, tm), 1)
    for dil, o_ref in zip(dilations, o_refs):
        if dil == 1:
            o_ref[...] = res
            continue
        per = tm // dil
        assert per & (per - 1) == 0
        shift = per.bit_length() - 1
        src_row = (out_row & (per - 1)) * dil + (out_row >> shift)
        grouped = _dot((in_row == src_row).astype(BF16), res).astype(BF16)
        for r in range(dil):
            o_ref[:, r * n:(r + 1) * n] = grouped[r * per:(r + 1) * per, :]


def norm_matmul_views(x, g, w, dilations):
    t_len, d = x.shape
    n = w.shape[1]
    tm = min(QKV_ROW_TILE, t_len)
    return pl.pallas_call(
        functools.partial(_norm_matmul_views_kernel, dilations=dilations),
        grid=(t_len // tm,),
        in_specs=[pl.BlockSpec((tm, d), lambda i: (i, 0)),
                  pl.BlockSpec((1, d), lambda i: (0, 0)),
                  pl.BlockSpec((d, n), lambda i: (0, 0))],
        out_specs=[pl.BlockSpec((tm // dil, dil * n), lambda i: (i, 0)) for dil in dilations],
        out_shape=[jax.ShapeDtypeStruct((t_len // dil, dil * n), BF16) for dil in dilations],
        compiler_params=_params(1),
        name="norm_matmul_views",
    )(x, g.reshape(1, d), w)


def _even_in_kernel(x_ref, g_ref, w_ref, lng_ref, lnb_ref, ws_ref, bs_ref, ya_ref, pb_ref):
    tm = x_ref.shape[0]
    h = _rmsnorm(x_ref[...], g_ref[...]).astype(BF16)
    p = _dot(h, w_ref[...])
    pb_ref[...] = p[:, 2 * A_WIDTH:]

    row = lax.broadcasted_iota(jnp.int32, (A_CHUNK, A_CHUNK), 0)
    col = lax.broadcasted_iota(jnp.int32, (A_CHUNK, A_CHUNK), 1)
    causal = col <= row
    w_masked = [jnp.where(causal, ws_ref[g], 0.0).astype(BF16) for g in range(A_GROUPS)]
    for c in range(tm // A_CHUNK):
        rows = slice(c * A_CHUNK, (c + 1) * A_CHUNK)
        u = p[rows, :A_WIDTH]
        v = p[rows, A_WIDTH:2 * A_WIDTH]
        mean = jnp.mean(v, axis=-1, keepdims=True)
        cen = v - mean
        var = jnp.mean(cen * cen, axis=-1, keepdims=True)
        vn = (cen * lax.rsqrt(var + LN_EPS) * lng_ref[...] + lnb_ref[...]).astype(BF16)
        for g in range(A_GROUPS):
            cols = slice(g * A_GROUP, (g + 1) * A_GROUP)
            mixed = _dot(w_masked[g], vn[:, cols]) + bs_ref[:, g:g + 1]
            ya_ref[rows, cols] = (u[:, cols] * mixed).astype(ya_ref.dtype)


def even_in(x, g, w_in, ln_g, ln_b, w_s, b_s):
    t_len, d = x.shape
    n = w_in.shape[1]
    tm = min(ROW_TILE, t_len)
    full = lambda shape: pl.BlockSpec(shape, lambda i: (0,) * len(shape))
    return pl.pallas_call(
        _even_in_kernel,
        grid=(t_len // tm,),
        in_specs=[pl.BlockSpec((tm, d), lambda i: (i, 0)),
                  full((1, d)), full((d, n)), full((1, A_WIDTH)), full((1, A_WIDTH)),
                  full((A_GROUPS, A_CHUNK, A_CHUNK)), full((A_CHUNK, A_GROUPS))],
        out_specs=[pl.BlockSpec((tm, A_WIDTH), lambda i: (i, 0)),
                   pl.BlockSpec((tm, SHIFT_W), lambda i: (i, 0))],
        out_shape=[jax.ShapeDtypeStruct((t_len, A_WIDTH), BF16),
                   jax.ShapeDtypeStruct((t_len, SHIFT_W), F32)],
        compiler_params=_params(1),
        name="even_in",
    )(x, g.reshape(1, d), w_in.astype(BF16), ln_g.reshape(1, A_WIDTH), ln_b.reshape(1, A_WIDTH),
      w_s, b_s.T)


def _head_sums(x):
    lane = lax.broadcasted_iota(jnp.int32, (1, LANES), 1)
    head0 = lane < B_HEAD
    outs = []
    for blk in range(x.shape[1] // LANES):
        xb = x[:, blk * LANES:(blk + 1) * LANES]
        s0 = jnp.sum(jnp.where(head0, xb, 0.0), axis=-1, keepdims=True)
        s1 = jnp.sum(jnp.where(head0, 0.0, xb), axis=-1, keepdims=True)
        outs.append(jnp.where(head0, s0, s1))
    return jnp.concatenate(outs, axis=-1)


def _rwkv_prep_tile(p, prev_row, prm, buf):
    mu_ref, w0_ref, wa_ref, a0_ref, g2_ref, kkw_ref, kaw_ref = prm
    r_o, k_o, v_o, kk_o, b_o, lw_o, g_o = buf
    tm = p.shape[0]
    rowid = lax.broadcasted_iota(jnp.int32, (tm, 1), 0)
    prev = jnp.where(rowid == 0, prev_row, pltpu.roll(p, 1, 0))
    ps = p + (prev - p) * mu_ref[...]

    r = ps[:, :B_WIDTH]
    k = ps[:, B_WIDTH:2 * B_WIDTH]
    v = ps[:, 2 * B_WIDTH:3 * B_WIDTH]
    lwa = ps[:, 3 * B_WIDTH:3 * B_WIDTH + LORA_W + LORA_A]
    lg = ps[:, 3 * B_WIDTH + LORA_W + LORA_A:]

    lane = lax.broadcasted_iota(jnp.int32, (1, LORA_W + LORA_A), 1)
    lora_in = jnp.where(lane < LORA_W, jnp.tanh(lwa), lwa).astype(BF16)
    lora = _dot(lora_in, wa_ref[...])
    neg = -(w0_ref[...] + lora[:, :B_WIDTH])
    softplus = jnp.maximum(neg, 0.0) + jnp.log1p(jnp.exp(-jnp.abs(neg)))
    w = -softplus - 0.5
    a = jax.nn.sigmoid(a0_ref[...] + lora[:, B_WIDTH:])
    g = _dot(jax.nn.sigmoid(lg).astype(BF16), g2_ref[...])

    kkr = k * kkw_ref[...]
    kk = kkr * lax.rsqrt(jnp.maximum(_head_sums(kkr * kkr), 1e-24))

    r_o[...] = r
    k_o[...] = k * (1.0 + (a - 1.0) * kaw_ref[...])
    v_o[...] = v
    kk_o[...] = kk
    b_o[...] = kk * a
    lw_o[...] = -jnp.exp(w)
    g_o[...] = g


def _rwkv_scan_tile(buf, prm, o_ref, out_rows, h_ref, po_ref, q_ref, o2_ref, oacc_ref, n_chunks):
    r_ref, k_ref, v_ref, kk_ref, b_ref, lw_ref, g_ref = buf
    rk_ref, gng_ref, gnb_ref = prm
    c_len = RWKV_CHUNK
    n_blocks = B_WIDTH // LANES

    lane = lax.broadcasted_iota(jnp.int32, (1, LANES), 1)
    head0 = lane < B_HEAD
    rho = lax.broadcasted_iota(jnp.int32, (LANES, LANES), 0)
    sig = lax.broadcasted_iota(jnp.int32, (LANES, LANES), 1)
    same_head = (rho // c_len) == (sig // c_len)
    strict = same_head & ((sig % c_len) < (rho % c_len))
    incl = same_head & ((sig % c_len) <= (rho % c_len))
    eye = rho == sig
    tm = n_chunks * c_len
    tri_r = lax.broadcasted_iota(jnp.int32, (tm, tm), 0)
    tri_c = lax.broadcasted_iota(jnp.int32, (tm, tm), 1)
    cum_mat = ((tri_r // c_len == tri_c // c_len) & (tri_c <= tri_r)).astype(BF16)

    def stack(x):
        return jnp.concatenate([jnp.where(head0, x, 0.0), jnp.where(head0, 0.0, x)], axis=0)

    lw_all = lw_ref[...]
    cum_all = _exact_lhs(cum_mat, lw_all)
    chains = []
    for c in range(n_chunks):
        rows = slice(c * c_len, (c + 1) * c_len)
        cum = cum_all[rows, :]
        total = cum[c_len - 1:c_len, :]
        grow = jnp.exp(cum)
        inv = jnp.exp(-cum)
        to_end = jnp.exp(total - cum)
        decay = jnp.exp(total)
        r_t = r_ref[rows, :] * grow
        kk_t = kk_ref[rows, :] * jnp.exp(cum - lw_all[rows, :])
        b_t = b_ref[rows, :] * inv
        k_t = k_ref[rows, :] * inv
        b_e = b_ref[rows, :] * to_end
        k_e = k_ref[rows, :] * to_end
        v = v_ref[rows, :]
        for blk in range(n_blocks):
            cols = slice(blk * LANES, (blk + 1) * LANES)
            kk_s = stack(kk_t[:, cols]).astype(BF16)
            r_s = stack(r_t[:, cols])
            a_all = _dot_nt(
                jnp.concatenate([kk_s, r_s.astype(BF16)], axis=0),
                jnp.concatenate([stack(b_t[:, cols]), stack(k_t[:, cols])], axis=0).astype(BF16))
            neg_l = -jnp.where(strict, a_all[:LANES, :LANES], 0.0)
            v_lhs = jnp.concatenate(
                [stack(k_e[:, cols]).T,
                 jnp.where(incl, a_all[LANES:, LANES:], 0.0),
                 jnp.where(strict, a_all[:LANES, LANES:], 0.0)], axis=0).astype(BF16)
            w_lhs = jnp.concatenate(
                [stack(b_e[:, cols]).T, jnp.where(incl, a_all[LANES:, :LANES], 0.0)],
                axis=0).astype(BF16)
            chains.append(dict(kk_s=kk_s, r_s=r_s, v_s=stack(v[:, cols]).astype(BF16),
                               v_lhs=v_lhs, w_lhs=w_lhs, decay=decay[:, cols],
                               power=neg_l.astype(BF16),
                               t_inv=jnp.where(eye, 1.0, 0.0) + neg_l))

    for ch in chains:
        ch["power"] = _dot(ch["power"], ch["power"]).astype(BF16)
    for level in range(1, NEUMANN_LEVELS):
        for ch in chains:
            both = _dot(jnp.concatenate([ch["power"], ch["t_inv"].astype(BF16)], axis=0),
                        ch["power"])
            ch["power"] = both[:LANES].astype(BF16)
            ch["t_inv"] = ch["t_inv"] + both[LANES:]
    for ch in chains:
        ch["t_inv"] = (ch["t_inv"] + _dot(ch["t_inv"].astype(BF16), ch["power"])).astype(BF16)

    for ch in chains:
        ch["xv"] = _dot(ch["v_lhs"], ch["v_s"])
    for ch in chains:
        rhs = jnp.concatenate([ch["kk_s"], ch["xv"][2 * LANES:].astype(BF16)], axis=1)
        ch["w12"] = _dot(ch["t_inv"], rhs).astype(BF16)
    for idx, ch in enumerate(chains):
        xw = _dot(ch["w_lhs"], ch["w12"])
        p_mat = jnp.where(eye, ch["decay"], 0.0) - xw[:LANES, :LANES]
        o1 = ch["r_s"] - xw[LANES:, :LANES]
        po_ref[idx] = jnp.concatenate([p_mat, o1], axis=0).astype(BF16)
        q_ref[idx] = ch["xv"][:LANES] - xw[:LANES, LANES:]
        o2_ref[idx] = ch["xv"][LANES:2 * LANES] - xw[LANES:, LANES:]

    states = [h_ref[blk] for blk in range(n_blocks)]
    for c in range(n_chunks):
        rows = slice(c * c_len, (c + 1) * c_len)
        pos = [_dot(po_ref[c * n_blocks + blk], states[blk].astype(BF16))
               for blk in range(n_blocks)]
        for blk in range(n_blocks):
            chain = c * n_blocks + blk
            states[blk] = pos[blk][:LANES] + q_ref[chain]
            o_st = pos[blk][LANES:] + o2_ref[chain]
            oacc_ref[rows, blk * LANES:(blk + 1) * LANES] = o_st[:c_len] + o_st[c_len:]
    for blk in range(n_blocks):
        h_ref[blk] = states[blk]

    o = oacc_ref[...]
    mean = _head_sums(o) * (1.0 / B_HEAD)
    cen = o - mean
    var = _head_sums(cen * cen) * (1.0 / B_HEAD)
    normed = cen * lax.rsqrt(var + GN_EPS) * gng_ref[...] + gnb_ref[...]
    bonus = _head_sums(r_ref[...] * k_ref[...] * rk_ref[...]) * v_ref[...]
    o_ref[out_rows, :] = ((normed + bonus) * g_ref[...]).astype(o_ref.dtype)


def _rwkv_kernel(*refs, n_chunks):
    p_ref, pn_ref = refs[0:2]
    prep_prm = refs[2:9]
    scan_prm = refs[9:12]
    o_ref = refs[12]
    buf_a = refs[13:20]
    buf_b = refs[20:27]
    h_ref, po_ref, q_ref, o2_ref, oacc_ref = refs[27:32]
    tm = n_chunks * RWKV_CHUNK
    scan_scratch = (h_ref, po_ref, q_ref, o2_ref, oacc_ref, n_chunks)

    @pl.when(pl.program_id(0) == 0)
    def _():
        h_ref[...] = jnp.zeros_like(h_ref)
        _rwkv_prep_tile(p_ref[0:tm, :], jnp.zeros((1, SHIFT_W), F32), prep_prm, buf_a)

    _rwkv_prep_tile(p_ref[tm:2 * tm, :], p_ref[tm - 1:tm, :], prep_prm, buf_b)
    _rwkv_scan_tile(buf_a, scan_prm, o_ref, slice(0, tm), *scan_scratch)
    _rwkv_prep_tile(pn_ref[...], p_ref[2 * tm - 1:2 * tm, :], prep_prm, buf_a)
    _rwkv_scan_tile(buf_b, scan_prm, o_ref, slice(tm, 2 * tm), *scan_scratch)


def rwkv_mixer(p_b, mu, w0, w2, a0, a2, g2, k_k, k_a, r_k, gn_g, gn_b):
    t_len = p_b.shape[0]
    n_chunks = RWKV_CHUNKS_PER_STEP
    tm = n_chunks * RWKV_CHUNK
    n_tiles = t_len // tm
    n_blocks = B_WIDTH // LANES
    n_chains = n_chunks * n_blocks
    zeros = jnp.zeros((LORA_W, B_WIDTH), F32)
    wa = jnp.concatenate([jnp.concatenate([w2, zeros], 1),
                          jnp.concatenate([zeros, a2], 1)], 0).astype(BF16)
    row = lambda z: z.reshape(1, -1)
    full = lambda shape: pl.BlockSpec(shape, lambda i: (0,) * len(shape))
    vec = full((1, B_WIDTH))
    tile_buf = [pltpu.VMEM((tm, B_WIDTH), F32)] * 7
    return pl.pallas_call(
        functools.partial(_rwkv_kernel, n_chunks=n_chunks),
        grid=(n_tiles // 2,),
        in_specs=[pl.BlockSpec((2 * tm, SHIFT_W), lambda i: (i, 0)),
                  pl.BlockSpec((tm, SHIFT_W), lambda i: (jnp.minimum(2 * i + 2, n_tiles - 1), 0)),
                  full((1, SHIFT_W)), vec, full((LORA_W + LORA_A, 2 * B_WIDTH)), vec,
                  full((LORA_G, B_WIDTH)), vec, vec,
                  vec, vec, vec],
        out_specs=pl.BlockSpec((2 * tm, B_WIDTH), lambda i: (i, 0)),
        out_shape=jax.ShapeDtypeStruct((t_len, B_WIDTH), BF16),
        scratch_shapes=tile_buf + tile_buf
                       + [pltpu.VMEM((n_blocks, LANES, LANES), F32),
                          pltpu.VMEM((n_chains, 2 * LANES, LANES), BF16),
                          pltpu.VMEM((n_chains, LANES, LANES), F32),
                          pltpu.VMEM((n_chains, LANES, LANES), F32),
                          pltpu.VMEM((tm, B_WIDTH), F32)],
        compiler_params=_params(1),
        name="rwkv_mixer",
    )(p_b, p_b, row(mu), row(w0), wa, row(a0), g2.astype(BF16), row(k_k), row(k_a),
      row(r_k), row(gn_g), row(gn_b))


def _proj_residual_kernel(x_ref, ya_ref, yb_ref, w_ref, o_ref):
    o_ref[...] = (x_ref[...] + _dot(ya_ref[...], w_ref[:A_WIDTH, :])
                  + _dot(yb_ref[...], w_ref[A_WIDTH:, :]))


def proj_residual(x, y_a, y_b, w_out):
    t_len, d = x.shape
    tm = min(ROW_TILE, t_len)
    half = pl.BlockSpec((tm, A_WIDTH), lambda i: (i, 0))
    return pl.pallas_call(
        _proj_residual_kernel,
        grid=(t_len // tm,),
        in_specs=[pl.BlockSpec((tm, d), lambda i: (i, 0)), half, half,
                  pl.BlockSpec((d, d), lambda i: (0, 0))],
        out_specs=pl.BlockSpec((tm, d), lambda i: (i, 0)),
        out_shape=jax.ShapeDtypeStruct((t_len, d), F32),
        compiler_params=_params(1),
        name="proj_residual",
    )(x, y_a, y_b, w_out.astype(BF16))


def _attn_kernel(q_ref, kp_ref, kc_ref, vp_ref, vc_ref, o_ref, lse_ref, o_tmp, lse_tmp, *, dil,
                 n_qb):
    blk = ATTN_BLOCK
    half = blk // 2
    span = blk + half
    assert half == C_HEAD
    i = pl.program_id(0)
    res = pl.program_id(1)
    lane = lax.broadcasted_iota(jnp.int32, (1, LANES), 1)
    head0 = lane < C_HEAD
    key = lax.broadcasted_iota(jnp.int32, (span, LANES), 0)
    qry = lax.broadcasted_iota(jnp.int32, (span, LANES), 1) % half
    dist = qry + blk - key
    in_window = (dist >= 0) & (dist <= blk)
    first_valid = [in_window & ((key >= blk - sub * half) | (i > 0)) for sub in range(2)]
    swap_heads = lambda z: pltpu.roll(z, C_HEAD, 1)
    n_pairs = C_HEADS // HEADS_PER_BLOCK
    units = [(pair, sub) for pair in range(n_pairs) for sub in range(2)]

    for qb in range(n_qb):
        q_rows = slice(qb * blk, (qb + 1) * blk)
        prev_k, prev_v = (kp_ref, vp_ref) if qb == 0 else (kc_ref.at[(qb - 1) * blk:qb * blk],
                                                           vc_ref.at[(qb - 1) * blk:qb * blk])
        cur_k, cur_v = kc_ref.at[q_rows], vc_ref.at[q_rows]
        valid = first_valid if qb == 0 else [in_window, in_window]
        scores = []
        for pair, sub in units:
            cols = slice(pair * LANES, (pair + 1) * LANES)
            qsub = q_ref[qb * blk + sub * half:qb * blk + (sub + 1) * half, cols]
            qsub = qsub * (C_HEAD ** -0.5)
            zero = jnp.zeros_like(qsub)
            qs = jnp.concatenate([jnp.where(head0, qsub, zero), jnp.where(head0, zero, qsub)], axis=0)
            ks = jnp.concatenate([prev_k[sub * half:, cols], cur_k[:(sub + 1) * half, cols]], axis=0)
            scores.append(jnp.where(valid[sub], _dot_nt(ks, qs), NEG_INF))
        probs, lses = [], []
        for st in scores:
            mx = jnp.max(st, axis=0, keepdims=True)
            e = jnp.exp(st - mx)
            den = jnp.sum(e, axis=0, keepdims=True)
            probs.append((e * (1.0 / den)).astype(BF16))
            lses.append(mx + jnp.log(den))
        for (pair, sub), prob in zip(units, probs):
            cols = slice(pair * LANES, (pair + 1) * LANES)
            vs = jnp.concatenate([prev_v[sub * half:, cols], cur_v[:(sub + 1) * half, cols]], axis=0)
            pv = lax.dot_general(prob, vs, (((0,), (0,)), ((), ())),
                                 preferred_element_type=F32)
            o_tmp[qb * blk + sub * half:qb * blk + (sub + 1) * half, cols] = (
                jnp.where(head0, pv[:half], pv[half:]))
        lse_rows = []
        for pair in range(n_pairs):
            l0, l1 = lses[2 * pair], lses[2 * pair + 1]
            lse_rows.append(jnp.where(head0, l0, swap_heads(l1)))
            lse_rows.append(jnp.where(head0, swap_heads(l0), l1))
        lse_t = jnp.concatenate(lse_rows + [jnp.zeros((LANES - C_HEADS, LANES), F32)], axis=0)
        lse_tmp[q_rows, :] = lse_t.T

    for static_res in range(dil):
        @pl.when(res == static_res)
        def _():
            rows = pl.ds(static_res, n_qb * blk, stride=dil)
            for c in range(o_ref.shape[0]):
                o_ref[c, rows, :] = o_tmp[:, c * LANES:(c + 1) * LANES]
            lse_ref[rows, :] = lse_tmp[...]


def dilated_attention_pattern(view, dil):
    sub = view.shape[0]
    t_len = sub * dil
    d = D_MODEL
    blk = ATTN_BLOCK
    n_qb = ATTN_BLOCKS_PER_STEP
    rows = n_qb * blk
    cur = lambda part: pl.BlockSpec((rows, d), lambda i, r: (i, 3 * r + part))
    prv = lambda part: pl.BlockSpec((blk, d),
                                    lambda i, r: (jnp.maximum(n_qb * i - 1, 0), 3 * r + part))
    return pl.pallas_call(
        functools.partial(_attn_kernel, dil=dil, n_qb=n_qb),
        grid=(sub // rows, dil),
        in_specs=[cur(0), prv(1), cur(1), prv(2), cur(2)],
        out_specs=[pl.BlockSpec((d // LANES, rows * dil, LANES), lambda i, r: (0, i, 0)),
                   pl.BlockSpec((rows * dil, LANES), lambda i, r: (i, 0))],
        out_shape=[jax.ShapeDtypeStruct((d // LANES, t_len, LANES), F32),
                   jax.ShapeDtypeStruct((t_len, LANES), F32)],
        scratch_shapes=[pltpu.VMEM((rows, d), F32), pltpu.VMEM((rows, LANES), F32)],
        compiler_params=_params(2),
        name=f"dilated_attn_d{dil}",
    )(view, view, view, view, view)


def _attn_combine_proj_kernel(x_ref, o1_ref, o2_ref, o3_ref, l1_ref, l2_ref, l3_ref, ex_ref, w_ref,
                              o_ref):
    lses = [l1_ref[...], l2_ref[...], l3_ref[...]]
    mx = jnp.maximum(jnp.maximum(lses[0], lses[1]), lses[2])
    es = [jnp.exp(l - mx) for l in lses]
    den = es[0] + es[1] + es[2]
    y = jnp.zeros(o_ref.shape, F32)
    for e, o in zip(es, (o1_ref, o2_ref, o3_ref)):
        o_full = jnp.concatenate([o[c] for c in range(o.shape[0])], axis=-1)
        y = y + _exact_rhs(e / den, ex_ref[...]) * o_full
    o_ref[...] = x_ref[...] + _dot(y.astype(BF16), w_ref[...])


def attn_combine_proj(x, outs, lses, w_out):
    t_len, d = x.shape
    tm = min(ROW_TILE, t_len)
    head_of_lane = jnp.arange(d) // C_HEAD
    expand = (jnp.arange(LANES)[:, None] == head_of_lane[None, :]).astype(BF16)
    wide = pl.BlockSpec((tm, d), lambda i: (i, 0))
    chunked = pl.BlockSpec((d // LANES, tm, LANES), lambda i: (0, i, 0))
    narrow = pl.BlockSpec((tm, LANES), lambda i: (i, 0))
    return pl.pallas_call(
        _attn_combine_proj_kernel,
        grid=(t_len // tm,),
        in_specs=[wide, chunked, chunked, chunked, narrow, narrow, narrow,
                  pl.BlockSpec((LANES, d), lambda i: (0, 0)),
                  pl.BlockSpec((d, d), lambda i: (0, 0))],
        out_specs=wide,
        out_shape=jax.ShapeDtypeStruct((t_len, d), F32),
        compiler_params=_params(1),
        name="attn_combine_proj",
    )(x, *outs, *lses, expand, w_out.astype(BF16))


def _ffn_kernel(x_ref, g_ref, wu_ref, cw_ref, cb_ref, wd_ref, fn_ref,
                o_ref, h_ref, zg0_ref, zg1_ref, zv0_ref, zv1_ref, carry_g_ref, carry_v_ref, *,
                final_norm):
    tm = x_ref.shape[0]
    n_j, tf = wd_ref.shape[0], wd_ref.shape[1]
    halo = SUBLANES
    zg_refs = (zg0_ref, zg1_ref)
    zv_refs = (zv0_ref, zv1_ref)

    def tile_cols(j, base):
        if isinstance(j, int):
            return pl.ds(base + j * tf, tf)
        return pl.ds(pl.multiple_of(base + j * tf, tf), tf)

    @pl.when(pl.program_id(0) == 0)
    def _():
        carry_g_ref[...] = jnp.zeros_like(carry_g_ref)
        carry_v_ref[...] = jnp.zeros_like(carry_v_ref)

    x = x_ref[...]
    h_ref[...] = _rmsnorm(x, g_ref[...]).astype(BF16)
    o_ref[...] = x

    def up(j, slot):
        h = h_ref[...]
        for base, z_ref, carry_ref in ((0, zg_refs[slot], carry_g_ref),
                                       (D_FF, zv_refs[slot], carry_v_ref)):
            z = _dot(h, wu_ref[:, tile_cols(j, base)])
            z_ref[0:halo, :] = carry_ref[j]
            z_ref[halo:halo + tm, :] = z
            carry_ref[j] = z[tm - halo:tm, :]

    def finish(j, slot):
        def conv(z_ref, base):
            z = z_ref[halo:halo + tm, :]
            z1 = z_ref[halo - 1:halo - 1 + tm, :]
            z2 = z_ref[halo - 2:halo - 2 + tm, :]
            cw = cw_ref[:, tile_cols(j, base)]
            return (cw[0:1, :] * z2 + cw[1:2, :] * z1 + cw[2:3, :] * z
                    + cb_ref[:, tile_cols(j, base)])

        gate = conv(zg_refs[slot], 0)
        val = conv(zv_refs[slot], D_FF)
        act = (gate * jax.nn.sigmoid(gate) * val).astype(BF16)
        o_ref[...] += _dot(act, wd_ref[j])

    up(0, 0)

    def body(pair, carry):
        j = 2 * pair
        up(j + 1, 1)
        finish(j, 0)
        up(j + 2, 0)
        finish(j + 1, 1)
        return carry

    assert n_j % 2 == 1
    lax.fori_loop(0, (n_j - 1) // 2, body, 0)
    finish(n_j - 1, 0)

    if final_norm:
        o_ref[...] = _rmsnorm(o_ref[...], fn_ref[...])


def conv_glu_ffn(x, norm_g, w_up, conv_w, conv_b, w_down, final_g, final_norm):
    t_len, d = x.shape
    tm = min(FFN_ROW_TILE, t_len)
    tf = FFN_COL_TILE
    n_j = D_FF // tf
    full = lambda shape: pl.BlockSpec(shape, lambda i: (0,) * len(shape),
                                      pipeline_mode=pl.Buffered(1))
    return pl.pallas_call(
        functools.partial(_ffn_kernel, final_norm=final_norm),
        grid=(t_len // tm,),
        in_specs=[pl.BlockSpec((tm, d), lambda i: (i, 0)),
                  full((1, d)),
                  full((d, 2 * D_FF)),
                  full((3, 2 * D_FF)),
                  full((1, 2 * D_FF)),
                  full((n_j, tf, d)),
                  full((1, d))],
        out_specs=pl.BlockSpec((tm, d), lambda i: (i, 0)),
        out_shape=jax.ShapeDtypeStruct((t_len, d), F32),
        scratch_shapes=[pltpu.VMEM((tm, d), BF16)]
                       + [pltpu.VMEM((tm + SUBLANES, tf), F32)] * 4
                       + [pltpu.VMEM((n_j, SUBLANES, tf), F32),
                        pltpu.VMEM((n_j, SUBLANES, tf), F32)],
        compiler_params=_params(1),
        name="conv_glu_ffn",
    )(x, norm_g.reshape(1, d), w_up.astype(BF16), conv_w, conv_b.reshape(1, 2 * D_FF),
      w_down.astype(BF16).reshape(n_j, tf, d), final_g.reshape(1, d))


def even_mixer(x, norm, w_in, ln_g, ln_b, w_s, b_s, mu, w0, w2, a0, a2, g2, k_k, k_a, r_k, gn_g,
               gn_b, w_out):
    y_a, p_b = even_in(x, norm, w_in, ln_g, ln_b, w_s, b_s)
    y_b = rwkv_mixer(p_b, mu, w0, w2, a0, a2, g2, k_k, k_a, r_k, gn_g, gn_b)
    return proj_residual(x, y_a, y_b, w_out)


def odd_mixer(x, norm, w_qkv, w_out):
    dilations = tuple(dil for _, dil in C_PATTERNS)
    views = norm_matmul_views(x, norm, w_qkv.astype(BF16), dilations)
    outs, lses = [], []
    for view, dil in zip(views, dilations):
        o, l = dilated_attention_pattern(view, dil)
        outs.append(o)
        lses.append(l)
    return attn_combine_proj(x, outs, lses, w_out)


def kernel(x, ev_norm, ev_w_in, ev_ln_g, ev_ln_b, ev_w_s, ev_b_s, ev_mu, ev_w0, ev_w2, ev_a0, ev_a2, ev_g2, ev_k_k, ev_k_a, ev_r_k, ev_gn_g, ev_gn_b, ev_w_out, od_norm, od_w_qkv, od_w_out, ff_norm, ff_w_up, ff_conv_w, ff_conv_b, ff_w_down, final_norm):
    bsz, t_len, d = x.shape
    depth = ff_norm.shape[0]
    outs = []
    for bi in range(bsz):
        xb = x[bi]
        for layer in range(depth):
            j = layer // 2
            if layer % 2 == 0:
                xb = even_mixer(xb, ev_norm[j], ev_w_in[j], ev_ln_g[j], ev_ln_b[j], ev_w_s[j],
                                ev_b_s[j], ev_mu[j], ev_w0[j], ev_w2[j], ev_a0[j], ev_a2[j],
                                ev_g2[j], ev_k_k[j], ev_k_a[j], ev_r_k[j], ev_gn_g[j], ev_gn_b[j],
                                ev_w_out[j])
            else:
                xb = odd_mixer(xb, od_norm[j], od_w_qkv[j], od_w_out[j])
            xb = conv_glu_ffn(xb, ff_norm[layer], ff_w_up[layer], ff_conv_w[layer],
                              ff_conv_b[layer], ff_w_down[layer], final_norm,
                              final_norm=(layer == depth - 1))
        outs.append(xb)
    return jnp.stack(outs, axis=0)
```

```python
import functools

import jax
import jax.numpy as jnp
from jax import lax
from jax.experimental import pallas as pl
from jax.experimental.pallas import tpu as pltpu

F32 = jnp.float32
BF16 = jnp.bfloat16

D_MODEL = 1024
A_WIDTH = 512
A_GROUP = 128
A_GROUPS = 4
A_CHUNK = 128
LN_EPS = 1e-5
B_WIDTH = 512
B_HEAD = 64
LORA_W = 64
LORA_A = 64
LORA_G = 128
GN_EPS = 64e-5
SHIFT_W = 3 * B_WIDTH + LORA_W + LORA_A + LORA_G
C_HEAD = 64
C_HEADS = 16
C_PATTERNS = ((128, 1), (512, 4), (2048, 16))
NEG_INF = -1e30
D_FF = 2816
RMS_EPS = 1e-6

LANES = 128
SUBLANES = 8
HEADS_PER_BLOCK = LANES // B_HEAD
VMEM_LIMIT_BYTES = 52 * 1024 * 1024

ROW_TILE = 512
QKV_ROW_TILE = 256
FFN_ROW_TILE = 512
FFN_COL_TILE = 256
RWKV_CHUNK = 64
RWKV_CHUNKS_PER_STEP = 4
ATTN_BLOCK = 128
ATTN_MAX_BLOCKS_PER_STEP = 4
ATTN_OUT_BLOCK_ROWS = 4096
NEUMANN_LEVELS = 5


def _params(n_axes):
    return pltpu.CompilerParams(dimension_semantics=("arbitrary",) * n_axes,
                                vmem_limit_bytes=VMEM_LIMIT_BYTES)


def _rmsnorm(x, g):
    return x * lax.rsqrt(jnp.mean(x * x, axis=-1, keepdims=True) + RMS_EPS) * g


def _dot(a, b):
    return jnp.dot(a, b, preferred_element_type=F32)


def _dot_nt(a, b):
    return lax.dot_general(a, b, (((1,), (1,)), ((), ())), preferred_element_type=F32)


def _split3(x):
    hi = x.astype(BF16)
    r1 = x - hi.astype(F32)
    mid = r1.astype(BF16)
    lo = (r1 - mid.astype(F32)).astype(BF16)
    return hi, mid, lo


def _exact_rhs(x, m_bf16):
    hi, mid, lo = _split3(x)
    return _dot(hi, m_bf16) + (_dot(mid, m_bf16) + _dot(lo, m_bf16))


def _exact_lhs(m_bf16, x):
    hi, mid, lo = _split3(x)
    return _dot(m_bf16, hi) + (_dot(m_bf16, mid) + _dot(m_bf16, lo))


def _norm_matmul_views_kernel(x_ref, g_ref, w_ref, *o_refs, dilations):
    tm = x_ref.shape[0]
    n = w_ref.shape[1]
    h = _rmsnorm(x_ref[...], g_ref[...]).astype(BF16)
    res = _dot(h, w_ref[...]).astype(BF16)
    out_row = lax.broadcasted_iota(jnp.int32, (tm, tm), 0)
    in_row = lax.broadcasted_iota(jnp.int32, (tm, tm), 1)
    for dil, o_ref in zip(dilations, o_refs):
        if dil == 1:
            o_ref[...] = res
            continue
        per = tm // dil
        assert per & (per - 1) == 0
        shift = per.bit_length() - 1
        src_row = (out_row & (per - 1)) * dil + (out_row >> shift)
        grouped = _dot((in_row == src_row).astype(BF16), res).astype(BF16)
        for r in range(dil):
            o_ref[:, r * n:(r + 1) * n] = grouped[r * per:(r + 1) * per, :]


def norm_matmul_views(x, g, w, dilations):
    t_len, d = x.shape
    n = w.shape[1]
    tm = min(QKV_ROW_TILE, t_len)
    return pl.pallas_call(
        functools.partial(_norm_matmul_views_kernel, dilations=dilations),
        grid=(t_len // tm,),
        in_specs=[pl.BlockSpec((tm, d), lambda i: (i, 0)),
                  pl.BlockSpec((1, d), lambda i: (0, 0)),
                  pl.BlockSpec((d, n), lambda i: (0, 0))],
        out_specs=[pl.BlockSpec((tm // dil, dil * n), lambda i: (i, 0)) for dil in dilations],
        out_shape=[jax.ShapeDtypeStruct((t_len // dil, dil * n), BF16) for dil in dilations],
        compiler_params=_params(1),
        name="norm_matmul_views",
    )(x, g.reshape(1, d), w)


def _even_in_kernel(x_ref, g_ref, w_ref, lng_ref, lnb_ref, ws_ref, bs_ref, ya_ref, pb_ref):
    tm = x_ref.shape[0]
    h = _rmsnorm(x_ref[...], g_ref[...]).astype(BF16)
    p = _dot(h, w_ref[...])
    pb_ref[...] = p[:, 2 * A_WIDTH:]

    row = lax.broadcasted_iota(jnp.int32, (A_CHUNK, A_CHUNK), 0)
    col = lax.broadcasted_iota(jnp.int32, (A_CHUNK, A_CHUNK), 1)
    causal = col <= row
    w_masked = [jnp.where(causal, ws_ref[g], 0.0).astype(BF16) for g in range(A_GROUPS)]
    for c in range(tm // A_CHUNK):
        rows = slice(c * A_CHUNK, (c + 1) * A_CHUNK)
        u = p[rows, :A_WIDTH]
        v = p[rows, A_WIDTH:2 * A_WIDTH]
        mean = jnp.mean(v, axis=-1, keepdims=True)
        cen = v - mean
        var = jnp.mean(cen * cen, axis=-1, keepdims=True)
        vn = (cen * lax.rsqrt(var + LN_EPS) * lng_ref[...] + lnb_ref[...]).astype(BF16)
        for g in range(A_GROUPS):
            cols = slice(g * A_GROUP, (g + 1) * A_GROUP)
            mixed = _dot(w_masked[g], vn[:, cols]) + bs_ref[:, g:g + 1]
            ya_ref[rows, cols] = (u[:, cols] * mixed).astype(ya_ref.dtype)


def even_in(x, g, w_in, ln_g, ln_b, w_s, b_s):
    t_len, d = x.shape
    n = w_in.shape[1]
    tm = min(ROW_TILE, t_len)
    full = lambda shape: pl.BlockSpec(shape, lambda i: (0,) * len(shape))
    return pl.pallas_call(
        _even_in_kernel,
        grid=(t_len // tm,),
        in_specs=[pl.BlockSpec((tm, d), lambda i: (i, 0)),
                  full((1, d)), full((d, n)), full((1, A_WIDTH)), full((1, A_WIDTH)),
                  full((A_GROUPS, A_CHUNK, A_CHUNK)), full((A_CHUNK, A_GROUPS))],
        out_specs=[pl.BlockSpec((tm, A_WIDTH), lambda i: (i, 0)),
                   pl.BlockSpec((tm, SHIFT_W), lambda i: (i, 0))],
        out_shape=[jax.ShapeDtypeStruct((t_len, A_WIDTH), BF16),
                   jax.ShapeDtypeStruct((t_len, SHIFT_W), F32)],
        compiler_params=_params(1),
        name="even_in",
    )(x, g.reshape(1, d), w_in.astype(BF16), ln_g.reshape(1, A_WIDTH), ln_b.reshape(1, A_WIDTH),
      w_s, b_s.T)


def _head_sums(x):
    lane = lax.broadcasted_iota(jnp.int32, (1, LANES), 1)
    head0 = lane < B_HEAD
    outs = []
    for blk in range(x.shape[1] // LANES):
        xb = x[:, blk * LANES:(blk + 1) * LANES]
        s0 = jnp.sum(jnp.where(head0, xb, 0.0), axis=-1, keepdims=True)
        s1 = jnp.sum(jnp.where(head0, 0.0, xb), axis=-1, keepdims=True)
        outs.append(jnp.where(head0, s0, s1))
    return jnp.concatenate(outs, axis=-1)


def _rwkv_prep_tile(p, prev_row, prm, buf):
    mu_ref, w0_ref, wa_ref, a0_ref, g2_ref, kkw_ref, kaw_ref = prm
    r_o, k_o, v_o, kk_o, b_o, lw_o, g_o = buf
    tm = p.shape[0]
    rowid = lax.broadcasted_iota(jnp.int32, (tm, 1), 0)
    prev = jnp.where(rowid == 0, prev_row, pltpu.roll(p, 1, 0))
    ps = p + (prev - p) * mu_ref[...]

    r = ps[:, :B_WIDTH]
    k = ps[:, B_WIDTH:2 * B_WIDTH]
    v = ps[:, 2 * B_WIDTH:3 * B_WIDTH]
    lwa = ps[:, 3 * B_WIDTH:3 * B_WIDTH + LORA_W + LORA_A]
    lg = ps[:, 3 * B_WIDTH + LORA_W + LORA_A:]

    lane = lax.broadcasted_iota(jnp.int32, (1, LORA_W + LORA_A), 1)
    lora_in = jnp.where(lane < LORA_W, jnp.tanh(lwa), lwa).astype(BF16)
    lora = _dot(lora_in, wa_ref[...])
    neg = -(w0_ref[...] + lora[:, :B_WIDTH])
    softplus = jnp.maximum(neg, 0.0) + jnp.log1p(jnp.exp(-jnp.abs(neg)))
    w = -softplus - 0.5
    a = jax.nn.sigmoid(a0_ref[...] + lora[:, B_WIDTH:])
    g = _dot(jax.nn.sigmoid(lg).astype(BF16), g2_ref[...])

    kkr = k * kkw_ref[...]
    kk = kkr * lax.rsqrt(jnp.maximum(_head_sums(kkr * kkr), 1e-24))

    r_o[...] = r
    k_o[...] = k * (1.0 + (a - 1.0) * kaw_ref[...])
    v_o[...] = v
    kk_o[...] = kk
    b_o[...] = kk * a
    lw_o[...] = -jnp.exp(w)
    g_o[...] = g


def _rwkv_scan_tile(buf, prm, o_ref, out_rows, h_ref, po_ref, q_ref, o2_ref, oacc_ref, n_chunks):
    r_ref, k_ref, v_ref, kk_ref, b_ref, lw_ref, g_ref = buf
    rk_ref, gng_ref, gnb_ref = prm
    c_len = RWKV_CHUNK
    n_blocks = B_WIDTH // LANES

    lane = lax.broadcasted_iota(jnp.int32, (1, LANES), 1)
    head0 = lane < B_HEAD
    rho = lax.broadcasted_iota(jnp.int32, (LANES, LANES), 0)
    sig = lax.broadcasted_iota(jnp.int32, (LANES, LANES), 1)
    same_head = (rho // c_len) == (sig // c_len)
    strict = same_head & ((sig % c_len) < (rho % c_len))
    incl = same_head & ((sig % c_len) <= (rho % c_len))
    eye = rho == sig
    tm = n_chunks * c_len
    tri_r = lax.broadcasted_iota(jnp.int32, (tm, tm), 0)
    tri_c = lax.broadcasted_iota(jnp.int32, (tm, tm), 1)
    cum_mat = ((tri_r // c_len == tri_c // c_len) & (tri_c <= tri_r)).astype(BF16)

    def stack(x):
        return jnp.concatenate([jnp.where(head0, x, 0.0), jnp.where(head0, 0.0, x)], axis=0)

    lw_all = lw_ref[...]
    cum_all = _exact_lhs(cum_mat, lw_all)
    chains = []
    for c in range(n_chunks):
        rows = slice(c * c_len, (c + 1) * c_len)
        cum = cum_all[rows, :]
        total = cum[c_len - 1:c_len, :]
        grow = jnp.exp(cum)
        inv = jnp.exp(-cum)
        to_end = jnp.exp(total - cum)
        decay = jnp.exp(total)
        r_t = r_ref[rows, :] * grow
        kk_t = kk_ref[rows, :] * jnp.exp(cum - lw_all[rows, :])
        b_t = b_ref[rows, :] * inv
        k_t = k_ref[rows, :] * inv
        b_e = b_ref[rows, :] * to_end
        k_e = k_ref[rows, :] * to_end
        v = v_ref[rows, :]
        for blk in range(n_blocks):
            cols = slice(blk * LANES, (blk + 1) * LANES)
            kk_s = stack(kk_t[:, cols]).astype(BF16)
            r_s = stack(r_t[:, cols])
            a_all = _dot_nt(
                jnp.concatenate([kk_s, r_s.astype(BF16)], axis=0),
                jnp.concatenate([stack(b_t[:, cols]), stack(k_t[:, cols])], axis=0).astype(BF16))
            neg_l = -jnp.where(strict, a_all[:LANES, :LANES], 0.0)
            v_lhs = jnp.concatenate(
                [stack(k_e[:, cols]).T,
                 jnp.where(incl, a_all[LANES:, LANES:], 0.0),
                 jnp.where(strict, a_all[:LANES, LANES:], 0.0)], axis=0).astype(BF16)
            w_lhs = jnp.concatenate(
                [stack(b_e[:, cols]).T, jnp.where(incl, a_all[LANES:, :LANES], 0.0)],
                axis=0).astype(BF16)
            chains.append(dict(kk_s=kk_s, r_s=r_s, v_s=stack(v[:, cols]).astype(BF16),
                               v_lhs=v_lhs, w_lhs=w_lhs, decay=decay[:, cols],
                               power=neg_l.astype(BF16),
                               t_inv=jnp.where(eye, 1.0, 0.0) + neg_l))

    for ch in chains:
        ch["power"] = _dot(ch["power"], ch["power"]).astype(BF16)
    for level in range(1, NEUMANN_LEVELS):
        for ch in chains:
            both = _dot(jnp.concatenate([ch["power"], ch["t_inv"].astype(BF16)], axis=0),
                        ch["power"])
            ch["power"] = both[:LANES].astype(BF16)
            ch["t_inv"] = ch["t_inv"] + both[LANES:]
    for ch in chains:
        ch["t_inv"] = (ch["t_inv"] + _dot(ch["t_inv"].astype(BF16), ch["power"])).astype(BF16)

    for ch in chains:
        ch["xv"] = _dot(ch["v_lhs"], ch["v_s"])
    for ch in chains:
        rhs = jnp.concatenate([ch["kk_s"], ch["xv"][2 * LANES:].astype(BF16)], axis=1)
        ch["w12"] = _dot(ch["t_inv"], rhs).astype(BF16)
    for idx, ch in enumerate(chains):
        xw = _dot(ch["w_lhs"], ch["w12"])
        p_mat = jnp.where(eye, ch["decay"], 0.0) - xw[:LANES, :LANES]
        o1 = ch["r_s"] - xw[LANES:, :LANES]
        po_ref[idx] = jnp.concatenate([p_mat, o1], axis=0).astype(BF16)
        q_ref[idx] = ch["xv"][:LANES] - xw[:LANES, LANES:]
        o2_ref[idx] = ch["xv"][LANES:2 * LANES] - xw[LANES:, LANES:]

    states = [h_ref[blk] for blk in range(n_blocks)]
    for c in range(n_chunks):
        rows = slice(c * c_len, (c + 1) * c_len)
        pos = [_dot(po_ref[c * n_blocks + blk], states[blk].astype(BF16))
               for blk in range(n_blocks)]
        for blk in range(n_blocks):
            chain = c * n_blocks + blk
            states[blk] = pos[blk][:LANES] + q_ref[chain]
            o_st = pos[blk][LANES:] + o2_ref[chain]
            oacc_ref[rows, blk * LANES:(blk + 1) * LANES] = o_st[:c_len] + o_st[c_len:]
    for blk in range(n_blocks):
        h_ref[blk] = states[blk]

    o = oacc_ref[...]
    mean = _head_sums(o) * (1.0 / B_HEAD)
    cen = o - mean
    var = _head_sums(cen * cen) * (1.0 / B_HEAD)
    normed = cen * lax.rsqrt(var + GN_EPS) * gng_ref[...] + gnb_ref[...]
    bonus = _head_sums(r_ref[...] * k_ref[...] * rk_ref[...]) * v_ref[...]
    o_ref[out_rows, :] = ((normed + bonus) * g_ref[...]).astype(o_ref.dtype)


def _rwkv_kernel(*refs, n_chunks):
    p_ref, pn_ref = refs[0:2]
    prep_prm = refs[2:9]
    scan_prm = refs[9:12]
    o_ref = refs[12]
    buf_a = refs[13:20]
    buf_b = refs[20:27]
    h_ref, po_ref, q_ref, o2_ref, oacc_ref = refs[27:32]
    tm = n_chunks * RWKV_CHUNK
    scan_scratch = (h_ref, po_ref, q_ref, o2_ref, oacc_ref, n_chunks)

    @pl.when(pl.program_id(0) == 0)
    def _():
        h_ref[...] = jnp.zeros_like(h_ref)
        _rwkv_prep_tile(p_ref[0:tm, :], jnp.zeros((1, SHIFT_W), F32), prep_prm, buf_a)

    _rwkv_prep_tile(p_ref[tm:2 * tm, :], p_ref[tm - 1:tm, :], prep_prm, buf_b)
    _rwkv_scan_tile(buf_a, scan_prm, o_ref, slice(0, tm), *scan_scratch)
    _rwkv_prep_tile(pn_ref[...], p_ref[2 * tm - 1:2 * tm, :], prep_prm, buf_a)
    _rwkv_scan_tile(buf_b, scan_prm, o_ref, slice(tm, 2 * tm), *scan_scratch)


def rwkv_mixer(p_b, mu, w0, w2, a0, a2, g2, k_k, k_a, r_k, gn_g, gn_b):
    t_len = p_b.shape[0]
    n_chunks = RWKV_CHUNKS_PER_STEP
    tm = n_chunks * RWKV_CHUNK
    n_tiles = t_len // tm
    n_blocks = B_WIDTH // LANES
    n_chains = n_chunks * n_blocks
    zeros = jnp.zeros((LORA_W, B_WIDTH), F32)
    wa = jnp.concatenate([jnp.concatenate([w2, zeros], 1),
                          jnp.concatenate([zeros, a2], 1)], 0).astype(BF16)
    row = lambda z: z.reshape(1, -1)
    full = lambda shape: pl.BlockSpec(shape, lambda i: (0,) * len(shape))
    vec = full((1, B_WIDTH))
    tile_buf = [pltpu.VMEM((tm, B_WIDTH), F32)] * 7
    return pl.pallas_call(
        functools.partial(_rwkv_kernel, n_chunks=n_chunks),
        grid=(n_tiles // 2,),
        in_specs=[pl.BlockSpec((2 * tm, SHIFT_W), lambda i: (i, 0)),
                  pl.BlockSpec((tm, SHIFT_W), lambda i: (jnp.minimum(2 * i + 2, n_tiles - 1), 0)),
                  full((1, SHIFT_W)), vec, full((LORA_W + LORA_A, 2 * B_WIDTH)), vec,
                  full((LORA_G, B_WIDTH)), vec, vec,
                  vec, vec, vec],
        out_specs=pl.BlockSpec((2 * tm, B_WIDTH), lambda i: (i, 0)),
        out_shape=jax.ShapeDtypeStruct((t_len, B_WIDTH), BF16),
        scratch_shapes=tile_buf + tile_buf
                       + [pltpu.VMEM((n_blocks, LANES, LANES), F32),
                          pltpu.VMEM((n_chains, 2 * LANES, LANES), BF16),
                          pltpu.VMEM((n_chains, LANES, LANES), F32),
                          pltpu.VMEM((n_chains, LANES, LANES), F32),
                          pltpu.VMEM((tm, B_WIDTH), F32)],
        compiler_params=_params(1),
        name="rwkv_mixer",
    )(p_b, p_b, row(mu), row(w0), wa, row(a0), g2.astype(BF16), row(k_k), row(k_a),
      row(r_k), row(gn_g), row(gn_b))


def _proj_residual_kernel(x_ref, ya_ref, yb_ref, w_ref, o_ref):
    o_ref[...] = (x_ref[...] + _dot(ya_ref[...], w_ref[:A_WIDTH, :])
                  + _dot(yb_ref[...], w_ref[A_WIDTH:, :]))


def proj_residual(x, y_a, y_b, w_out):
    t_len, d = x.shape
    tm = min(ROW_TILE, t_len)
    half = pl.BlockSpec((tm, A_WIDTH), lambda i: (i, 0))
    return pl.pallas_call(
        _proj_residual_kernel,
        grid=(t_len // tm,),
        in_specs=[pl.BlockSpec((tm, d), lambda i: (i, 0)), half, half,
                  pl.BlockSpec((d, d), lambda i: (0, 0))],
        out_specs=pl.BlockSpec((tm, d), lambda i: (i, 0)),
        out_shape=jax.ShapeDtypeStruct((t_len, d), F32),
        compiler_params=_params(1),
        name="proj_residual",
    )(x, y_a, y_b, w_out.astype(BF16))


def _attn_kernel(q_ref, kp_ref, kc_ref, vp_ref, vc_ref, o_ref, lse_ref, o_tmp, lse_tmp, *, dil,
                 n_qb):
    blk = ATTN_BLOCK
    half = blk // 2
    span = blk + half
    assert half == C_HEAD
    i = pl.program_id(0)
    res = pl.program_id(1)
    lane = lax.broadcasted_iota(jnp.int32, (1, LANES), 1)
    head0 = lane < C_HEAD
    key = lax.broadcasted_iota(jnp.int32, (span, LANES), 0)
    qry = lax.broadcasted_iota(jnp.int32, (span, LANES), 1) % half
    dist = qry + blk - key
    in_window = (dist >= 0) & (dist <= blk)
    first_valid = [in_window & ((key >= blk - sub * half) | (i > 0)) for sub in range(2)]
    swap_heads = lambda z: pltpu.roll(z, C_HEAD, 1)
    n_pairs = C_HEADS // HEADS_PER_BLOCK
    units = [(pair, sub) for pair in range(n_pairs) for sub in range(2)]

    for qb in range(n_qb):
        q_rows = slice(qb * blk, (qb + 1) * blk)
        prev_k, prev_v = (kp_ref, vp_ref) if qb == 0 else (kc_ref.at[(qb - 1) * blk:qb * blk],
                                                           vc_ref.at[(qb - 1) * blk:qb * blk])
        cur_k, cur_v = kc_ref.at[q_rows], vc_ref.at[q_rows]
        valid = first_valid if qb == 0 else [in_window, in_window]
        scores = []
        for pair, sub in units:
            cols = slice(pair * LANES, (pair + 1) * LANES)
            qsub = q_ref[qb * blk + sub * half:qb * blk + (sub + 1) * half, cols]
            qsub = qsub * (C_HEAD ** -0.5)
            zero = jnp.zeros_like(qsub)
            qs = jnp.concatenate([jnp.where(head0, qsub, zero), jnp.where(head0, zero, qsub)], axis=0)
            ks = jnp.concatenate([prev_k[sub * half:, cols], cur_k[:(sub + 1) * half, cols]], axis=0)
            scores.append(jnp.where(valid[sub], _dot_nt(ks, qs), NEG_INF))
        probs, lses = [], []
        for st in scores:
            mx = jnp.max(st, axis=0, keepdims=True)
            e = jnp.exp(st - mx)
            den = jnp.sum(e, axis=0, keepdims=True)
            probs.append((e * (1.0 / den)).astype(BF16))
            lses.append(mx + jnp.log(den))
        for (pair, sub), prob in zip(units, probs):
            cols = slice(pair * LANES, (pair + 1) * LANES)
            vs = jnp.concatenate([prev_v[sub * half:, cols], cur_v[:(sub + 1) * half, cols]], axis=0)
            pv = lax.dot_general(prob, vs, (((0,), (0,)), ((), ())),
                                 preferred_element_type=F32)
            o_tmp[qb * blk + sub * half:qb * blk + (sub + 1) * half, cols] = (
                jnp.where(head0, pv[:half], pv[half:]))
        lse_rows = []
        for pair in range(n_pairs):
            l0, l1 = lses[2 * pair], lses[2 * pair + 1]
            lse_rows.append(jnp.where(head0, l0, swap_heads(l1)))
            lse_rows.append(jnp.where(head0, swap_heads(l0), l1))
        lse_t = jnp.concatenate(lse_rows + [jnp.zeros((LANES - C_HEADS, LANES), F32)], axis=0)
        lse_tmp[q_rows, :] = lse_t.T

    for static_res in range(dil):
        @pl.when(res == static_res)
        def _():
            rows = pl.ds(static_res, n_qb * blk, stride=dil)
            for c in range(o_ref.shape[0]):
                o_ref[c, rows, :] = o_tmp[:, c * LANES:(c + 1) * LANES]
            lse_ref[rows, :] = lse_tmp[...]


def dilated_attention_pattern(view, dil):
    sub = view.shape[0]
    t_len = sub * dil
    d = D_MODEL
    blk = ATTN_BLOCK
    n_qb = max(1, min(ATTN_MAX_BLOCKS_PER_STEP, ATTN_OUT_BLOCK_ROWS // (blk * dil), sub // blk))
    rows = n_qb * blk
    cur = lambda part: pl.BlockSpec((rows, d), lambda i, r: (i, 3 * r + part))
    prv = lambda part: pl.BlockSpec((blk, d),
                                    lambda i, r: (jnp.maximum(n_qb * i - 1, 0), 3 * r + part))
    return pl.pallas_call(
        functools.partial(_attn_kernel, dil=dil, n_qb=n_qb),
        grid=(sub // rows, dil),
        in_specs=[cur(0), prv(1), cur(1), prv(2), cur(2)],
        out_specs=[pl.BlockSpec((d // LANES, rows * dil, LANES), lambda i, r: (0, i, 0)),
                   pl.BlockSpec((rows * dil, LANES), lambda i, r: (i, 0))],
        out_shape=[jax.ShapeDtypeStruct((d // LANES, t_len, LANES), F32),
                   jax.ShapeDtypeStruct((t_len, LANES), F32)],
        scratch_shapes=[pltpu.VMEM((rows, d), F32), pltpu.VMEM((rows, LANES), F32)],
        compiler_params=_params(2),
        name=f"dilated_attn_d{dil}",
    )(view, view, view, view, view)


def _attn_combine_proj_kernel(x_ref, o1_ref, o2_ref, o3_ref, l1_ref, l2_ref, l3_ref, ex_ref, w_ref,
                              o_ref):
    lses = [l1_ref[...], l2_ref[...], l3_ref[...]]
    mx = jnp.maximum(jnp.maximum(lses[0], lses[1]), lses[2])
    es = [jnp.exp(l - mx) for l in lses]
    den = es[0] + es[1] + es[2]
    y = jnp.zeros(o_ref.shape, F32)
    for e, o in zip(es, (o1_ref, o2_ref, o3_ref)):
        o_full = jnp.concatenate([o[c] for c in range(o.shape[0])], axis=-1)
        y = y + _exact_rhs(e / den, ex_ref[...]) * o_full
    o_ref[...] = x_ref[...] + _dot(y.astype(BF16), w_ref[...])


def attn_combine_proj(x, outs, lses, w_out):
    t_len, d = x.shape
    tm = min(ROW_TILE, t_len)
    head_of_lane = jnp.arange(d) // C_HEAD
    expand = (jnp.arange(LANES)[:, None] == head_of_lane[None, :]).astype(BF16)
    wide = pl.BlockSpec((tm, d), lambda i: (i, 0))
    chunked = pl.BlockSpec((d // LANES, tm, LANES), lambda i: (0, i, 0))
    narrow = pl.BlockSpec((tm, LANES), lambda i: (i, 0))
    return pl.pallas_call(
        _attn_combine_proj_kernel,
        grid=(t_len // tm,),
        in_specs=[wide, chunked, chunked, chunked, narrow, narrow, narrow,
                  pl.BlockSpec((LANES, d), lambda i: (0, 0)),
                  pl.BlockSpec((d, d), lambda i: (0, 0))],
        out_specs=wide,
        out_shape=jax.ShapeDtypeStruct((t_len, d), F32),
        compiler_params=_params(1),
        name="attn_combine_proj",
    )(x, *outs, *lses, expand, w_out.astype(BF16))


def _ffn_kernel(x_ref, g_ref, wu_ref, cw_ref, cb_ref, wd_ref, fn_ref,
                o_ref, ha_ref, hb_ref, zg0_ref, zg1_ref, zv0_ref, zv1_ref, carry_g_ref,
                carry_v_ref, *, final_norm):
    tm = x_ref.shape[0] // 2
    n_j, tf = wd_ref.shape[0], wd_ref.shape[1]
    halo = SUBLANES
    zg_refs = (zg0_ref, zg1_ref)
    zv_refs = (zv0_ref, zv1_ref)
    assert n_j % 2 == 1

    def tile_cols(j, base):
        if isinstance(j, int):
            return pl.ds(base + j * tf, tf)
        return pl.ds(pl.multiple_of(base + j * tf, tf), tf)

    @pl.when(pl.program_id(0) == 0)
    def _():
        carry_g_ref[...] = jnp.zeros_like(carry_g_ref)
        carry_v_ref[...] = jnp.zeros_like(carry_v_ref)

    def start(rows, h_ref):
        x = x_ref[rows, :]
        h_ref[...] = _rmsnorm(x, g_ref[...]).astype(BF16)
        o_ref[rows, :] = x

    def up(j, slot, h_ref):
        h = h_ref[...]
        for base, z_ref, carry_ref in ((0, zg_refs[slot], carry_g_ref),
                                       (D_FF, zv_refs[slot], carry_v_ref)):
            z = _dot(h, wu_ref[:, tile_cols(j, base)])
            z_ref[0:halo, :] = carry_ref[j]
            z_ref[halo:halo + tm, :] = z
            carry_ref[j] = z[tm - halo:tm, :]

    def finish(j, slot, rows):
        def conv(z_ref, base):
            z = z_ref[halo:halo + tm, :]
            z1 = z_ref[halo - 1:halo - 1 + tm, :]
            z2 = z_ref[halo - 2:halo - 2 + tm, :]
            cw = cw_ref[:, tile_cols(j, base)]
            return (cw[0:1, :] * z2 + cw[1:2, :] * z1 + cw[2:3, :] * z
                    + cb_ref[:, tile_cols(j, base)])

        gate = conv(zg_refs[slot], 0)
        val = conv(zv_refs[slot], D_FF)
        act = (gate * jax.nn.sigmoid(gate) * val).astype(BF16)
        o_ref[rows, :] += _dot(act, wd_ref[j])

    def middle(rows, h_ref, first_slot):
        def body(pair, carry):
            j = 2 * pair
            up(j + 1, 1 - first_slot, h_ref)
            finish(j, first_slot, rows)
            up(j + 2, first_slot, h_ref)
            finish(j + 1, 1 - first_slot, rows)
            return carry
        lax.fori_loop(0, (n_j - 1) // 2, body, 0)

    rows_a, rows_b = slice(0, tm), slice(tm, 2 * tm)
    start(rows_a, ha_ref)
    up(0, 0, ha_ref)
    middle(rows_a, ha_ref, 0)
    start(rows_b, hb_ref)
    up(0, 1, hb_ref)
    finish(n_j - 1, 0, rows_a)
    middle(rows_b, hb_ref, 1)
    finish(n_j - 1, 1, rows_b)

    if final_norm:
        o_ref[...] = _rmsnorm(o_ref[...], fn_ref[...])


def conv_glu_ffn(x, norm_g, w_up, conv_w, conv_b, w_down, final_g, final_norm):
    t_len, d = x.shape
    tm = min(FFN_ROW_TILE, t_len // 2)
    tf = FFN_COL_TILE
    n_j = D_FF // tf
    full = lambda shape: pl.BlockSpec(shape, lambda i: (0,) * len(shape),
                                      pipeline_mode=pl.Buffered(1))
    return pl.pallas_call(
        functools.partial(_ffn_kernel, final_norm=final_norm),
        grid=(t_len // (2 * tm),),
        in_specs=[pl.BlockSpec((2 * tm, d), lambda i: (i, 0)),
                  full((1, d)),
                  full((d, 2 * D_FF)),
                  full((3, 2 * D_FF)),
                  full((1, 2 * D_FF)),
                  full((n_j, tf, d)),
                  full((1, d))],
        out_specs=pl.BlockSpec((2 * tm, d), lambda i: (i, 0)),
        out_shape=jax.ShapeDtypeStruct((t_len, d), F32),
        scratch_shapes=[pltpu.VMEM((tm, d), BF16)] * 2
                       + [pltpu.VMEM((tm + SUBLANES, tf), F32)] * 4
                       + [pltpu.VMEM((n_j, SUBLANES, tf), F32),
                        pltpu.VMEM((n_j, SUBLANES, tf), F32)],
        compiler_params=_params(1),
        name="conv_glu_ffn",
    )(x, norm_g.reshape(1, d), w_up.astype(BF16), conv_w, conv_b.reshape(1, 2 * D_FF),
      w_down.astype(BF16).reshape(n_j, tf, d), final_g.reshape(1, d))


def even_mixer(x, norm, w_in, ln_g, ln_b, w_s, b_s, mu, w0, w2, a0, a2, g2, k_k, k_a, r_k, gn_g,
               gn_b, w_out):
    y_a, p_b = even_in(x, norm, w_in, ln_g, ln_b, w_s, b_s)
    y_b = rwkv_mixer(p_b, mu, w0, w2, a0, a2, g2, k_k, k_a, r_k, gn_g, gn_b)
    return proj_residual(x, y_a, y_b, w_out)


def odd_mixer(x, norm, w_qkv, w_out):
    dilations = tuple(dil for _, dil in C_PATTERNS)
    views = norm_matmul_views(x, norm, w_qkv.astype(BF16), dilations)
    outs, lses = [], []
    for view, dil in zip(views, dilations):
        o, l = dilated_attention_pattern(view, dil)
        outs.append(o)
        lses.append(l)
    return attn_combine_proj(x, outs, lses, w_out)


def kernel(x, ev_norm, ev_w_in, ev_ln_g, ev_ln_b, ev_w_s, ev_b_s, ev_mu, ev_w0, ev_w2, ev_a0, ev_a2, ev_g2, ev_k_k, ev_k_a, ev_r_k, ev_gn_g, ev_gn_b, ev_w_out, od_norm, od_w_qkv, od_w_out, ff_norm, ff_w_up, ff_conv_w, ff_conv_b, ff_w_down, final_norm):
    bsz, t_len, d = x.shape
    depth = ff_norm.shape[0]
    outs = []
    for bi in range(bsz):
        xb = x[bi]
        for layer in range(depth):
            j = layer // 2
            if layer % 2 == 0:
                xb = even_mixer(xb, ev_norm[j], ev_w_in[j], ev_ln_g[j], ev_ln_b[j], ev_w_s[j],
                                ev_b_s[j], ev_mu[j], ev_w0[j], ev_w2[j], ev_a0[j], ev_a2[j],
                                ev_g2[j], ev_k_k[j], ev_k_a[j], ev_r_k[j], ev_gn_g[j], ev_gn_b[j],
                                ev_w_out[j])
            else:
                xb = odd_mixer(xb, od_norm[j], od_w_qkv[j], od_w_out[j])
            xb = conv_glu_ffn(xb, ff_norm[layer], ff_w_up[layer], ff_conv_w[layer],
                              ff_conv_b[layer], ff_w_down[layer], final_norm,
                              final_norm=(layer == depth - 1))
        outs.append(xb)
    return jnp.stack(outs, axis=0)
```

```python
import functools

import jax
import jax.numpy as jnp
from jax import lax
from jax.experimental import pallas as pl
from jax.experimental.pallas import tpu as pltpu

F32 = jnp.float32
BF16 = jnp.bfloat16

D_MODEL = 1024
A_WIDTH = 512
A_GROUP = 128
A_GROUPS = 4
A_CHUNK = 128
LN_EPS = 1e-5
B_WIDTH = 512
B_HEAD = 64
LORA_W = 64
LORA_A = 64
LORA_G = 128
GN_EPS = 64e-5
SHIFT_W = 3 * B_WIDTH + LORA_W + LORA_A + LORA_G
C_HEAD = 64
C_HEADS = 16
C_PATTERNS = ((128, 1), (512, 4), (2048, 16))
NEG_INF = -1e30
D_FF = 2816
RMS_EPS = 1e-6

LANES = 128
SUBLANES = 8
HEADS_PER_BLOCK = LANES // B_HEAD
VMEM_LIMIT_BYTES = 52 * 1024 * 1024

ROW_TILE = 512
QKV_ROW_TILE = 256
FFN_ROW_TILE = 512
FFN_COL_TILE = 256
RWKV_CHUNK = 64
RWKV_CHUNKS_PER_STEP = 4
ATTN_BLOCK = 128
ATTN_MAX_BLOCKS_PER_STEP = 4
NEUMANN_LEVELS = 5


def _params(n_axes):
    return pltpu.CompilerParams(dimension_semantics=("arbitrary",) * n_axes,
                                vmem_limit_bytes=VMEM_LIMIT_BYTES)


def _rmsnorm(x, g):
    return x * lax.rsqrt(jnp.mean(x * x, axis=-1, keepdims=True) + RMS_EPS) * g


def _dot(a, b):
    return jnp.dot(a, b, preferred_element_type=F32)


def _dot_nt(a, b):
    return lax.dot_general(a, b, (((1,), (1,)), ((), ())), preferred_element_type=F32)


def _split3(x):
    hi = x.astype(BF16)
    r1 = x - hi.astype(F32)
    mid = r1.astype(BF16)
    lo = (r1 - mid.astype(F32)).astype(BF16)
    return hi, mid, lo


def _exact_rhs(x, m_bf16):
    hi, mid, lo = _split3(x)
    return _dot(hi, m_bf16) + (_dot(mid, m_bf16) + _dot(lo, m_bf16))


def _exact_lhs(m_bf16, x):
    hi, mid, lo = _split3(x)
    return _dot(m_bf16, hi) + (_dot(m_bf16, mid) + _dot(m_bf16, lo))


def _norm_matmul_views_kernel(x_ref, g_ref, w_ref, *o_refs, dilations):
    tm = x_ref.shape[0]
    n = w_ref.shape[1]
    h = _rmsnorm(x_ref[...], g_ref[...]).astype(BF16)
    res = _dot(h, w_ref[...]).astype(BF16)
    out_row = lax.broadcasted_iota(jnp.int32, (tm, tm), 0)
    in_row = lax.broadcasted_iota(jnp.int32, (tm, tm), 1)
    for dil, o_ref in zip(dilations, o_refs):
        if dil == 1:
            o_ref[...] = res
            continue
        per = tm // dil
        assert per & (per - 1) == 0
        shift = per.bit_length() - 1
        src_row = (out_row & (per - 1)) * dil + (out_row >> shift)
        grouped = _dot((in_row == src_row).astype(BF16), res).astype(BF16)
        for r in range(dil):
            o_ref[:, r * n:(r + 1) * n] = grouped[r * per:(r + 1) * per, :]


def norm_matmul_views(x, g, w, dilations):
    t_len, d = x.shape
    n = w.shape[1]
    tm = min(QKV_ROW_TILE, t_len)
    return pl.pallas_call(
        functools.partial(_norm_matmul_views_kernel, dilations=dilations),
        grid=(t_len // tm,),
        in_specs=[pl.BlockSpec((tm, d), lambda i: (i, 0)),
                  pl.BlockSpec((1, d), lambda i: (0, 0)),
                  pl.BlockSpec((d, n), lambda i: (0, 0))],
        out_specs=[pl.BlockSpec((tm // dil, dil * n), lambda i: (i, 0)) for dil in dilations],
        out_shape=[jax.ShapeDtypeStruct((t_len // dil, dil * n), BF16) for dil in dilations],
        compiler_params=_params(1),
        name="norm_matmul_views",
    )(x, g.reshape(1, d), w)


def _even_in_kernel(x_ref, g_ref, w_ref, lng_ref, lnb_ref, ws_ref, bs_ref, ya_ref, pb_ref):
    tm = x_ref.shape[0]
    h = _rmsnorm(x_ref[...], g_ref[...]).astype(BF16)
    p = _dot(h, w_ref[...])
    pb_ref[...] = p[:, 2 * A_WIDTH:]

    row = lax.broadcasted_iota(jnp.int32, (A_CHUNK, A_CHUNK), 0)
    col = lax.broadcasted_iota(jnp.int32, (A_CHUNK, A_CHUNK), 1)
    causal = col <= row
    w_masked = [jnp.where(causal, ws_ref[g], 0.0).astype(BF16) for g in range(A_GROUPS)]
    for c in range(tm // A_CHUNK):
        rows = slice(c * A_CHUNK, (c + 1) * A_CHUNK)
        u = p[rows, :A_WIDTH]
        v = p[rows, A_WIDTH:2 * A_WIDTH]
        mean = jnp.mean(v, axis=-1, keepdims=True)
        cen = v - mean
        var = jnp.mean(cen * cen, axis=-1, keepdims=True)
        vn = (cen * lax.rsqrt(var + LN_EPS) * lng_ref[...] + lnb_ref[...]).astype(BF16)
        for g in range(A_GROUPS):
            cols = slice(g * A_GROUP, (g + 1) * A_GROUP)
            mixed = _dot(w_masked[g], vn[:, cols]) + bs_ref[:, g:g + 1]
            ya_ref[rows, cols] = (u[:, cols] * mixed).astype(ya_ref.dtype)


def even_in(x, g, w_in, ln_g, ln_b, w_s, b_s):
    t_len, d = x.shape
    n = w_in.shape[1]
    tm = min(ROW_TILE, t_len)
    full = lambda shape: pl.BlockSpec(shape, lambda i: (0,) * len(shape))
    return pl.pallas_call(
        _even_in_kernel,
        grid=(t_len // tm,),
        in_specs=[pl.BlockSpec((tm, d), lambda i: (i, 0)),
                  full((1, d)), full((d, n)), full((1, A_WIDTH)), full((1, A_WIDTH)),
                  full((A_GROUPS, A_CHUNK, A_CHUNK)), full((A_CHUNK, A_GROUPS))],
        out_specs=[pl.BlockSpec((tm, A_WIDTH), lambda i: (i, 0)),
                   pl.BlockSpec((tm, SHIFT_W), lambda i: (i, 0))],
        out_shape=[jax.ShapeDtypeStruct((t_len, A_WIDTH), BF16),
                   jax.ShapeDtypeStruct((t_len, SHIFT_W), F32)],
        compiler_params=_params(1),
        name="even_in",
    )(x, g.reshape(1, d), w_in.astype(BF16), ln_g.reshape(1, A_WIDTH), ln_b.reshape(1, A_WIDTH),
      w_s, b_s.T)


def _head_sums(x):
    lane = lax.broadcasted_iota(jnp.int32, (1, LANES), 1)
    head0 = lane < B_HEAD
    outs = []
    for blk in range(x.shape[1] // LANES):
        xb = x[:, blk * LANES:(blk + 1) * LANES]
        s0 = jnp.sum(jnp.where(head0, xb, 0.0), axis=-1, keepdims=True)
        s1 = jnp.sum(jnp.where(head0, 0.0, xb), axis=-1, keepdims=True)
        outs.append(jnp.where(head0, s0, s1))
    return jnp.concatenate(outs, axis=-1)


def _rwkv_prep_tile(p, prev_row, prm, buf):
    mu_ref, w0_ref, wa_ref, a0_ref, g2_ref, kkw_ref, kaw_ref = prm
    r_o, k_o, v_o, kk_o, b_o, lw_o, g_o = buf
    tm = p.shape[0]
    rowid = lax.broadcasted_iota(jnp.int32, (tm, 1), 0)
    prev = jnp.where(rowid == 0, prev_row, pltpu.roll(p, 1, 0))
    ps = p + (prev - p) * mu_ref[...]

    r = ps[:, :B_WIDTH]
    k = ps[:, B_WIDTH:2 * B_WIDTH]
    v = ps[:, 2 * B_WIDTH:3 * B_WIDTH]
    lwa = ps[:, 3 * B_WIDTH:3 * B_WIDTH + LORA_W + LORA_A]
    lg = ps[:, 3 * B_WIDTH + LORA_W + LORA_A:]

    lane = lax.broadcasted_iota(jnp.int32, (1, LORA_W + LORA_A), 1)
    lora_in = jnp.where(lane < LORA_W, jnp.tanh(lwa), lwa).astype(BF16)
    lora = _dot(lora_in, wa_ref[...])
    neg = -(w0_ref[...] + lora[:, :B_WIDTH])
    softplus = jnp.maximum(neg, 0.0) + jnp.log1p(jnp.exp(-jnp.abs(neg)))
    w = -softplus - 0.5
    a = jax.nn.sigmoid(a0_ref[...] + lora[:, B_WIDTH:])
    g = _dot(jax.nn.sigmoid(lg).astype(BF16), g2_ref[...])

    kkr = k * kkw_ref[...]
    kk = kkr * lax.rsqrt(jnp.maximum(_head_sums(kkr * kkr), 1e-24))

    r_o[...] = r
    k_o[...] = k * (1.0 + (a - 1.0) * kaw_ref[...])
    v_o[...] = v
    kk_o[...] = kk
    b_o[...] = kk * a
    lw_o[...] = -jnp.exp(w)
    g_o[...] = g


def _rwkv_scan_tile(buf, prm, o_ref, out_rows, h_ref, po_ref, q_ref, o2_ref, oacc_ref, n_chunks):
    r_ref, k_ref, v_ref, kk_ref, b_ref, lw_ref, g_ref = buf
    rk_ref, gng_ref, gnb_ref = prm
    c_len = RWKV_CHUNK
    n_blocks = B_WIDTH // LANES

    lane = lax.broadcasted_iota(jnp.int32, (1, LANES), 1)
    head0 = lane < B_HEAD
    rho = lax.broadcasted_iota(jnp.int32, (LANES, LANES), 0)
    sig = lax.broadcasted_iota(jnp.int32, (LANES, LANES), 1)
    same_head = (rho // c_len) == (sig // c_len)
    strict = same_head & ((sig % c_len) < (rho % c_len))
    incl = same_head & ((sig % c_len) <= (rho % c_len))
    eye = rho == sig
    tm = n_chunks * c_len
    tri_r = lax.broadcasted_iota(jnp.int32, (tm, tm), 0)
    tri_c = lax.broadcasted_iota(jnp.int32, (tm, tm), 1)
    cum_mat = ((tri_r // c_len == tri_c // c_len) & (tri_c <= tri_r)).astype(BF16)

    def stack(x):
        return jnp.concatenate([jnp.where(head0, x, 0.0), jnp.where(head0, 0.0, x)], axis=0)

    lw_all = lw_ref[...]
    cum_all = _exact_lhs(cum_mat, lw_all)
    chains = []
    for c in range(n_chunks):
        rows = slice(c * c_len, (c + 1) * c_len)
        cum = cum_all[rows, :]
        total = cum[c_len - 1:c_len, :]
        grow = jnp.exp(cum)
        inv = jnp.exp(-cum)
        to_end = jnp.exp(total - cum)
        decay = jnp.exp(total)
        r_t = r_ref[rows, :] * grow
        kk_t = kk_ref[rows, :] * jnp.exp(cum - lw_all[rows, :])
        b_t = b_ref[rows, :] * inv
        k_t = k_ref[rows, :] * inv
        b_e = b_ref[rows, :] * to_end
        k_e = k_ref[rows, :] * to_end
        v = v_ref[rows, :]
        for blk in range(n_blocks):
            cols = slice(blk * LANES, (blk + 1) * LANES)
            kk_s = stack(kk_t[:, cols]).astype(BF16)
            r_s = stack(r_t[:, cols])
            a_all = _dot_nt(
                jnp.concatenate([kk_s, r_s.astype(BF16)], axis=0),
                jnp.concatenate([stack(b_t[:, cols]), stack(k_t[:, cols])], axis=0).astype(BF16))
            neg_l = -jnp.where(strict, a_all[:LANES, :LANES], 0.0)
            v_lhs = jnp.concatenate(
                [stack(k_e[:, cols]).T,
                 jnp.where(incl, a_all[LANES:, LANES:], 0.0),
                 jnp.where(strict, a_all[:LANES, LANES:], 0.0)], axis=0).astype(BF16)
            w_lhs = jnp.concatenate(
                [stack(b_e[:, cols]).T, jnp.where(incl, a_all[LANES:, :LANES], 0.0)],
                axis=0).astype(BF16)
            chains.append(dict(kk_s=kk_s, r_s=r_s, v_s=stack(v[:, cols]).astype(BF16),
                               v_lhs=v_lhs, w_lhs=w_lhs, decay=decay[:, cols],
                               power=neg_l.astype(BF16),
                               t_inv=jnp.where(eye, 1.0, 0.0) + neg_l))

    for ch in chains:
        ch["power"] = _dot(ch["power"], ch["power"]).astype(BF16)
    for level in range(1, NEUMANN_LEVELS):
        for ch in chains:
            both = _dot(jnp.concatenate([ch["power"], ch["t_inv"].astype(BF16)], axis=0),
                        ch["power"])
            ch["power"] = both[:LANES].astype(BF16)
            ch["t_inv"] = ch["t_inv"] + both[LANES:]
    for ch in chains:
        ch["t_inv"] = (ch["t_inv"] + _dot(ch["t_inv"].astype(BF16), ch["power"])).astype(BF16)

    for ch in chains:
        ch["xv"] = _dot(ch["v_lhs"], ch["v_s"])
    for ch in chains:
        rhs = jnp.concatenate([ch["kk_s"], ch["xv"][2 * LANES:].astype(BF16)], axis=1)
        ch["w12"] = _dot(ch["t_inv"], rhs).astype(BF16)
    for idx, ch in enumerate(chains):
        xw = _dot(ch["w_lhs"], ch["w12"])
        p_mat = jnp.where(eye, ch["decay"], 0.0) - xw[:LANES, :LANES]
        o1 = ch["r_s"] - xw[LANES:, :LANES]
        po_ref[idx] = jnp.concatenate([p_mat, o1], axis=0).astype(BF16)
        q_ref[idx] = ch["xv"][:LANES] - xw[:LANES, LANES:]
        o2_ref[idx] = ch["xv"][LANES:2 * LANES] - xw[LANES:, LANES:]

    states = [h_ref[blk] for blk in range(n_blocks)]
    for c in range(n_chunks):
        rows = slice(c * c_len, (c + 1) * c_len)
        pos = [_dot(po_ref[c * n_blocks + blk], states[blk].astype(BF16))
               for blk in range(n_blocks)]
        for blk in range(n_blocks):
            chain = c * n_blocks + blk
            states[blk] = pos[blk][:LANES] + q_ref[chain]
            o_st = pos[blk][LANES:] + o2_ref[chain]
            oacc_ref[rows, blk * LANES:(blk + 1) * LANES] = o_st[:c_len] + o_st[c_len:]
    for blk in range(n_blocks):
        h_ref[blk] = states[blk]

    o = oacc_ref[...]
    mean = _head_sums(o) * (1.0 / B_HEAD)
    cen = o - mean
    var = _head_sums(cen * cen) * (1.0 / B_HEAD)
    normed = cen * lax.rsqrt(var + GN_EPS) * gng_ref[...] + gnb_ref[...]
    bonus = _head_sums(r_ref[...] * k_ref[...] * rk_ref[...]) * v_ref[...]
    o_ref[out_rows, :] = ((normed + bonus) * g_ref[...]).astype(o_ref.dtype)


def _rwkv_kernel(*refs, n_chunks):
    p_ref, pn_ref = refs[0:2]
    prep_prm = refs[2:9]
    scan_prm = refs[9:12]
    o_ref = refs[12]
    buf_a = refs[13:20]
    buf_b = refs[20:27]
    h_ref, po_ref, q_ref, o2_ref, oacc_ref = refs[27:32]
    tm = n_chunks * RWKV_CHUNK
    scan_scratch = (h_ref, po_ref, q_ref, o2_ref, oacc_ref, n_chunks)

    @pl.when(pl.program_id(0) == 0)
    def _():
        h_ref[...] = jnp.zeros_like(h_ref)
        _rwkv_prep_tile(p_ref[0:tm, :], jnp.zeros((1, SHIFT_W), F32), prep_prm, buf_a)

    _rwkv_prep_tile(p_ref[tm:2 * tm, :], p_ref[tm - 1:tm, :], prep_prm, buf_b)
    _rwkv_scan_tile(buf_a, scan_prm, o_ref, slice(0, tm), *scan_scratch)
    _rwkv_prep_tile(pn_ref[...], p_ref[2 * tm - 1:2 * tm, :], prep_prm, buf_a)
    _rwkv_scan_tile(buf_b, scan_prm, o_ref, slice(tm, 2 * tm), *scan_scratch)


def rwkv_mixer(p_b, mu, w0, w2, a0, a2, g2, k_k, k_a, r_k, gn_g, gn_b):
    t_len = p_b.shape[0]
    n_chunks = RWKV_CHUNKS_PER_STEP
    tm = n_chunks * RWKV_CHUNK
    n_tiles = t_len // tm
    n_blocks = B_WIDTH // LANES
    n_chains = n_chunks * n_blocks
    zeros = jnp.zeros((LORA_W, B_WIDTH), F32)
    wa = jnp.concatenate([jnp.concatenate([w2, zeros], 1),
                          jnp.concatenate([zeros, a2], 1)], 0).astype(BF16)
    row = lambda z: z.reshape(1, -1)
    full = lambda shape: pl.BlockSpec(shape, lambda i: (0,) * len(shape))
    vec = full((1, B_WIDTH))
    tile_buf = [pltpu.VMEM((tm, B_WIDTH), F32)] * 7
    return pl.pallas_call(
        functools.partial(_rwkv_kernel, n_chunks=n_chunks),
        grid=(n_tiles // 2,),
        in_specs=[pl.BlockSpec((2 * tm, SHIFT_W), lambda i: (i, 0)),
                  pl.BlockSpec((tm, SHIFT_W), lambda i: (jnp.minimum(2 * i + 2, n_tiles - 1), 0)),
                  full((1, SHIFT_W)), vec, full((LORA_W + LORA_A, 2 * B_WIDTH)), vec,
                  full((LORA_G, B_WIDTH)), vec, vec,
                  vec, vec, vec],
        out_specs=pl.BlockSpec((2 * tm, B_WIDTH), lambda i: (i, 0)),
        out_shape=jax.ShapeDtypeStruct((t_len, B_WIDTH), BF16),
        scratch_shapes=tile_buf + tile_buf
                       + [pltpu.VMEM((n_blocks, LANES, LANES), F32),
                          pltpu.VMEM((n_chains, 2 * LANES, LANES), BF16),
                          pltpu.VMEM((n_chains, LANES, LANES), F32),
                          pltpu.VMEM((n_chains, LANES, LANES), F32),
                          pltpu.VMEM((tm, B_WIDTH), F32)],
        compiler_params=_params(1),
        name="rwkv_mixer",
    )(p_b, p_b, row(mu), row(w0), wa, row(a0), g2.astype(BF16), row(k_k), row(k_a),
      row(r_k), row(gn_g), row(gn_b))


def _proj_residual_kernel(x_ref, ya_ref, yb_ref, w_ref, o_ref):
    o_ref[...] = (x_ref[...] + _dot(ya_ref[...], w_ref[:A_WIDTH, :])
                  + _dot(yb_ref[...], w_ref[A_WIDTH:, :]))


def proj_residual(x, y_a, y_b, w_out):
    t_len, d = x.shape
    tm = min(ROW_TILE, t_len)
    half = pl.BlockSpec((tm, A_WIDTH), lambda i: (i, 0))
    return pl.pallas_call(
        _proj_residual_kernel,
        grid=(t_len // tm,),
        in_specs=[pl.BlockSpec((tm, d), lambda i: (i, 0)), half, half,
                  pl.BlockSpec((d, d), lambda i: (0, 0))],
        out_specs=pl.BlockSpec((tm, d), lambda i: (i, 0)),
        out_shape=jax.ShapeDtypeStruct((t_len, d), F32),
        compiler_params=_params(1),
        name="proj_residual",
    )(x, y_a, y_b, w_out.astype(BF16))


def _attn_kernel(q_ref, kp_ref, kc_ref, vp_ref, vc_ref, o_ref, lse_ref, *, n_qb):
    blk = ATTN_BLOCK
    half = blk // 2
    span = blk + half
    assert half == C_HEAD
    i = pl.program_id(0)
    lane = lax.broadcasted_iota(jnp.int32, (1, LANES), 1)
    head0 = lane < C_HEAD
    key = lax.broadcasted_iota(jnp.int32, (span, LANES), 0)
    qry = lax.broadcasted_iota(jnp.int32, (span, LANES), 1) % half
    dist = qry + blk - key
    in_window = (dist >= 0) & (dist <= blk)
    first_valid = [in_window & ((key >= blk - sub * half) | (i > 0)) for sub in range(2)]
    swap_heads = lambda z: pltpu.roll(z, C_HEAD, 1)
    n_pairs = C_HEADS // HEADS_PER_BLOCK
    units = [(pair, sub) for pair in range(n_pairs) for sub in range(2)]

    for qb in range(n_qb):
        q_rows = slice(qb * blk, (qb + 1) * blk)
        prev_k, prev_v = (kp_ref, vp_ref) if qb == 0 else (kc_ref.at[(qb - 1) * blk:qb * blk],
                                                           vc_ref.at[(qb - 1) * blk:qb * blk])
        cur_k, cur_v = kc_ref.at[q_rows], vc_ref.at[q_rows]
        valid = first_valid if qb == 0 else [in_window, in_window]
        scores = []
        for pair, sub in units:
            cols = slice(pair * LANES, (pair + 1) * LANES)
            qsub = q_ref[qb * blk + sub * half:qb * blk + (sub + 1) * half, cols]
            qsub = qsub * (C_HEAD ** -0.5)
            zero = jnp.zeros_like(qsub)
            qs = jnp.concatenate([jnp.where(head0, qsub, zero), jnp.where(head0, zero, qsub)], axis=0)
            ks = jnp.concatenate([prev_k[sub * half:, cols], cur_k[:(sub + 1) * half, cols]], axis=0)
            scores.append(jnp.where(valid[sub], _dot_nt(ks, qs), NEG_INF))
        probs, lses = [], []
        for st in scores:
            mx = jnp.max(st, axis=0, keepdims=True)
            e = jnp.exp(st - mx)
            den = jnp.sum(e, axis=0, keepdims=True)
            probs.append((e * (1.0 / den)).astype(BF16))
            lses.append(mx + jnp.log(den))
        for (pair, sub), prob in zip(units, probs):
            cols = slice(pair * LANES, (pair + 1) * LANES)
            vs = jnp.concatenate([prev_v[sub * half:, cols], cur_v[:(sub + 1) * half, cols]], axis=0)
            pv = lax.dot_general(prob, vs, (((0,), (0,)), ((), ())),
                                 preferred_element_type=F32)
            o_ref[qb * blk + sub * half:qb * blk + (sub + 1) * half, cols] = (
                jnp.where(head0, pv[:half], pv[half:]).astype(o_ref.dtype))
        lse_rows = []
        for pair in range(n_pairs):
            l0, l1 = lses[2 * pair], lses[2 * pair + 1]
            lse_rows.append(jnp.where(head0, l0, swap_heads(l1)))
            lse_rows.append(jnp.where(head0, swap_heads(l0), l1))
        lse_t = jnp.concatenate(lse_rows + [jnp.zeros((LANES - C_HEADS, LANES), F32)], axis=0)
        lse_ref[q_rows, :] = lse_t.T


def dilated_attention_pattern(view, dil):
    sub = view.shape[0]
    d = D_MODEL
    blk = ATTN_BLOCK
    n_qb = min(ATTN_MAX_BLOCKS_PER_STEP, sub // blk)
    rows = n_qb * blk
    cur = lambda part: pl.BlockSpec((rows, d), lambda i, r: (i, 3 * r + part))
    prv = lambda part: pl.BlockSpec((blk, d),
                                    lambda i, r: (jnp.maximum(n_qb * i - 1, 0), 3 * r + part))
    return pl.pallas_call(
        functools.partial(_attn_kernel, n_qb=n_qb),
        grid=(sub // rows, dil),
        in_specs=[cur(0), prv(1), cur(1), prv(2), cur(2)],
        out_specs=[pl.BlockSpec((rows, d), lambda i, r: (i, r)),
                   pl.BlockSpec((rows, LANES), lambda i, r: (i, r))],
        out_shape=[jax.ShapeDtypeStruct((sub, dil * d), BF16),
                   jax.ShapeDtypeStruct((sub, dil * LANES), F32)],
        compiler_params=_params(2),
        name=f"dilated_attn_d{dil}",
    )(view, view, view, view, view)


def _attn_combine_proj_kernel(x_ref, *refs, dilations):
    n = len(dilations)
    o_refs, l_refs = refs[:n], refs[n:2 * n]
    ex_ref, w_ref, o_ref = refs[2 * n:2 * n + 3]
    l_tmps = refs[2 * n + 3:]
    tm, d = x_ref.shape
    row = lax.broadcasted_iota(jnp.int32, (tm, tm), 0)
    col = lax.broadcasted_iota(jnp.int32, (tm, tm), 1)
    outs, lses = [], []
    for dil, o_v, l_v, l_tmp in zip(dilations, o_refs, l_refs, l_tmps):
        if dil == 1:
            outs.append(o_v[...].astype(F32))
            lses.append(l_v[...])
            continue
        per = tm // dil
        assert dil & (dil - 1) == 0
        stacked = jnp.concatenate([o_v[:, r * d:(r + 1) * d] for r in range(dil)], axis=0)
        src_row = (row & (dil - 1)) * per + (row >> (dil.bit_length() - 1))
        outs.append(_dot((col == src_row).astype(BF16), stacked))
        for r in range(dil):
            l_tmp[pl.ds(r, per, stride=dil), :] = l_v[:, r * LANES:(r + 1) * LANES]
        lses.append(l_tmp[...])
    mx = functools.reduce(jnp.maximum, lses)
    es = [jnp.exp(l - mx) for l in lses]
    den = functools.reduce(lambda a, b: a + b, es)
    y = jnp.zeros(o_ref.shape, F32)
    for e, o in zip(es, outs):
        y = y + _exact_rhs(e / den, ex_ref[...]) * o
    o_ref[...] = x_ref[...] + _dot(y.astype(BF16), w_ref[...])


def attn_combine_proj(x, outs, lses, w_out, dilations):
    t_len, d = x.shape
    tm = min(ROW_TILE, t_len)
    head_of_lane = jnp.arange(d) // C_HEAD
    expand = (jnp.arange(LANES)[:, None] == head_of_lane[None, :]).astype(BF16)
    wide = pl.BlockSpec((tm, d), lambda i: (i, 0))
    view = lambda width: [pl.BlockSpec((tm // dil, dil * width), lambda i: (i, 0))
                          for dil in dilations]
    return pl.pallas_call(
        functools.partial(_attn_combine_proj_kernel, dilations=dilations),
        grid=(t_len // tm,),
        in_specs=[wide] + view(d) + view(LANES)
                 + [pl.BlockSpec((LANES, d), lambda i: (0, 0)),
                    pl.BlockSpec((d, d), lambda i: (0, 0))],
        out_specs=wide,
        out_shape=jax.ShapeDtypeStruct((t_len, d), F32),
        scratch_shapes=[pltpu.VMEM((tm, LANES), F32) for _ in dilations],
        compiler_params=_params(1),
        name="attn_combine_proj",
    )(x, *outs, *lses, expand, w_out.astype(BF16))


def _ffn_kernel(x_ref, g_ref, wu_ref, cw_ref, cb_ref, wd_ref, fn_ref,
                o_ref, ha_ref, hb_ref, zg0_ref, zg1_ref, zv0_ref, zv1_ref, carry_g_ref,
                carry_v_ref, *, final_norm):
    tm = x_ref.shape[0] // 2
    n_j, tf = wd_ref.shape[0], wd_ref.shape[1]
    halo = SUBLANES
    zg_refs = (zg0_ref, zg1_ref)
    zv_refs = (zv0_ref, zv1_ref)
    assert n_j % 2 == 1

    def tile_cols(j, base):
        if isinstance(j, int):
            return pl.ds(base + j * tf, tf)
        return pl.ds(pl.multiple_of(base + j * tf, tf), tf)

    @pl.when(pl.program_id(0) == 0)
    def _():
        carry_g_ref[...] = jnp.zeros_like(carry_g_ref)
        carry_v_ref[...] = jnp.zeros_like(carry_v_ref)

    def start(rows, h_ref):
        x = x_ref[rows, :]
        h_ref[...] = _rmsnorm(x, g_ref[...]).astype(BF16)
        o_ref[rows, :] = x

    def up(j, slot, h_ref):
        h = h_ref[...]
        for base, z_ref, carry_ref in ((0, zg_refs[slot], carry_g_ref),
                                       (D_FF, zv_refs[slot], carry_v_ref)):
            z = _dot(h, wu_ref[:, tile_cols(j, base)])
            z_ref[0:halo, :] = carry_ref[j]
            z_ref[halo:halo + tm, :] = z
            carry_ref[j] = z[tm - halo:tm, :]

    def finish(j, slot, rows):
        def conv(z_ref, base):
            z = z_ref[halo:halo + tm, :]
            z1 = z_ref[halo - 1:halo - 1 + tm, :]
            z2 = z_ref[halo - 2:halo - 2 + tm, :]
            cw = cw_ref[:, tile_cols(j, base)]
            return (cw[0:1, :] * z2 + cw[1:2, :] * z1 + cw[2:3, :] * z
                    + cb_ref[:, tile_cols(j, base)])

        gate = conv(zg_refs[slot], 0)
        val = conv(zv_refs[slot], D_FF)
        act = (gate * jax.nn.sigmoid(gate) * val).astype(BF16)
        o_ref[rows, :] += _dot(act, wd_ref[j])

    def middle(rows, h_ref, first_slot):
        def body(pair, carry):
            j = 2 * pair
            up(j + 1, 1 - first_slot, h_ref)
            finish(j, first_slot, rows)
            up(j + 2, first_slot, h_ref)
            finish(j + 1, 1 - first_slot, rows)
            return carry
        lax.fori_loop(0, (n_j - 1) // 2, body, 0)

    rows_a, rows_b = slice(0, tm), slice(tm, 2 * tm)
    start(rows_a, ha_ref)
    up(0, 0, ha_ref)
    middle(rows_a, ha_ref, 0)
    start(rows_b, hb_ref)
    up(0, 1, hb_ref)
    finish(n_j - 1, 0, rows_a)
    middle(rows_b, hb_ref, 1)
    finish(n_j - 1, 1, rows_b)

    if final_norm:
        o_ref[...] = _rmsnorm(o_ref[...], fn_ref[...])


def conv_glu_ffn(x, norm_g, w_up, conv_w, conv_b, w_down, final_g, final_norm):
    t_len, d = x.shape
    tm = min(FFN_ROW_TILE, t_len // 2)
    tf = FFN_COL_TILE
    n_j = D_FF // tf
    full = lambda shape: pl.BlockSpec(shape, lambda i: (0,) * len(shape),
                                      pipeline_mode=pl.Buffered(1))
    return pl.pallas_call(
        functools.partial(_ffn_kernel, final_norm=final_norm),
        grid=(t_len // (2 * tm),),
        in_specs=[pl.BlockSpec((2 * tm, d), lambda i: (i, 0)),
                  full((1, d)),
                  full((d, 2 * D_FF)),
                  full((3, 2 * D_FF)),
                  full((1, 2 * D_FF)),
                  full((n_j, tf, d)),
                  full((1, d))],
        out_specs=pl.BlockSpec((2 * tm, d), lambda i: (i, 0)),
        out_shape=jax.ShapeDtypeStruct((t_len, d), F32),
        scratch_shapes=[pltpu.VMEM((tm, d), BF16)] * 2
                       + [pltpu.VMEM((tm + SUBLANES, tf), F32)] * 4
                       + [pltpu.VMEM((n_j, SUBLANES, tf), F32),
                        pltpu.VMEM((n_j, SUBLANES, tf), F32)],
        compiler_params=_params(1),
        name="conv_glu_ffn",
    )(x, norm_g.reshape(1, d), w_up.astype(BF16), conv_w, conv_b.reshape(1, 2 * D_FF),
      w_down.astype(BF16).reshape(n_j, tf, d), final_g.reshape(1, d))


def even_mixer(x, norm, w_in, ln_g, ln_b, w_s, b_s, mu, w0, w2, a0, a2, g2, k_k, k_a, r_k, gn_g,
               gn_b, w_out):
    y_a, p_b = even_in(x, norm, w_in, ln_g, ln_b, w_s, b_s)
    y_b = rwkv_mixer(p_b, mu, w0, w2, a0, a2, g2, k_k, k_a, r_k, gn_g, gn_b)
    return proj_residual(x, y_a, y_b, w_out)


def odd_mixer(x, norm, w_qkv, w_out):
    dilations = tuple(dil for _, dil in C_PATTERNS)
    views = norm_matmul_views(x, norm, w_qkv.astype(BF16), dilations)
    outs, lses = [], []
    for view, dil in zip(views, dilations):
        o, l = dilated_attention_pattern(view, dil)
        outs.append(o)
        lses.append(l)
    return attn_combine_proj(x, outs, lses, w_out, dilations)


def kernel(x, ev_norm, ev_w_in, ev_ln_g, ev_ln_b, ev_w_s, ev_b_s, ev_mu, ev_w0, ev_w2, ev_a0, ev_a2, ev_g2, ev_k_k, ev_k_a, ev_r_k, ev_gn_g, ev_gn_b, ev_w_out, od_norm, od_w_qkv, od_w_out, ff_norm, ff_w_up, ff_conv_w, ff_conv_b, ff_w_down, final_norm):
    bsz, t_len, d = x.shape
    depth = ff_norm.shape[0]
    outs = []
    for bi in range(bsz):
        xb = x[bi]
        for layer in range(depth):
            j = layer // 2
            if layer % 2 == 0:
                xb = even_mixer(xb, ev_norm[j], ev_w_in[j], ev_ln_g[j], ev_ln_b[j], ev_w_s[j],
                                ev_b_s[j], ev_mu[j], ev_w0[j], ev_w2[j], ev_a0[j], ev_a2[j],
                                ev_g2[j], ev_k_k[j], ev_k_a[j], ev_r_k[j], ev_gn_g[j], ev_gn_b[j],
                                ev_w_out[j])
            else:
                xb = odd_mixer(xb, od_norm[j], od_w_qkv[j], od_w_out[j])
            xb = conv_glu_ffn(xb, ff_norm[layer], ff_w_up[layer], ff_conv_w[layer],
                              ff_conv_b[layer], ff_w_down[layer], final_norm,
                              final_norm=(layer == depth - 1))
        outs.append(xb)
    return jnp.stack(outs, axis=0)
```

```python
import functools

import jax
import jax.numpy as jnp
from jax import lax
from jax.experimental import pallas as pl
from jax.experimental.pallas import tpu as pltpu

F32 = jnp.float32
BF16 = jnp.bfloat16

D_MODEL = 1024
A_WIDTH = 512
A_GROUP = 128
A_GROUPS = 4
A_CHUNK = 128
LN_EPS = 1e-5
B_WIDTH = 512
B_HEAD = 64
LORA_W = 64
LORA_A = 64
LORA_G = 128
GN_EPS = 64e-5
SHIFT_W = 3 * B_WIDTH + LORA_W + LORA_A + LORA_G
C_HEAD = 64
C_HEADS = 16
C_PATTERNS = ((128, 1), (512, 4), (2048, 16))
NEG_INF = -1e30
D_FF = 2816
RMS_EPS = 1e-6

LANES = 128
SUBLANES = 8
HEADS_PER_BLOCK = LANES // B_HEAD
VMEM_LIMIT_BYTES = 52 * 1024 * 1024

ROW_TILE = 512
QKV_ROW_TILE = 256
FFN_ROW_TILE = 512
FFN_COL_TILE = 256
RWKV_CHUNK = 64
RWKV_CHUNKS_PER_STEP = 4
ATTN_BLOCK = 128
ATTN_MAX_BLOCKS_PER_STEP = 4
NEUMANN_LEVELS = 5


def _params(n_axes):
    return pltpu.CompilerParams(dimension_semantics=("arbitrary",) * n_axes,
                                vmem_limit_bytes=VMEM_LIMIT_BYTES)


def _rmsnorm(x, g):
    return x * lax.rsqrt(jnp.mean(x * x, axis=-1, keepdims=True) + RMS_EPS) * g


def _dot(a, b):
    return jnp.dot(a, b, preferred_element_type=F32)


def _dot_nt(a, b):
    return lax.dot_general(a, b, (((1,), (1,)), ((), ())), preferred_element_type=F32)


def _split3(x):
    hi = x.astype(BF16)
    r1 = x - hi.astype(F32)
    mid = r1.astype(BF16)
    lo = (r1 - mid.astype(F32)).astype(BF16)
    return hi, mid, lo


def _exact_rhs(x, m_bf16):
    hi, mid, lo = _split3(x)
    return _dot(hi, m_bf16) + (_dot(mid, m_bf16) + _dot(lo, m_bf16))


def _exact_lhs(m_bf16, x):
    hi, mid, lo = _split3(x)
    return _dot(m_bf16, hi) + (_dot(m_bf16, mid) + _dot(m_bf16, lo))


def _norm_matmul_views_kernel(x_ref, g_ref, w_ref, *o_refs, dilations):
    tm = x_ref.shape[0]
    n = w_ref.shape[1]
    h = _rmsnorm(x_ref[...], g_ref[...]).astype(BF16)
    res = _dot(h, w_ref[...]).astype(BF16)
    out_row = lax.broadcasted_iota(jnp.int32, (tm, tm), 0)
    in_row = lax.broadcasted_iota(jnp.int32, (tm, tm), 1)
    for dil, o_ref in zip(dilations, o_refs):
        if dil == 1:
            o_ref[...] = res
            continue
        per = tm // dil
        assert per & (per - 1) == 0
        shift = per.bit_length() - 1
        src_row = (out_row & (per - 1)) * dil + (out_row >> shift)
        grouped = _dot((in_row == src_row).astype(BF16), res).astype(BF16)
        for r in range(dil):
            o_ref[:, r * n:(r + 1) * n] = grouped[r * per:(r + 1) * per, :]


def norm_matmul_views(x, g, w, dilations):
    t_len, d = x.shape
    n = w.shape[1]
    tm = min(QKV_ROW_TILE, t_len)
    return pl.pallas_call(
        functools.partial(_norm_matmul_views_kernel, dilations=dilations),
        grid=(t_len // tm,),
        in_specs=[pl.BlockSpec((tm, d), lambda i: (i, 0)),
                  pl.BlockSpec((1, d), lambda i: (0, 0)),
                  pl.BlockSpec((d, n), lambda i: (0, 0))],
        out_specs=[pl.BlockSpec((tm // dil, dil * n), lambda i: (i, 0)) for dil in dilations],
        out_shape=[jax.ShapeDtypeStruct((t_len // dil, dil * n), BF16) for dil in dilations],
        compiler_params=_params(1),
        name="norm_matmul_views",
    )(x, g.reshape(1, d), w)


def _even_in_kernel(x_ref, g_ref, w_ref, lng_ref, lnb_ref, ws_ref, bs_ref, ya_ref, pb_ref):
    tm = x_ref.shape[0]
    h = _rmsnorm(x_ref[...], g_ref[...]).astype(BF16)
    p = _dot(h, w_ref[...])
    pb_ref[...] = p[:, 2 * A_WIDTH:]

    row = lax.broadcasted_iota(jnp.int32, (A_CHUNK, A_CHUNK), 0)
    col = lax.broadcasted_iota(jnp.int32, (A_CHUNK, A_CHUNK), 1)
    causal = col <= row
    w_masked = [jnp.where(causal, ws_ref[g], 0.0).astype(BF16) for g in range(A_GROUPS)]
    for c in range(tm // A_CHUNK):
        rows = slice(c * A_CHUNK, (c + 1) * A_CHUNK)
        u = p[rows, :A_WIDTH]
        v = p[rows, A_WIDTH:2 * A_WIDTH]
        mean = jnp.mean(v, axis=-1, keepdims=True)
        cen = v - mean
        var = jnp.mean(cen * cen, axis=-1, keepdims=True)
        vn = (cen * lax.rsqrt(var + LN_EPS) * lng_ref[...] + lnb_ref[...]).astype(BF16)
        for g in range(A_GROUPS):
            cols = slice(g * A_GROUP, (g + 1) * A_GROUP)
            mixed = _dot(w_masked[g], vn[:, cols]) + bs_ref[:, g:g + 1]
            ya_ref[rows, cols] = (u[:, cols] * mixed).astype(ya_ref.dtype)


def even_in(x, g, w_in, ln_g, ln_b, w_s, b_s):
    t_len, d = x.shape
    n = w_in.shape[1]
    tm = min(ROW_TILE, t_len)
    full = lambda shape: pl.BlockSpec(shape, lambda i: (0,) * len(shape))
    return pl.pallas_call(
        _even_in_kernel,
        grid=(t_len // tm,),
        in_specs=[pl.BlockSpec((tm, d), lambda i: (i, 0)),
                  full((1, d)), full((d, n)), full((1, A_WIDTH)), full((1, A_WIDTH)),
                  full((A_GROUPS, A_CHUNK, A_CHUNK)), full((A_CHUNK, A_GROUPS))],
        out_specs=[pl.BlockSpec((tm, A_WIDTH), lambda i: (i, 0)),
                   pl.BlockSpec((tm, SHIFT_W), lambda i: (i, 0))],
        out_shape=[jax.ShapeDtypeStruct((t_len, A_WIDTH), BF16),
                   jax.ShapeDtypeStruct((t_len, SHIFT_W), F32)],
        compiler_params=_params(1),
        name="even_in",
    )(x, g.reshape(1, d), w_in.astype(BF16), ln_g.reshape(1, A_WIDTH), ln_b.reshape(1, A_WIDTH),
      w_s, b_s.T)


def _head_sums(x):
    lane = lax.broadcasted_iota(jnp.int32, (1, LANES), 1)
    head0 = lane < B_HEAD
    outs = []
    for blk in range(x.shape[1] // LANES):
        xb = x[:, blk * LANES:(blk + 1) * LANES]
        s0 = jnp.sum(jnp.where(head0, xb, 0.0), axis=-1, keepdims=True)
        s1 = jnp.sum(jnp.where(head0, 0.0, xb), axis=-1, keepdims=True)
        outs.append(jnp.where(head0, s0, s1))
    return jnp.concatenate(outs, axis=-1)


def _rwkv_prep_tile(p, prev_row, prm, buf):
    mu_ref, w0_ref, wa_ref, a0_ref, g2_ref, kkw_ref, kaw_ref = prm
    r_o, k_o, v_o, kk_o, b_o, lw_o, g_o = buf
    tm = p.shape[0]
    rowid = lax.broadcasted_iota(jnp.int32, (tm, 1), 0)
    prev = jnp.where(rowid == 0, prev_row, pltpu.roll(p, 1, 0))
    ps = p + (prev - p) * mu_ref[...]

    r = ps[:, :B_WIDTH]
    k = ps[:, B_WIDTH:2 * B_WIDTH]
    v = ps[:, 2 * B_WIDTH:3 * B_WIDTH]
    lwa = ps[:, 3 * B_WIDTH:3 * B_WIDTH + LORA_W + LORA_A]
    lg = ps[:, 3 * B_WIDTH + LORA_W + LORA_A:]

    lane = lax.broadcasted_iota(jnp.int32, (1, LORA_W + LORA_A), 1)
    lora_in = jnp.where(lane < LORA_W, jnp.tanh(lwa), lwa).astype(BF16)
    lora = _dot(lora_in, wa_ref[...])
    neg = -(w0_ref[...] + lora[:, :B_WIDTH])
    softplus = jnp.maximum(neg, 0.0) + jnp.log1p(jnp.exp(-jnp.abs(neg)))
    w = -softplus - 0.5
    a = jax.nn.sigmoid(a0_ref[...] + lora[:, B_WIDTH:])
    g = _dot(jax.nn.sigmoid(lg).astype(BF16), g2_ref[...])

    kkr = k * kkw_ref[...]
    kk = kkr * lax.rsqrt(jnp.maximum(_head_sums(kkr * kkr), 1e-24))

    r_o[...] = r
    k_o[...] = k * (1.0 + (a - 1.0) * kaw_ref[...])
    v_o[...] = v
    kk_o[...] = kk
    b_o[...] = kk * a
    lw_o[...] = -jnp.exp(w)
    g_o[...] = g


def _rwkv_scan_tile(buf, prm, o_ref, out_rows, h_ref, po_ref, q_ref, o2_ref, oacc_ref, n_chunks):
    r_ref, k_ref, v_ref, kk_ref, b_ref, lw_ref, g_ref = buf
    rk_ref, gng_ref, gnb_ref = prm
    c_len = RWKV_CHUNK
    n_blocks = B_WIDTH // LANES

    lane = lax.broadcasted_iota(jnp.int32, (1, LANES), 1)
    head0 = lane < B_HEAD
    rho = lax.broadcasted_iota(jnp.int32, (LANES, LANES), 0)
    sig = lax.broadcasted_iota(jnp.int32, (LANES, LANES), 1)
    same_head = (rho // c_len) == (sig // c_len)
    strict = same_head & ((sig % c_len) < (rho % c_len))
    incl = same_head & ((sig % c_len) <= (rho % c_len))
    eye = rho == sig
    tm = n_chunks * c_len
    tri_r = lax.broadcasted_iota(jnp.int32, (tm, tm), 0)
    tri_c = lax.broadcasted_iota(jnp.int32, (tm, tm), 1)
    cum_mat = ((tri_r // c_len == tri_c // c_len) & (tri_c <= tri_r)).astype(BF16)

    def stack(x):
        return jnp.concatenate([jnp.where(head0, x, 0.0), jnp.where(head0, 0.0, x)], axis=0)

    lw_all = lw_ref[...]
    cum_all = _exact_lhs(cum_mat, lw_all)
    chains = []
    for c in range(n_chunks):
        rows = slice(c * c_len, (c + 1) * c_len)
        cum = cum_all[rows, :]
        total = cum[c_len - 1:c_len, :]
        grow = jnp.exp(cum)
        inv = jnp.exp(-cum)
        to_end = jnp.exp(total - cum)
        decay = jnp.exp(total)
        r_t = r_ref[rows, :] * grow
        kk_t = kk_ref[rows, :] * jnp.exp(cum - lw_all[rows, :])
        b_t = b_ref[rows, :] * inv
        k_t = k_ref[rows, :] * inv
        b_e = b_ref[rows, :] * to_end
        k_e = k_ref[rows, :] * to_end
        v = v_ref[rows, :]
        for blk in range(n_blocks):
            cols = slice(blk * LANES, (blk + 1) * LANES)
            kk_s = stack(kk_t[:, cols]).astype(BF16)
            r_s = stack(r_t[:, cols])
            a_all = _dot_nt(
                jnp.concatenate([kk_s, r_s.astype(BF16)], axis=0),
                jnp.concatenate([stack(b_t[:, cols]), stack(k_t[:, cols])], axis=0).astype(BF16))
            neg_l = -jnp.where(strict, a_all[:LANES, :LANES], 0.0)
            v_lhs = jnp.concatenate(
                [stack(k_e[:, cols]).T,
                 jnp.where(incl, a_all[LANES:, LANES:], 0.0),
                 jnp.where(strict, a_all[:LANES, LANES:], 0.0)], axis=0).astype(BF16)
            w_lhs = jnp.concatenate(
                [stack(b_e[:, cols]).T, jnp.where(incl, a_all[LANES:, :LANES], 0.0)],
                axis=0).astype(BF16)
            chains.append(dict(kk_s=kk_s, r_s=r_s, v_s=stack(v[:, cols]).astype(BF16),
                               v_lhs=v_lhs, w_lhs=w_lhs, decay=decay[:, cols],
                               power=neg_l.astype(BF16),
                               t_inv=jnp.where(eye, 1.0, 0.0) + neg_l))

    for ch in chains:
        ch["power"] = _dot(ch["power"], ch["power"]).astype(BF16)
    for level in range(1, NEUMANN_LEVELS):
        for ch in chains:
            both = _dot(jnp.concatenate([ch["power"], ch["t_inv"].astype(BF16)], axis=0),
                        ch["power"])
            ch["power"] = both[:LANES].astype(BF16)
            ch["t_inv"] = ch["t_inv"] + both[LANES:]
    for ch in chains:
        ch["t_inv"] = (ch["t_inv"] + _dot(ch["t_inv"].astype(BF16), ch["power"])).astype(BF16)

    for ch in chains:
        ch["xv"] = _dot(ch["v_lhs"], ch["v_s"])
    for ch in chains:
        rhs = jnp.concatenate([ch["kk_s"], ch["xv"][2 * LANES:].astype(BF16)], axis=1)
        ch["w12"] = _dot(ch["t_inv"], rhs).astype(BF16)
    for idx, ch in enumerate(chains):
        xw = _dot(ch["w_lhs"], ch["w12"])
        p_mat = jnp.where(eye, ch["decay"], 0.0) - xw[:LANES, :LANES]
        o1 = ch["r_s"] - xw[LANES:, :LANES]
        po_ref[idx] = jnp.concatenate([p_mat, o1], axis=0).astype(BF16)
        q_ref[idx] = ch["xv"][:LANES] - xw[:LANES, LANES:]
        o2_ref[idx] = ch["xv"][LANES:2 * LANES] - xw[LANES:, LANES:]

    states = [h_ref[blk] for blk in range(n_blocks)]
    for c in range(n_chunks):
        rows = slice(c * c_len, (c + 1) * c_len)
        pos = [_dot(po_ref[c * n_blocks + blk], states[blk].astype(BF16))
               for blk in range(n_blocks)]
        for blk in range(n_blocks):
            chain = c * n_blocks + blk
            states[blk] = pos[blk][:LANES] + q_ref[chain]
            o_st = pos[blk][LANES:] + o2_ref[chain]
            oacc_ref[rows, blk * LANES:(blk + 1) * LANES] = o_st[:c_len] + o_st[c_len:]
    for blk in range(n_blocks):
        h_ref[blk] = states[blk]

    o = oacc_ref[...]
    mean = _head_sums(o) * (1.0 / B_HEAD)
    cen = o - mean
    var = _head_sums(cen * cen) * (1.0 / B_HEAD)
    normed = cen * lax.rsqrt(var + GN_EPS) * gng_ref[...] + gnb_ref[...]
    bonus = _head_sums(r_ref[...] * k_ref[...] * rk_ref[...]) * v_ref[...]
    o_ref[out_rows, :] = ((normed + bonus) * g_ref[...]).astype(o_ref.dtype)


def _rwkv_kernel(*refs, n_chunks):
    p_ref, pn_ref = refs[0:2]
    prep_prm = refs[2:9]
    scan_prm = refs[9:12]
    o_ref = refs[12]
    buf_a = refs[13:20]
    buf_b = refs[20:27]
    h_ref, po_ref, q_ref, o2_ref, oacc_ref = refs[27:32]
    tm = n_chunks * RWKV_CHUNK
    scan_scratch = (h_ref, po_ref, q_ref, o2_ref, oacc_ref, n_chunks)

    @pl.when(pl.program_id(0) == 0)
    def _():
        h_ref[...] = jnp.zeros_like(h_ref)
        _rwkv_prep_tile(p_ref[0:tm, :], jnp.zeros((1, SHIFT_W), F32), prep_prm, buf_a)

    _rwkv_prep_tile(p_ref[tm:2 * tm, :], p_ref[tm - 1:tm, :], prep_prm, buf_b)
    _rwkv_scan_tile(buf_a, scan_prm, o_ref, slice(0, tm), *scan_scratch)
    _rwkv_prep_tile(pn_ref[...], p_ref[2 * tm - 1:2 * tm, :], prep_prm, buf_a)
    _rwkv_scan_tile(buf_b, scan_prm, o_ref, slice(tm, 2 * tm), *scan_scratch)


def rwkv_mixer(p_b, mu, w0, w2, a0, a2, g2, k_k, k_a, r_k, gn_g, gn_b):
    t_len = p_b.shape[0]
    n_chunks = RWKV_CHUNKS_PER_STEP
    tm = n_chunks * RWKV_CHUNK
    n_tiles = t_len // tm
    n_blocks = B_WIDTH // LANES
    n_chains = n_chunks * n_blocks
    zeros = jnp.zeros((LORA_W, B_WIDTH), F32)
    wa = jnp.concatenate([jnp.concatenate([w2, zeros], 1),
                          jnp.concatenate([zeros, a2], 1)], 0).astype(BF16)
    row = lambda z: z.reshape(1, -1)
    full = lambda shape: pl.BlockSpec(shape, lambda i: (0,) * len(shape))
    vec = full((1, B_WIDTH))
    tile_buf = [pltpu.VMEM((tm, B_WIDTH), F32)] * 7
    return pl.pallas_call(
        functools.partial(_rwkv_kernel, n_chunks=n_chunks),
        grid=(n_tiles // 2,),
        in_specs=[pl.BlockSpec((2 * tm, SHIFT_W), lambda i: (i, 0)),
                  pl.BlockSpec((tm, SHIFT_W), lambda i: (jnp.minimum(2 * i + 2, n_tiles - 1), 0)),
                  full((1, SHIFT_W)), vec, full((LORA_W + LORA_A, 2 * B_WIDTH)), vec,
                  full((LORA_G, B_WIDTH)), vec, vec,
                  vec, vec, vec],
        out_specs=pl.BlockSpec((2 * tm, B_WIDTH), lambda i: (i, 0)),
        out_shape=jax.ShapeDtypeStruct((t_len, B_WIDTH), BF16),
        scratch_shapes=tile_buf + tile_buf
                       + [pltpu.VMEM((n_blocks, LANES, LANES), F32),
                          pltpu.VMEM((n_chains, 2 * LANES, LANES), BF16),
                          pltpu.VMEM((n_chains, LANES, LANES), F32),
                          pltpu.VMEM((n_chains, LANES, LANES), F32),
                          pltpu.VMEM((tm, B_WIDTH), F32)],
        compiler_params=_params(1),
        name="rwkv_mixer",
    )(p_b, p_b, row(mu), row(w0), wa, row(a0), g2.astype(BF16), row(k_k), row(k_a),
      row(r_k), row(gn_g), row(gn_b))


def _attn_kernel(q_ref, kp_ref, kc_ref, vp_ref, vc_ref, o_ref, lse_ref, *, n_qb):
    blk = ATTN_BLOCK
    half = blk // 2
    span = blk + half
    assert half == C_HEAD
    i = pl.program_id(0)
    lane = lax.broadcasted_iota(jnp.int32, (1, LANES), 1)
    head0 = lane < C_HEAD
    key = lax.broadcasted_iota(jnp.int32, (span, LANES), 0)
    qry = lax.broadcasted_iota(jnp.int32, (span, LANES), 1) % half
    dist = qry + blk - key
    in_window = (dist >= 0) & (dist <= blk)
    first_valid = [in_window & ((key >= blk - sub * half) | (i > 0)) for sub in range(2)]
    swap_heads = lambda z: pltpu.roll(z, C_HEAD, 1)
    n_pairs = C_HEADS // HEADS_PER_BLOCK
    units = [(pair, sub) for pair in range(n_pairs) for sub in range(2)]

    for qb in range(n_qb):
        q_rows = slice(qb * blk, (qb + 1) * blk)
        prev_k, prev_v = (kp_ref, vp_ref) if qb == 0 else (kc_ref.at[(qb - 1) * blk:qb * blk],
                                                           vc_ref.at[(qb - 1) * blk:qb * blk])
        cur_k, cur_v = kc_ref.at[q_rows], vc_ref.at[q_rows]
        valid = first_valid if qb == 0 else [in_window, in_window]
        scores = []
        for pair, sub in units:
            cols = slice(pair * LANES, (pair + 1) * LANES)
            qsub = q_ref[qb * blk + sub * half:qb * blk + (sub + 1) * half, cols]
            qsub = qsub * (C_HEAD ** -0.5)
            zero = jnp.zeros_like(qsub)
            qs = jnp.concatenate([jnp.where(head0, qsub, zero), jnp.where(head0, zero, qsub)], axis=0)
            ks = jnp.concatenate([prev_k[sub * half:, cols], cur_k[:(sub + 1) * half, cols]], axis=0)
            scores.append(jnp.where(valid[sub], _dot_nt(ks, qs), NEG_INF))
        probs, lses = [], []
        for st in scores:
            mx = jnp.max(st, axis=0, keepdims=True)
            e = jnp.exp(st - mx)
            den = jnp.sum(e, axis=0, keepdims=True)
            probs.append((e * (1.0 / den)).astype(BF16))
            lses.append(mx + jnp.log(den))
        for (pair, sub), prob in zip(units, probs):
            cols = slice(pair * LANES, (pair + 1) * LANES)
            vs = jnp.concatenate([prev_v[sub * half:, cols], cur_v[:(sub + 1) * half, cols]], axis=0)
            pv = lax.dot_general(prob, vs, (((0,), (0,)), ((), ())),
                                 preferred_element_type=F32)
            o_ref[qb * blk + sub * half:qb * blk + (sub + 1) * half, cols] = (
                jnp.where(head0, pv[:half], pv[half:]).astype(o_ref.dtype))
        lse_rows = []
        for pair in range(n_pairs):
            l0, l1 = lses[2 * pair], lses[2 * pair + 1]
            lse_rows.append(jnp.where(head0, l0, swap_heads(l1)))
            lse_rows.append(jnp.where(head0, swap_heads(l0), l1))
        lse_t = jnp.concatenate(lse_rows + [jnp.zeros((LANES - C_HEADS, LANES), F32)], axis=0)
        lse_ref[q_rows, :] = lse_t.T


def dilated_attention_pattern(view, dil):
    sub = view.shape[0]
    d = D_MODEL
    blk = ATTN_BLOCK
    n_qb = min(ATTN_MAX_BLOCKS_PER_STEP, sub // blk)
    rows = n_qb * blk
    cur = lambda part: pl.BlockSpec((rows, d), lambda i, r: (i, 3 * r + part))
    prv = lambda part: pl.BlockSpec((blk, d),
                                    lambda i, r: (jnp.maximum(n_qb * i - 1, 0), 3 * r + part))
    return pl.pallas_call(
        functools.partial(_attn_kernel, n_qb=n_qb),
        grid=(sub // rows, dil),
        in_specs=[cur(0), prv(1), cur(1), prv(2), cur(2)],
        out_specs=[pl.BlockSpec((rows, d), lambda i, r: (i, r)),
                   pl.BlockSpec((rows, LANES), lambda i, r: (i, r))],
        out_shape=[jax.ShapeDtypeStruct((sub, dil * d), BF16),
                   jax.ShapeDtypeStruct((sub, dil * LANES), F32)],
        compiler_params=_params(2),
        name=f"dilated_attn_d{dil}",
    )(view, view, view, view, view)


def _attn_combine_proj_kernel(x_ref, *refs, dilations):
    n = len(dilations)
    o_refs, l_refs = refs[:n], refs[n:2 * n]
    ex_ref, w_ref, o_ref = refs[2 * n:2 * n + 3]
    l_tmps = refs[2 * n + 3:]
    tm, d = x_ref.shape
    row = lax.broadcasted_iota(jnp.int32, (tm, tm), 0)
    col = lax.broadcasted_iota(jnp.int32, (tm, tm), 1)
    outs, lses = [], []
    for dil, o_v, l_v, l_tmp in zip(dilations, o_refs, l_refs, l_tmps):
        if dil == 1:
            outs.append(o_v[...].astype(F32))
            lses.append(l_v[...])
            continue
        per = tm // dil
        assert dil & (dil - 1) == 0
        stacked = jnp.concatenate([o_v[:, r * d:(r + 1) * d] for r in range(dil)], axis=0)
        src_row = (row & (dil - 1)) * per + (row >> (dil.bit_length() - 1))
        outs.append(_dot((col == src_row).astype(BF16), stacked))
        for r in range(dil):
            l_tmp[pl.ds(r, per, stride=dil), :] = l_v[:, r * LANES:(r + 1) * LANES]
        lses.append(l_tmp[...])
    mx = functools.reduce(jnp.maximum, lses)
    es = [jnp.exp(l - mx) for l in lses]
    den = functools.reduce(lambda a, b: a + b, es)
    y = outs[-1]
    for e, o in zip(es[:-1], outs[:-1]):
        y = y + _exact_rhs(e / den, ex_ref[...]) * (o - outs[-1])
    o_ref[...] = x_ref[...] + _dot(y.astype(BF16), w_ref[...])


def attn_combine_proj(x, outs, lses, w_out, dilations):
    t_len, d = x.shape
    tm = min(ROW_TILE, t_len)
    head_of_lane = jnp.arange(d) // C_HEAD
    expand = (jnp.arange(LANES)[:, None] == head_of_lane[None, :]).astype(BF16)
    wide = pl.BlockSpec((tm, d), lambda i: (i, 0))
    view = lambda width: [pl.BlockSpec((tm // dil, dil * width), lambda i: (i, 0))
                          for dil in dilations]
    return pl.pallas_call(
        functools.partial(_attn_combine_proj_kernel, dilations=dilations),
        grid=(t_len // tm,),
        in_specs=[wide] + view(d) + view(LANES)
                 + [pl.BlockSpec((LANES, d), lambda i: (0, 0)),
                    pl.BlockSpec((d, d), lambda i: (0, 0))],
        out_specs=wide,
        out_shape=jax.ShapeDtypeStruct((t_len, d), F32),
        scratch_shapes=[pltpu.VMEM((tm, LANES), F32) for _ in dilations],
        compiler_params=_params(1),
        name="attn_combine_proj",
    )(x, *outs, *lses, expand, w_out.astype(BF16))


def _ffn_kernel(*refs, final_norm, mixer_proj):
    x_ref = refs[0]
    if mixer_proj:
        ya_ref, yb_ref, wo_ref = refs[1:4]
        refs = refs[:1] + refs[4:]
    (_, g_ref, wu_ref, cw_ref, cb_ref, wd_ref, fn_ref, o_ref, ha_ref, hb_ref, zg0_ref, zg1_ref,
     zv0_ref, zv1_ref, carry_g_ref, carry_v_ref) = refs
    tm = x_ref.shape[0] // 2
    n_j, tf = wd_ref.shape[0], wd_ref.shape[1]
    halo = SUBLANES
    zg_refs = (zg0_ref, zg1_ref)
    zv_refs = (zv0_ref, zv1_ref)
    assert n_j % 2 == 1

    def tile_cols(j, base):
        if isinstance(j, int):
            return pl.ds(base + j * tf, tf)
        return pl.ds(pl.multiple_of(base + j * tf, tf), tf)

    @pl.when(pl.program_id(0) == 0)
    def _():
        carry_g_ref[...] = jnp.zeros_like(carry_g_ref)
        carry_v_ref[...] = jnp.zeros_like(carry_v_ref)

    def start(rows, h_ref):
        x = x_ref[rows, :]
        if mixer_proj:
            x = (x + _dot(ya_ref[rows, :], wo_ref[:A_WIDTH, :])
                 + _dot(yb_ref[rows, :], wo_ref[A_WIDTH:, :]))
        h_ref[...] = _rmsnorm(x, g_ref[...]).astype(BF16)
        o_ref[rows, :] = x

    def up(j, slot, h_ref):
        h = h_ref[...]
        for base, z_ref, carry_ref in ((0, zg_refs[slot], carry_g_ref),
                                       (D_FF, zv_refs[slot], carry_v_ref)):
            z = _dot(h, wu_ref[:, tile_cols(j, base)])
            z_ref[0:halo, :] = carry_ref[j]
            z_ref[halo:halo + tm, :] = z
            carry_ref[j] = z[tm - halo:tm, :]

    def finish(j, slot, rows):
        def conv(z_ref, base):
            z = z_ref[halo:halo + tm, :]
            z1 = z_ref[halo - 1:halo - 1 + tm, :]
            z2 = z_ref[halo - 2:halo - 2 + tm, :]
            cw = cw_ref[:, tile_cols(j, base)]
            return (cw[0:1, :] * z2 + cw[1:2, :] * z1 + cw[2:3, :] * z
                    + cb_ref[:, tile_cols(j, base)])

        gate = conv(zg_refs[slot], 0)
        val = conv(zv_refs[slot], D_FF)
        act = (gate * jax.nn.sigmoid(gate) * val).astype(BF16)
        o_ref[rows, :] += _dot(act, wd_ref[j])

    def middle(rows, h_ref, first_slot):
        def body(pair, carry):
            j = 2 * pair
            up(j + 1, 1 - first_slot, h_ref)
            finish(j, first_slot, rows)
            up(j + 2, first_slot, h_ref)
            finish(j + 1, 1 - first_slot, rows)
            return carry
        lax.fori_loop(0, (n_j - 1) // 2, body, 0)

    rows_a, rows_b = slice(0, tm), slice(tm, 2 * tm)
    start(rows_a, ha_ref)
    up(0, 0, ha_ref)
    middle(rows_a, ha_ref, 0)
    start(rows_b, hb_ref)
    up(0, 1, hb_ref)
    finish(n_j - 1, 0, rows_a)
    middle(rows_b, hb_ref, 1)
    finish(n_j - 1, 1, rows_b)

    if final_norm:
        o_ref[...] = _rmsnorm(o_ref[...], fn_ref[...])


def conv_glu_ffn(x, norm_g, w_up, conv_w, conv_b, w_down, final_g, final_norm, mixer=None):
    t_len, d = x.shape
    tm = min(FFN_ROW_TILE, t_len // 2)
    tf = FFN_COL_TILE
    n_j = D_FF // tf
    full = lambda shape: pl.BlockSpec(shape, lambda i: (0,) * len(shape),
                                      pipeline_mode=pl.Buffered(1))
    mixer_specs, mixer_args = [], []
    if mixer is not None:
        y_a, y_b, w_out = mixer
        half = pl.BlockSpec((2 * tm, A_WIDTH), lambda i: (i, 0))
        mixer_specs = [half, half, full((d, d))]
        mixer_args = [y_a, y_b, w_out.astype(BF16)]
    return pl.pallas_call(
        functools.partial(_ffn_kernel, final_norm=final_norm, mixer_proj=mixer is not None),
        grid=(t_len // (2 * tm),),
        in_specs=[pl.BlockSpec((2 * tm, d), lambda i: (i, 0))] + mixer_specs
                 + [full((1, d)),
                    full((d, 2 * D_FF)),
                    full((3, 2 * D_FF)),
                    full((1, 2 * D_FF)),
                    full((n_j, tf, d)),
                    full((1, d))],
        out_specs=pl.BlockSpec((2 * tm, d), lambda i: (i, 0)),
        out_shape=jax.ShapeDtypeStruct((t_len, d), F32),
        scratch_shapes=[pltpu.VMEM((tm, d), BF16)] * 2
                       + [pltpu.VMEM((tm + SUBLANES, tf), F32)] * 4
                       + [pltpu.VMEM((n_j, SUBLANES, tf), F32),
                        pltpu.VMEM((n_j, SUBLANES, tf), F32)],
        compiler_params=_params(1),
        name="conv_glu_ffn",
    )(x, *mixer_args, norm_g.reshape(1, d), w_up.astype(BF16), conv_w,
      conv_b.reshape(1, 2 * D_FF), w_down.astype(BF16).reshape(n_j, tf, d), final_g.reshape(1, d))


def even_mixer(x, norm, w_in, ln_g, ln_b, w_s, b_s, mu, w0, w2, a0, a2, g2, k_k, k_a, r_k, gn_g,
               gn_b):
    y_a, p_b = even_in(x, norm, w_in, ln_g, ln_b, w_s, b_s)
    y_b = rwkv_mixer(p_b, mu, w0, w2, a0, a2, g2, k_k, k_a, r_k, gn_g, gn_b)
    return y_a, y_b


def odd_mixer(x, norm, w_qkv, w_out):
    dilations = tuple(dil for _, dil in C_PATTERNS)
    views = norm_matmul_views(x, norm, w_qkv.astype(BF16), dilations)
    outs, lses = [], []
    for view, dil in zip(views, dilations):
        o, l = dilated_attention_pattern(view, dil)
        outs.append(o)
        lses.append(l)
    return attn_combine_proj(x, outs, lses, w_out, dilations)


def kernel(x, ev_norm, ev_w_in, ev_ln_g, ev_ln_b, ev_w_s, ev_b_s, ev_mu, ev_w0, ev_w2, ev_a0, ev_a2, ev_g2, ev_k_k, ev_k_a, ev_r_k, ev_gn_g, ev_gn_b, ev_w_out, od_norm, od_w_qkv, od_w_out, ff_norm, ff_w_up, ff_conv_w, ff_conv_b, ff_w_down, final_norm):
    bsz, t_len, d = x.shape
    depth = ff_norm.shape[0]
    outs = []
    for bi in range(bsz):
        xb = x[bi]
        for layer in range(depth):
            j = layer // 2
            mixer = None
            if layer % 2 == 0:
                y_a, y_b = even_mixer(xb, ev_norm[j], ev_w_in[j], ev_ln_g[j], ev_ln_b[j],
                                      ev_w_s[j], ev_b_s[j], ev_mu[j], ev_w0[j], ev_w2[j], ev_a0[j],
                                      ev_a2[j], ev_g2[j], ev_k_k[j], ev_k_a[j], ev_r_k[j],
                                      ev_gn_g[j], ev_gn_b[j])
                mixer = (y_a, y_b, ev_w_out[j])
            else:
                xb = odd_mixer(xb, od_norm[j], od_w_qkv[j], od_w_out[j])
            xb = conv_glu_ffn(xb, ff_norm[layer], ff_w_up[layer], ff_conv_w[layer],
                              ff_conv_b[layer], ff_w_down[layer], final_norm,
                              final_norm=(layer == depth - 1), mixer=mixer)
        outs.append(xb)
    return jnp.stack(outs, axis=0)
```

```python
import functools

import jax
import jax.numpy as jnp
from jax import lax
from jax.experimental import pallas as pl
from jax.experimental.pallas import tpu as pltpu

F32 = jnp.float32
BF16 = jnp.bfloat16

D_MODEL = 1024
A_WIDTH = 512
A_GROUP = 128
A_GROUPS = 4
A_CHUNK = 128
LN_EPS = 1e-5
B_WIDTH = 512
B_HEAD = 64
LORA_W = 64
LORA_A = 64
LORA_G = 128
GN_EPS = 64e-5
SHIFT_W = 3 * B_WIDTH + LORA_W + LORA_A + LORA_G
C_HEAD = 64
C_HEADS = 16
C_PATTERNS = ((128, 1), (512, 4), (2048, 16))
NEG_INF = -1e30
D_FF = 2816
RMS_EPS = 1e-6

LANES = 128
SUBLANES = 8
HEADS_PER_BLOCK = LANES // B_HEAD
VMEM_LIMIT_BYTES = 52 * 1024 * 1024

ROW_TILE = 512
QKV_ROW_TILE = 256
FFN_ROW_TILE = 512
FFN_COL_TILE = 256
RWKV_CHUNK = 64
RWKV_CHUNKS_PER_STEP = 4
ATTN_BLOCK = 128
ATTN_MAX_BLOCKS_PER_STEP = 8
NEUMANN_LEVELS = 5


def _params(n_axes):
    return pltpu.CompilerParams(dimension_semantics=("arbitrary",) * n_axes,
                                vmem_limit_bytes=VMEM_LIMIT_BYTES)


def _rmsnorm(x, g):
    return x * lax.rsqrt(jnp.mean(x * x, axis=-1, keepdims=True) + RMS_EPS) * g


def _dot(a, b):
    return jnp.dot(a, b, preferred_element_type=F32)


def _dot_nt(a, b):
    return lax.dot_general(a, b, (((1,), (1,)), ((), ())), preferred_element_type=F32)


def _split3(x):
    hi = x.astype(BF16)
    r1 = x - hi.astype(F32)
    mid = r1.astype(BF16)
    lo = (r1 - mid.astype(F32)).astype(BF16)
    return hi, mid, lo


def _exact_rhs(x, m_bf16):
    hi, mid, lo = _split3(x)
    return _dot(hi, m_bf16) + (_dot(mid, m_bf16) + _dot(lo, m_bf16))


def _exact_lhs(m_bf16, x):
    hi, mid, lo = _split3(x)
    return _dot(m_bf16, hi) + (_dot(m_bf16, mid) + _dot(m_bf16, lo))


def _norm_matmul_views_kernel(x_ref, g_ref, w_ref, *o_refs, dilations):
    tm = x_ref.shape[0]
    n = w_ref.shape[1]
    h = _rmsnorm(x_ref[...], g_ref[...]).astype(BF16)
    res = _dot(h, w_ref[...]).astype(BF16)
    out_row = lax.broadcasted_iota(jnp.int32, (tm, tm), 0)
    in_row = lax.broadcasted_iota(jnp.int32, (tm, tm), 1)
    for dil, o_ref in zip(dilations, o_refs):
        if dil == 1:
            o_ref[...] = res
            continue
        per = tm // dil
        assert per & (per - 1) == 0
        shift = per.bit_length() - 1
        src_row = (out_row & (per - 1)) * dil + (out_row >> shift)
        grouped = _dot((in_row == src_row).astype(BF16), res).astype(BF16)
        for r in range(dil):
            o_ref[:, r * n:(r + 1) * n] = grouped[r * per:(r + 1) * per, :]


def norm_matmul_views(x, g, w, dilations):
    t_len, d = x.shape
    n = w.shape[1]
    tm = min(QKV_ROW_TILE, t_len)
    return pl.pallas_call(
        functools.partial(_norm_matmul_views_kernel, dilations=dilations),
        grid=(t_len // tm,),
        in_specs=[pl.BlockSpec((tm, d), lambda i: (i, 0)),
                  pl.BlockSpec((1, d), lambda i: (0, 0)),
                  pl.BlockSpec((d, n), lambda i: (0, 0))],
        out_specs=[pl.BlockSpec((tm // dil, dil * n), lambda i: (i, 0)) for dil in dilations],
        out_shape=[jax.ShapeDtypeStruct((t_len // dil, dil * n), BF16) for dil in dilations],
        compiler_params=_params(1),
        name="norm_matmul_views",
    )(x, g.reshape(1, d), w)


def _even_in_kernel(x_ref, g_ref, w_ref, lng_ref, lnb_ref, ws_ref, bs_ref, ya_ref, pb_ref):
    tm = x_ref.shape[0]
    h = _rmsnorm(x_ref[...], g_ref[...]).astype(BF16)
    p = _dot(h, w_ref[...])
    pb_ref[...] = p[:, 2 * A_WIDTH:]

    row = lax.broadcasted_iota(jnp.int32, (A_CHUNK, A_CHUNK), 0)
    col = lax.broadcasted_iota(jnp.int32, (A_CHUNK, A_CHUNK), 1)
    causal = col <= row
    w_masked = [jnp.where(causal, ws_ref[g], 0.0).astype(BF16) for g in range(A_GROUPS)]
    for c in range(tm // A_CHUNK):
        rows = slice(c * A_CHUNK, (c + 1) * A_CHUNK)
        u = p[rows, :A_WIDTH]
        v = p[rows, A_WIDTH:2 * A_WIDTH]
        mean = jnp.mean(v, axis=-1, keepdims=True)
        cen = v - mean
        var = jnp.mean(cen * cen, axis=-1, keepdims=True)
        vn = (cen * lax.rsqrt(var + LN_EPS) * lng_ref[...] + lnb_ref[...]).astype(BF16)
        for g in range(A_GROUPS):
            cols = slice(g * A_GROUP, (g + 1) * A_GROUP)
            mixed = _dot(w_masked[g], vn[:, cols]) + bs_ref[:, g:g + 1]
            ya_ref[rows, cols] = (u[:, cols] * mixed).astype(ya_ref.dtype)


def even_in(x, g, w_in, ln_g, ln_b, w_s, b_s):
    t_len, d = x.shape
    n = w_in.shape[1]
    tm = min(ROW_TILE, t_len)
    full = lambda shape: pl.BlockSpec(shape, lambda i: (0,) * len(shape))
    return pl.pallas_call(
        _even_in_kernel,
        grid=(t_len // tm,),
        in_specs=[pl.BlockSpec((tm, d), lambda i: (i, 0)),
                  full((1, d)), full((d, n)), full((1, A_WIDTH)), full((1, A_WIDTH)),
                  full((A_GROUPS, A_CHUNK, A_CHUNK)), full((A_CHUNK, A_GROUPS))],
        out_specs=[pl.BlockSpec((tm, A_WIDTH), lambda i: (i, 0)),
                   pl.BlockSpec((tm, SHIFT_W), lambda i: (i, 0))],
        out_shape=[jax.ShapeDtypeStruct((t_len, A_WIDTH), BF16),
                   jax.ShapeDtypeStruct((t_len, SHIFT_W), F32)],
        compiler_params=_params(1),
        name="even_in",
    )(x, g.reshape(1, d), w_in.astype(BF16), ln_g.reshape(1, A_WIDTH), ln_b.reshape(1, A_WIDTH),
      w_s, b_s.T)


def _head_sums(x):
    lane = lax.broadcasted_iota(jnp.int32, (1, LANES), 1)
    head0 = lane < B_HEAD
    outs = []
    for blk in range(x.shape[1] // LANES):
        xb = x[:, blk * LANES:(blk + 1) * LANES]
        s0 = jnp.sum(jnp.where(head0, xb, 0.0), axis=-1, keepdims=True)
        s1 = jnp.sum(jnp.where(head0, 0.0, xb), axis=-1, keepdims=True)
        outs.append(jnp.where(head0, s0, s1))
    return jnp.concatenate(outs, axis=-1)


def _rwkv_prep_tile(p, prev_row, prm, buf):
    mu_ref, w0_ref, wa_ref, a0_ref, g2_ref, kkw_ref, kaw_ref = prm
    r_o, k_o, v_o, kk_o, b_o, lw_o, g_o = buf
    tm = p.shape[0]
    rowid = lax.broadcasted_iota(jnp.int32, (tm, 1), 0)
    prev = jnp.where(rowid == 0, prev_row, pltpu.roll(p, 1, 0))
    ps = p + (prev - p) * mu_ref[...]

    r = ps[:, :B_WIDTH]
    k = ps[:, B_WIDTH:2 * B_WIDTH]
    v = ps[:, 2 * B_WIDTH:3 * B_WIDTH]
    lwa = ps[:, 3 * B_WIDTH:3 * B_WIDTH + LORA_W + LORA_A]
    lg = ps[:, 3 * B_WIDTH + LORA_W + LORA_A:]

    lane = lax.broadcasted_iota(jnp.int32, (1, LORA_W + LORA_A), 1)
    lora_in = jnp.where(lane < LORA_W, jnp.tanh(lwa), lwa).astype(BF16)
    lora = _dot(lora_in, wa_ref[...])
    neg = -(w0_ref[...] + lora[:, :B_WIDTH])
    softplus = jnp.maximum(neg, 0.0) + jnp.log(1.0 + jnp.exp(-jnp.abs(neg)))
    w = -softplus - 0.5
    a = jax.nn.sigmoid(a0_ref[...] + lora[:, B_WIDTH:])
    g = _dot(jax.nn.sigmoid(lg).astype(BF16), g2_ref[...])

    kkr = k * kkw_ref[...]
    kk = kkr * lax.rsqrt(jnp.maximum(_head_sums(kkr * kkr), 1e-24))

    r_o[...] = r
    k_o[...] = k * (1.0 + (a - 1.0) * kaw_ref[...])
    v_o[...] = v
    kk_o[...] = kk
    b_o[...] = kk * a
    lw_o[...] = -jnp.exp(w)
    g_o[...] = g


def _rwkv_scan_tile(buf, prm, o_ref, out_rows, h_ref, po_ref, q_ref, o2_ref, oacc_ref, n_chunks):
    r_ref, k_ref, v_ref, kk_ref, b_ref, lw_ref, g_ref = buf
    rk_ref, gng_ref, gnb_ref = prm
    c_len = RWKV_CHUNK
    n_blocks = B_WIDTH // LANES

    lane = lax.broadcasted_iota(jnp.int32, (1, LANES), 1)
    head0 = lane < B_HEAD
    rho = lax.broadcasted_iota(jnp.int32, (LANES, LANES), 0)
    sig = lax.broadcasted_iota(jnp.int32, (LANES, LANES), 1)
    same_head = (rho // c_len) == (sig // c_len)
    strict = same_head & ((sig % c_len) < (rho % c_len))
    incl = same_head & ((sig % c_len) <= (rho % c_len))
    eye = rho == sig
    tm = n_chunks * c_len
    tri_r = lax.broadcasted_iota(jnp.int32, (tm, tm), 0)
    tri_c = lax.broadcasted_iota(jnp.int32, (tm, tm), 1)
    cum_mat = ((tri_r // c_len == tri_c // c_len) & (tri_c <= tri_r)).astype(BF16)

    def stack(x):
        return jnp.concatenate([jnp.where(head0, x, 0.0), jnp.where(head0, 0.0, x)], axis=0)

    lw_all = lw_ref[...]
    cum_all = _exact_lhs(cum_mat, lw_all)
    chains = []
    for c in range(n_chunks):
        rows = slice(c * c_len, (c + 1) * c_len)
        cum = cum_all[rows, :]
        total = cum[c_len - 1:c_len, :]
        grow = jnp.exp(cum)
        inv = jnp.exp(-cum)
        to_end = jnp.exp(total - cum)
        decay = jnp.exp(total)
        r_t = r_ref[rows, :] * grow
        kk_t = kk_ref[rows, :] * jnp.exp(cum - lw_all[rows, :])
        b_t = b_ref[rows, :] * inv
        k_t = k_ref[rows, :] * inv
        b_e = b_ref[rows, :] * to_end
        k_e = k_ref[rows, :] * to_end
        v = v_ref[rows, :]
        for blk in range(n_blocks):
            cols = slice(blk * LANES, (blk + 1) * LANES)
            kk_s = stack(kk_t[:, cols]).astype(BF16)
            r_s = stack(r_t[:, cols])
            a_all = _dot_nt(
                jnp.concatenate([kk_s, r_s.astype(BF16)], axis=0),
                jnp.concatenate([stack(b_t[:, cols]), stack(k_t[:, cols])], axis=0).astype(BF16))
            neg_l = -jnp.where(strict, a_all[:LANES, :LANES], 0.0)
            v_lhs = jnp.concatenate(
                [stack(k_e[:, cols]).T,
                 jnp.where(incl, a_all[LANES:, LANES:], 0.0),
                 jnp.where(strict, a_all[:LANES, LANES:], 0.0)], axis=0).astype(BF16)
            w_lhs = jnp.concatenate(
                [stack(b_e[:, cols]).T, jnp.where(incl, a_all[LANES:, :LANES], 0.0)],
                axis=0).astype(BF16)
            chains.append(dict(kk_s=kk_s, r_s=r_s, v_s=stack(v[:, cols]).astype(BF16),
                               v_lhs=v_lhs, w_lhs=w_lhs, decay=decay[:, cols],
                               power=neg_l.astype(BF16),
                               t_inv=jnp.where(eye, 1.0, 0.0) + neg_l))

    for ch in chains:
        ch["power"] = _dot(ch["power"], ch["power"]).astype(BF16)
    for level in range(1, NEUMANN_LEVELS):
        for ch in chains:
            both = _dot(jnp.concatenate([ch["power"], ch["t_inv"].astype(BF16)], axis=0),
                        ch["power"])
            ch["power"] = both[:LANES].astype(BF16)
            ch["t_inv"] = ch["t_inv"] + both[LANES:]
    for ch in chains:
        ch["t_inv"] = (ch["t_inv"] + _dot(ch["t_inv"].astype(BF16), ch["power"])).astype(BF16)

    for ch in chains:
        ch["xv"] = _dot(ch["v_lhs"], ch["v_s"])
    for ch in chains:
        rhs = jnp.concatenate([ch["kk_s"], ch["xv"][2 * LANES:].astype(BF16)], axis=1)
        ch["w12"] = _dot(ch["t_inv"], rhs).astype(BF16)
    for idx, ch in enumerate(chains):
        xw = _dot(ch["w_lhs"], ch["w12"])
        p_mat = jnp.where(eye, ch["decay"], 0.0) - xw[:LANES, :LANES]
        o1 = ch["r_s"] - xw[LANES:, :LANES]
        po_ref[idx] = jnp.concatenate([p_mat, o1], axis=0).astype(BF16)
        q_ref[idx] = ch["xv"][:LANES] - xw[:LANES, LANES:]
        o2_ref[idx] = ch["xv"][LANES:2 * LANES] - xw[LANES:, LANES:]

    states = [h_ref[blk] for blk in range(n_blocks)]
    for c in range(n_chunks):
        rows = slice(c * c_len, (c + 1) * c_len)
        pos = [_dot(po_ref[c * n_blocks + blk], states[blk].astype(BF16))
               for blk in range(n_blocks)]
        for blk in range(n_blocks):
            chain = c * n_blocks + blk
            states[blk] = pos[blk][:LANES] + q_ref[chain]
            o_st = pos[blk][LANES:] + o2_ref[chain]
            oacc_ref[rows, blk * LANES:(blk + 1) * LANES] = o_st[:c_len] + o_st[c_len:]
    for blk in range(n_blocks):
        h_ref[blk] = states[blk]

    o = oacc_ref[...]
    mean = _head_sums(o) * (1.0 / B_HEAD)
    cen = o - mean
    var = _head_sums(cen * cen) * (1.0 / B_HEAD)
    normed = cen * lax.rsqrt(var + GN_EPS) * gng_ref[...] + gnb_ref[...]
    bonus = _head_sums(r_ref[...] * k_ref[...] * rk_ref[...]) * v_ref[...]
    o_ref[out_rows, :] = ((normed + bonus) * g_ref[...]).astype(o_ref.dtype)


def _rwkv_kernel(*refs, n_chunks):
    p_ref, pn_ref = refs[0:2]
    prep_prm = refs[2:9]
    scan_prm = refs[9:12]
    o_ref = refs[12]
    buf_a = refs[13:20]
    buf_b = refs[20:27]
    h_ref, po_ref, q_ref, o2_ref, oacc_ref = refs[27:32]
    tm = n_chunks * RWKV_CHUNK
    scan_scratch = (h_ref, po_ref, q_ref, o2_ref, oacc_ref, n_chunks)

    @pl.when(pl.program_id(0) == 0)
    def _():
        h_ref[...] = jnp.zeros_like(h_ref)
        _rwkv_prep_tile(p_ref[0:tm, :], jnp.zeros((1, SHIFT_W), F32), prep_prm, buf_a)

    _rwkv_prep_tile(p_ref[tm:2 * tm, :], p_ref[tm - 1:tm, :], prep_prm, buf_b)
    _rwkv_scan_tile(buf_a, scan_prm, o_ref, slice(0, tm), *scan_scratch)
    _rwkv_prep_tile(pn_ref[...], p_ref[2 * tm - 1:2 * tm, :], prep_prm, buf_a)
    _rwkv_scan_tile(buf_b, scan_prm, o_ref, slice(tm, 2 * tm), *scan_scratch)


def rwkv_mixer(p_b, mu, w0, w2, a0, a2, g2, k_k, k_a, r_k, gn_g, gn_b):
    t_len = p_b.shape[0]
    n_chunks = RWKV_CHUNKS_PER_STEP
    tm = n_chunks * RWKV_CHUNK
    n_tiles = t_len // tm
    n_blocks = B_WIDTH // LANES
    n_chains = n_chunks * n_blocks
    zeros = jnp.zeros((LORA_W, B_WIDTH), F32)
    wa = jnp.concatenate([jnp.concatenate([w2, zeros], 1),
                          jnp.concatenate([zeros, a2], 1)], 0).astype(BF16)
    row = lambda z: z.reshape(1, -1)
    full = lambda shape: pl.BlockSpec(shape, lambda i: (0,) * len(shape))
    vec = full((1, B_WIDTH))
    tile_buf = [pltpu.VMEM((tm, B_WIDTH), F32)] * 7
    return pl.pallas_call(
        functools.partial(_rwkv_kernel, n_chunks=n_chunks),
        grid=(n_tiles // 2,),
        in_specs=[pl.BlockSpec((2 * tm, SHIFT_W), lambda i: (i, 0)),
                  pl.BlockSpec((tm, SHIFT_W), lambda i: (jnp.minimum(2 * i + 2, n_tiles - 1), 0)),
                  full((1, SHIFT_W)), vec, full((LORA_W + LORA_A, 2 * B_WIDTH)), vec,
                  full((LORA_G, B_WIDTH)), vec, vec,
                  vec, vec, vec],
        out_specs=pl.BlockSpec((2 * tm, B_WIDTH), lambda i: (i, 0)),
        out_shape=jax.ShapeDtypeStruct((t_len, B_WIDTH), BF16),
        scratch_shapes=tile_buf + tile_buf
                       + [pltpu.VMEM((n_blocks, LANES, LANES), F32),
                          pltpu.VMEM((n_chains, 2 * LANES, LANES), BF16),
                          pltpu.VMEM((n_chains, LANES, LANES), F32),
                          pltpu.VMEM((n_chains, LANES, LANES), F32),
                          pltpu.VMEM((tm, B_WIDTH), F32)],
        compiler_params=_params(1),
        name="rwkv_mixer",
    )(p_b, p_b, row(mu), row(w0), wa, row(a0), g2.astype(BF16), row(k_k), row(k_a),
      row(r_k), row(gn_g), row(gn_b))


def _attn_kernel(q_ref, kp_ref, kc_ref, vp_ref, vc_ref, o_ref, lse_ref, *, n_qb):
    blk = ATTN_BLOCK
    half = blk // 2
    span = blk + half
    assert half == C_HEAD
    i = pl.program_id(0)
    lane = lax.broadcasted_iota(jnp.int32, (1, LANES), 1)
    head0 = lane < C_HEAD
    key = lax.broadcasted_iota(jnp.int32, (span, LANES), 0)
    qry = lax.broadcasted_iota(jnp.int32, (span, LANES), 1) % half
    dist = qry + blk - key
    in_window = (dist >= 0) & (dist <= blk)
    first_valid = [in_window & ((key >= blk - sub * half) | (i > 0)) for sub in range(2)]
    swap_heads = lambda z: pltpu.roll(z, C_HEAD, 1)
    n_pairs = C_HEADS // HEADS_PER_BLOCK
    units = [(pair, sub) for pair in range(n_pairs) for sub in range(2)]

    for qb in range(n_qb):
        q_rows = slice(qb * blk, (qb + 1) * blk)
        prev_k, prev_v = (kp_ref, vp_ref) if qb == 0 else (kc_ref.at[(qb - 1) * blk:qb * blk],
                                                           vc_ref.at[(qb - 1) * blk:qb * blk])
        cur_k, cur_v = kc_ref.at[q_rows], vc_ref.at[q_rows]
        valid = first_valid if qb == 0 else [in_window, in_window]
        scores = []
        for pair, sub in units:
            cols = slice(pair * LANES, (pair + 1) * LANES)
            qsub = q_ref[qb * blk + sub * half:qb * blk + (sub + 1) * half, cols]
            qsub = qsub * (C_HEAD ** -0.5)
            zero = jnp.zeros_like(qsub)
            qs = jnp.concatenate([jnp.where(head0, qsub, zero), jnp.where(head0, zero, qsub)], axis=0)
            ks = jnp.concatenate([prev_k[sub * half:, cols], cur_k[:(sub + 1) * half, cols]], axis=0)
            scores.append(jnp.where(valid[sub], _dot_nt(ks, qs), NEG_INF))
        probs, lses = [], []
        for st in scores:
            mx = jnp.max(st, axis=0, keepdims=True)
            e = jnp.exp(st - mx)
            den = jnp.sum(e, axis=0, keepdims=True)
            probs.append((e * (1.0 / den)).astype(BF16))
            lses.append(mx + jnp.log(den))
        for (pair, sub), prob in zip(units, probs):
            cols = slice(pair * LANES, (pair + 1) * LANES)
            vs = jnp.concatenate([prev_v[sub * half:, cols], cur_v[:(sub + 1) * half, cols]], axis=0)
            pv = lax.dot_general(prob, vs, (((0,), (0,)), ((), ())),
                                 preferred_element_type=F32)
            o_ref[qb * blk + sub * half:qb * blk + (sub + 1) * half, cols] = (
                jnp.where(head0, pv[:half], pv[half:]).astype(o_ref.dtype))
        lse_rows = []
        for pair in range(n_pairs):
            l0, l1 = lses[2 * pair], lses[2 * pair + 1]
            lse_rows.append(jnp.where(head0, l0, swap_heads(l1)))
            lse_rows.append(jnp.where(head0, swap_heads(l0), l1))
        lse_t = jnp.concatenate(lse_rows + [jnp.zeros((LANES - C_HEADS, LANES), F32)], axis=0)
        lse_ref[q_rows, :] = lse_t.T


def dilated_attention_pattern(view, dil):
    sub = view.shape[0]
    d = D_MODEL
    blk = ATTN_BLOCK
    n_qb = min(ATTN_MAX_BLOCKS_PER_STEP, sub // blk)
    rows = n_qb * blk
    cur = lambda part: pl.BlockSpec((rows, d), lambda i, r: (i, 3 * r + part))
    prv = lambda part: pl.BlockSpec((blk, d),
                                    lambda i, r: (jnp.maximum(n_qb * i - 1, 0), 3 * r + part))
    return pl.pallas_call(
        functools.partial(_attn_kernel, n_qb=n_qb),
        grid=(sub // rows, dil),
        in_specs=[cur(0), prv(1), cur(1), prv(2), cur(2)],
        out_specs=[pl.BlockSpec((rows, d), lambda i, r: (i, r)),
                   pl.BlockSpec((rows, LANES), lambda i, r: (i, r))],
        out_shape=[jax.ShapeDtypeStruct((sub, dil * d), BF16),
                   jax.ShapeDtypeStruct((sub, dil * LANES), F32)],
        compiler_params=_params(2),
        name=f"dilated_attn_d{dil}",
    )(view, view, view, view, view)


def _attn_combine_proj_kernel(x_ref, *refs, dilations):
    n = len(dilations)
    o_refs, l_refs = refs[:n], refs[n:2 * n]
    ex_ref, w_ref, o_ref = refs[2 * n:2 * n + 3]
    l_tmps = refs[2 * n + 3:]
    tm, d = x_ref.shape
    row = lax.broadcasted_iota(jnp.int32, (tm, tm), 0)
    col = lax.broadcasted_iota(jnp.int32, (tm, tm), 1)
    outs, lses = [], []
    for dil, o_v, l_v, l_tmp in zip(dilations, o_refs, l_refs, l_tmps):
        if dil == 1:
            outs.append(o_v[...].astype(F32))
            lses.append(l_v[...])
            continue
        per = tm // dil
        assert dil & (dil - 1) == 0
        stacked = jnp.concatenate([o_v[:, r * d:(r + 1) * d] for r in range(dil)], axis=0)
        src_row = (row & (dil - 1)) * per + (row >> (dil.bit_length() - 1))
        outs.append(_dot((col == src_row).astype(BF16), stacked))
        for r in range(dil):
            l_tmp[pl.ds(r, per, stride=dil), :] = l_v[:, r * LANES:(r + 1) * LANES]
        lses.append(l_tmp[...])
    mx = functools.reduce(jnp.maximum, lses)
    es = [jnp.exp(l - mx) for l in lses]
    den = functools.reduce(lambda a, b: a + b, es)
    y = outs[-1]
    for e, o in zip(es[:-1], outs[:-1]):
        y = y + _exact_rhs(e / den, ex_ref[...]) * (o - outs[-1])
    o_ref[...] = x_ref[...] + _dot(y.astype(BF16), w_ref[...])


def attn_combine_proj(x, outs, lses, w_out, dilations):
    t_len, d = x.shape
    tm = min(ROW_TILE, t_len)
    head_of_lane = jnp.arange(d) // C_HEAD
    expand = (jnp.arange(LANES)[:, None] == head_of_lane[None, :]).astype(BF16)
    wide = pl.BlockSpec((tm, d), lambda i: (i, 0))
    view = lambda width: [pl.BlockSpec((tm // dil, dil * width), lambda i: (i, 0))
                          for dil in dilations]
    return pl.pallas_call(
        functools.partial(_attn_combine_proj_kernel, dilations=dilations),
        grid=(t_len // tm,),
        in_specs=[wide] + view(d) + view(LANES)
                 + [pl.BlockSpec((LANES, d), lambda i: (0, 0)),
                    pl.BlockSpec((d, d), lambda i: (0, 0))],
        out_specs=wide,
        out_shape=jax.ShapeDtypeStruct((t_len, d), F32),
        scratch_shapes=[pltpu.VMEM((tm, LANES), F32) for _ in dilations],
        compiler_params=_params(1),
        name="attn_combine_proj",
    )(x, *outs, *lses, expand, w_out.astype(BF16))


def _ffn_kernel(*refs, final_norm, mixer_proj):
    x_ref = refs[0]
    if mixer_proj:
        ya_ref, yb_ref, wo_ref = refs[1:4]
        refs = refs[:1] + refs[4:]
    (_, g_ref, wu_ref, cw_ref, cb_ref, wd_ref, fn_ref, o_ref, ha_ref, hb_ref, zg0_ref, zg1_ref,
     zv0_ref, zv1_ref, carry_g_ref, carry_v_ref) = refs
    tm = x_ref.shape[0] // 2
    n_j, tf = wd_ref.shape[0], wd_ref.shape[1]
    halo = SUBLANES
    zg_refs = (zg0_ref, zg1_ref)
    zv_refs = (zv0_ref, zv1_ref)
    assert n_j % 2 == 1

    def tile_cols(j, base):
        if isinstance(j, int):
            return pl.ds(base + j * tf, tf)
        return pl.ds(pl.multiple_of(base + j * tf, tf), tf)

    @pl.when(pl.program_id(0) == 0)
    def _():
        carry_g_ref[...] = jnp.zeros_like(carry_g_ref)
        carry_v_ref[...] = jnp.zeros_like(carry_v_ref)

    def start(rows, h_ref):
        x = x_ref[rows, :]
        if mixer_proj:
            x = (x + _dot(ya_ref[rows, :], wo_ref[:A_WIDTH, :])
                 + _dot(yb_ref[rows, :], wo_ref[A_WIDTH:, :]))
        h_ref[...] = _rmsnorm(x, g_ref[...]).astype(BF16)
        o_ref[rows, :] = x

    def up(j, slot, h_ref):
        h = h_ref[...]
        for base, z_ref, carry_ref in ((0, zg_refs[slot], carry_g_ref),
                                       (D_FF, zv_refs[slot], carry_v_ref)):
            z = _dot(h, wu_ref[:, tile_cols(j, base)])
            z_ref[0:halo, :] = carry_ref[j]
            z_ref[halo:halo + tm, :] = z
            carry_ref[j] = z[tm - halo:tm, :]

    def finish(j, slot, rows):
        def conv(z_ref, base):
            z = z_ref[halo:halo + tm, :]
            z1 = z_ref[halo - 1:halo - 1 + tm, :]
            z2 = z_ref[halo - 2:halo - 2 + tm, :]
            cw = cw_ref[:, tile_cols(j, base)]
            return (cw[0:1, :] * z2 + cw[1:2, :] * z1 + cw[2:3, :] * z
                    + cb_ref[:, tile_cols(j, base)])

        gate = conv(zg_refs[slot], 0)
        val = conv(zv_refs[slot], D_FF)
        act = (gate * jax.nn.sigmoid(gate) * val).astype(BF16)
        o_ref[rows, :] += _dot(act, wd_ref[j])

    def middle(rows, h_ref, first_slot):
        def body(pair, carry):
            j = 2 * pair
            up(j + 1, 1 - first_slot, h_ref)
            finish(j, first_slot, rows)
            up(j + 2, first_slot, h_ref)
            finish(j + 1, 1 - first_slot, rows)
            return carry
        lax.fori_loop(0, (n_j - 1) // 2, body, 0)

    rows_a, rows_b = slice(0, tm), slice(tm, 2 * tm)
    start(rows_a, ha_ref)
    up(0, 0, ha_ref)
    middle(rows_a, ha_ref, 0)
    start(rows_b, hb_ref)
    up(0, 1, hb_ref)
    finish(n_j - 1, 0, rows_a)
    middle(rows_b, hb_ref, 1)
    finish(n_j - 1, 1, rows_b)

    if final_norm:
        o_ref[...] = _rmsnorm(o_ref[...], fn_ref[...])


def conv_glu_ffn(x, norm_g, w_up, conv_w, conv_b, w_down, final_g, final_norm, mixer=None):
    t_len, d = x.shape
    tm = min(FFN_ROW_TILE, t_len // 2)
    tf = FFN_COL_TILE
    n_j = D_FF // tf
    full = lambda shape: pl.BlockSpec(shape, lambda i: (0,) * len(shape),
                                      pipeline_mode=pl.Buffered(1))
    mixer_specs, mixer_args = [], []
    if mixer is not None:
        y_a, y_b, w_out = mixer
        half = pl.BlockSpec((2 * tm, A_WIDTH), lambda i: (i, 0))
        mixer_specs = [half, half, full((d, d))]
        mixer_args = [y_a, y_b, w_out.astype(BF16)]
    return pl.pallas_call(
        functools.partial(_ffn_kernel, final_norm=final_norm, mixer_proj=mixer is not None),
        grid=(t_len // (2 * tm),),
        in_specs=[pl.BlockSpec((2 * tm, d), lambda i: (i, 0))] + mixer_specs
                 + [full((1, d)),
                    full((d, 2 * D_FF)),
                    full((3, 2 * D_FF)),
                    full((1, 2 * D_FF)),
                    full((n_j, tf, d)),
                    full((1, d))],
        out_specs=pl.BlockSpec((2 * tm, d), lambda i: (i, 0)),
        out_shape=jax.ShapeDtypeStruct((t_len, d), F32),
        scratch_shapes=[pltpu.VMEM((tm, d), BF16)] * 2
                       + [pltpu.VMEM((tm + SUBLANES, tf), F32)] * 4
                       + [pltpu.VMEM((n_j, SUBLANES, tf), F32),
                        pltpu.VMEM((n_j, SUBLANES, tf), F32)],
        compiler_params=_params(1),
        name="conv_glu_ffn",
    )(x, *mixer_args, norm_g.reshape(1, d), w_up.astype(BF16), conv_w,
      conv_b.reshape(1, 2 * D_FF), w_down.astype(BF16).reshape(n_j, tf, d), final_g.reshape(1, d))


def even_mixer(x, norm, w_in, ln_g, ln_b, w_s, b_s, mu, w0, w2, a0, a2, g2, k_k, k_a, r_k, gn_g,
               gn_b):
    y_a, p_b = even_in(x, norm, w_in, ln_g, ln_b, w_s, b_s)
    y_b = rwkv_mixer(p_b, mu, w0, w2, a0, a2, g2, k_k, k_a, r_k, gn_g, gn_b)
    return y_a, y_b


def odd_mixer(x, norm, w_qkv, w_out):
    dilations = tuple(dil for _, dil in C_PATTERNS)
    views = norm_matmul_views(x, norm, w_qkv.astype(BF16), dilations)
    outs, lses = [], []
    for view, dil in zip(views, dilations):
        o, l = dilated_attention_pattern(view, dil)
        outs.append(o)
        lses.append(l)
    return attn_combine_proj(x, outs, lses, w_out, dilations)


def kernel(x, ev_norm, ev_w_in, ev_ln_g, ev_ln_b, ev_w_s, ev_b_s, ev_mu, ev_w0, ev_w2, ev_a0, ev_a2, ev_g2, ev_k_k, ev_k_a, ev_r_k, ev_gn_g, ev_gn_b, ev_w_out, od_norm, od_w_qkv, od_w_out, ff_norm, ff_w_up, ff_conv_w, ff_conv_b, ff_w_down, final_norm):
    bsz, t_len, d = x.shape
    depth = ff_norm.shape[0]
    outs = []
    for bi in range(bsz):
        xb = x[bi]
        for layer in range(depth):
            j = layer // 2
            mixer = None
            if layer % 2 == 0:
                y_a, y_b = even_mixer(xb, ev_norm[j], ev_w_in[j], ev_ln_g[j], ev_ln_b[j],
                                      ev_w_s[j], ev_b_s[j], ev_mu[j], ev_w0[j], ev_w2[j], ev_a0[j],
                                      ev_a2[j], ev_g2[j], ev_k_k[j], ev_k_a[j], ev_r_k[j],
                                      ev_gn_g[j], ev_gn_b[j])
                mixer = (y_a, y_b, ev_w_out[j])
            else:
                xb = odd_mixer(xb, od_norm[j], od_w_qkv[j], od_w_out[j])
            xb = conv_glu_ffn(xb, ff_norm[layer], ff_w_up[layer], ff_conv_w[layer],
                              ff_conv_b[layer], ff_w_down[layer], final_norm,
                              final_norm=(layer == depth - 1), mixer=mixer)
        outs.append(xb)
    return jnp.stack(outs, axis=0)
```

```python
import functools

import jax
import jax.numpy as jnp
from jax import lax
from jax.experimental import pallas as pl
from jax.experimental.pallas import tpu as pltpu

F32 = jnp.float32
BF16 = jnp.bfloat16

D_MODEL = 1024
A_WIDTH = 512
A_GROUP = 128
A_GROUPS = 4
A_CHUNK = 128
LN_EPS = 1e-5
B_WIDTH = 512
B_HEAD = 64
LORA_W = 64
LORA_A = 64
LORA_G = 128
GN_EPS = 64e-5
SHIFT_W = 3 * B_WIDTH + LORA_W + LORA_A + LORA_G
C_HEAD = 64
C_HEADS = 16
C_PATTERNS = ((128, 1), (512, 4), (2048, 16))
NEG_INF = -1e30
D_FF = 2816
RMS_EPS = 1e-6

LANES = 128
SUBLANES = 8
HEADS_PER_BLOCK = LANES // B_HEAD
VMEM_LIMIT_BYTES = 52 * 1024 * 1024

ROW_TILE = 512
QKV_ROW_TILE = 256
FFN_ROW_TILE = 512
FFN_COL_TILE = 256
RWKV_CHUNK = 64
RWKV_CHUNKS_PER_STEP = 4
ATTN_BLOCK = 128
ATTN_MAX_BLOCKS_PER_STEP = 8
NEUMANN_LEVELS = 5


def _params(n_axes):
    return pltpu.CompilerParams(dimension_semantics=("arbitrary",) * n_axes,
                                vmem_limit_bytes=VMEM_LIMIT_BYTES)


def _rmsnorm(x, g):
    return x * lax.rsqrt(jnp.mean(x * x, axis=-1, keepdims=True) + RMS_EPS) * g


def _dot(a, b):
    return jnp.dot(a, b, preferred_element_type=F32)


def _dot_nt(a, b):
    return lax.dot_general(a, b, (((1,), (1,)), ((), ())), preferred_element_type=F32)


def _split3(x):
    hi = x.astype(BF16)
    r1 = x - hi.astype(F32)
    mid = r1.astype(BF16)
    lo = (r1 - mid.astype(F32)).astype(BF16)
    return hi, mid, lo


def _exact_rhs(x, m_bf16):
    hi, mid, lo = _split3(x)
    return _dot(hi, m_bf16) + (_dot(mid, m_bf16) + _dot(lo, m_bf16))


def _exact_lhs(m_bf16, x):
    hi, mid, lo = _split3(x)
    return _dot(m_bf16, hi) + (_dot(m_bf16, mid) + _dot(m_bf16, lo))


def _norm_matmul_views_kernel(x_ref, g_ref, w_ref, *o_refs, dilations):
    tm = x_ref.shape[0]
    n = w_ref.shape[1]
    h = _rmsnorm(x_ref[...], g_ref[...]).astype(BF16)
    res = _dot(h, w_ref[...]).astype(BF16)
    out_row = lax.broadcasted_iota(jnp.int32, (tm, tm), 0)
    in_row = lax.broadcasted_iota(jnp.int32, (tm, tm), 1)
    for dil, o_ref in zip(dilations, o_refs):
        if dil == 1:
            o_ref[...] = res
            continue
        per = tm // dil
        assert per & (per - 1) == 0
        shift = per.bit_length() - 1
        src_row = (out_row & (per - 1)) * dil + (out_row >> shift)
        grouped = _dot((in_row == src_row).astype(BF16), res).astype(BF16)
        for r in range(dil):
            o_ref[:, r * n:(r + 1) * n] = grouped[r * per:(r + 1) * per, :]


def norm_matmul_views(x, g, w, dilations):
    t_len, d = x.shape
    n = w.shape[1]
    tm = min(QKV_ROW_TILE, t_len)
    return pl.pallas_call(
        functools.partial(_norm_matmul_views_kernel, dilations=dilations),
        grid=(t_len // tm,),
        in_specs=[pl.BlockSpec((tm, d), lambda i: (i, 0)),
                  pl.BlockSpec((1, d), lambda i: (0, 0)),
                  pl.BlockSpec((d, n), lambda i: (0, 0))],
        out_specs=[pl.BlockSpec((tm // dil, dil * n), lambda i: (i, 0)) for dil in dilations],
        out_shape=[jax.ShapeDtypeStruct((t_len // dil, dil * n), BF16) for dil in dilations],
        compiler_params=_params(1),
        name="norm_matmul_views",
    )(x, g.reshape(1, d), w)


def _even_in_kernel(x_ref, g_ref, w_ref, lng_ref, lnb_ref, ws_ref, bs_ref, ya_ref, pb_ref):
    tm = x_ref.shape[0]
    h = _rmsnorm(x_ref[...], g_ref[...]).astype(BF16)
    p = _dot(h, w_ref[...])
    pb_ref[...] = p[:, 2 * A_WIDTH:]

    row = lax.broadcasted_iota(jnp.int32, (A_CHUNK, A_CHUNK), 0)
    col = lax.broadcasted_iota(jnp.int32, (A_CHUNK, A_CHUNK), 1)
    causal = col <= row
    w_masked = [jnp.where(causal, ws_ref[g], 0.0).astype(BF16) for g in range(A_GROUPS)]
    for c in range(tm // A_CHUNK):
        rows = slice(c * A_CHUNK, (c + 1) * A_CHUNK)
        u = p[rows, :A_WIDTH]
        v = p[rows, A_WIDTH:2 * A_WIDTH]
        mean = jnp.mean(v, axis=-1, keepdims=True)
        cen = v - mean
        var = jnp.mean(cen * cen, axis=-1, keepdims=True)
        vn = (cen * lax.rsqrt(var + LN_EPS) * lng_ref[...] + lnb_ref[...]).astype(BF16)
        for g in range(A_GROUPS):
            cols = slice(g * A_GROUP, (g + 1) * A_GROUP)
            mixed = _dot(w_masked[g], vn[:, cols]) + bs_ref[:, g:g + 1]
            ya_ref[rows, cols] = (u[:, cols] * mixed).astype(ya_ref.dtype)


def even_in(x, g, w_in, ln_g, ln_b, w_s, b_s):
    t_len, d = x.shape
    n = w_in.shape[1]
    tm = min(ROW_TILE, t_len)
    full = lambda shape: pl.BlockSpec(shape, lambda i: (0,) * len(shape))
    return pl.pallas_call(
        _even_in_kernel,
        grid=(t_len // tm,),
        in_specs=[pl.BlockSpec((tm, d), lambda i: (i, 0)),
                  full((1, d)), full((d, n)), full((1, A_WIDTH)), full((1, A_WIDTH)),
                  full((A_GROUPS, A_CHUNK, A_CHUNK)), full((A_CHUNK, A_GROUPS))],
        out_specs=[pl.BlockSpec((tm, A_WIDTH), lambda i: (i, 0)),
                   pl.BlockSpec((tm, SHIFT_W), lambda i: (i, 0))],
        out_shape=[jax.ShapeDtypeStruct((t_len, A_WIDTH), BF16),
                   jax.ShapeDtypeStruct((t_len, SHIFT_W), F32)],
        compiler_params=_params(1),
        name="even_in",
    )(x, g.reshape(1, d), w_in.astype(BF16), ln_g.reshape(1, A_WIDTH), ln_b.reshape(1, A_WIDTH),
      w_s, b_s.T)


def _head_sums(x):
    lane = lax.broadcasted_iota(jnp.int32, (1, LANES), 1)
    head0 = lane < B_HEAD
    outs = []
    for blk in range(x.shape[1] // LANES):
        xb = x[:, blk * LANES:(blk + 1) * LANES]
        s0 = jnp.sum(jnp.where(head0, xb, 0.0), axis=-1, keepdims=True)
        s1 = jnp.sum(jnp.where(head0, 0.0, xb), axis=-1, keepdims=True)
        outs.append(jnp.where(head0, s0, s1))
    return jnp.concatenate(outs, axis=-1)


def _rwkv_prep_tile(p, prev_row, prm, buf):
    mu_ref, w0_ref, wa_ref, a0_ref, g2_ref, kkw_ref, kaw_ref = prm
    r_o, k_o, v_o, kk_o, b_o, lw_o, g_o = buf
    tm = p.shape[0]
    rowid = lax.broadcasted_iota(jnp.int32, (tm, 1), 0)
    prev = jnp.where(rowid == 0, prev_row, pltpu.roll(p, 1, 0))
    ps = p + (prev - p) * mu_ref[...]

    r = ps[:, :B_WIDTH]
    k = ps[:, B_WIDTH:2 * B_WIDTH]
    v = ps[:, 2 * B_WIDTH:3 * B_WIDTH]
    lwa = ps[:, 3 * B_WIDTH:3 * B_WIDTH + LORA_W + LORA_A]
    lg = ps[:, 3 * B_WIDTH + LORA_W + LORA_A:]

    lane = lax.broadcasted_iota(jnp.int32, (1, LORA_W + LORA_A), 1)
    lora_in = jnp.where(lane < LORA_W, jnp.tanh(lwa), lwa).astype(BF16)
    lora = _dot(lora_in, wa_ref[...])
    neg = -(w0_ref[...] + lora[:, :B_WIDTH])
    softplus = jnp.maximum(neg, 0.0) + jnp.log(1.0 + jnp.exp(-jnp.abs(neg)))
    w = -softplus - 0.5
    a = jax.nn.sigmoid(a0_ref[...] + lora[:, B_WIDTH:])
    g = _dot(jax.nn.sigmoid(lg).astype(BF16), g2_ref[...])

    kkr = k * kkw_ref[...]
    kk = kkr * lax.rsqrt(jnp.maximum(_head_sums(kkr * kkr), 1e-24))

    r_o[...] = r
    k_o[...] = k * (1.0 + (a - 1.0) * kaw_ref[...])
    v_o[...] = v
    kk_o[...] = kk
    b_o[...] = kk * a
    lw_o[...] = -jnp.exp(w)
    g_o[...] = g


def _rwkv_scan_tile(buf, prm, o_ref, out_rows, h_ref, po_ref, q_ref, o2_ref, oacc_ref, n_chunks):
    r_ref, k_ref, v_ref, kk_ref, b_ref, lw_ref, g_ref = buf
    rk_ref, gng_ref, gnb_ref = prm
    c_len = RWKV_CHUNK
    n_blocks = B_WIDTH // LANES

    lane = lax.broadcasted_iota(jnp.int32, (1, LANES), 1)
    head0 = lane < B_HEAD
    rho = lax.broadcasted_iota(jnp.int32, (LANES, LANES), 0)
    sig = lax.broadcasted_iota(jnp.int32, (LANES, LANES), 1)
    same_head = (rho // c_len) == (sig // c_len)
    strict = same_head & ((sig % c_len) < (rho % c_len))
    incl = same_head & ((sig % c_len) <= (rho % c_len))
    eye = rho == sig
    tm = n_chunks * c_len
    tri_r = lax.broadcasted_iota(jnp.int32, (tm, tm), 0)
    tri_c = lax.broadcasted_iota(jnp.int32, (tm, tm), 1)
    cum_mat = ((tri_r // c_len == tri_c // c_len) & (tri_c <= tri_r)).astype(BF16)

    def stack(x):
        return jnp.concatenate([jnp.where(head0, x, 0.0), jnp.where(head0, 0.0, x)], axis=0)

    lw_all = lw_ref[...]
    cum_all = _exact_lhs(cum_mat, lw_all)
    chains = []
    for c in range(n_chunks):
        rows = slice(c * c_len, (c + 1) * c_len)
        cum = cum_all[rows, :]
        total = cum[c_len - 1:c_len, :]
        grow = jnp.exp(cum)
        inv = jnp.exp(-cum)
        to_end = jnp.exp(total - cum)
        decay = jnp.exp(total)
        r_t = r_ref[rows, :] * grow
        kk_t = kk_ref[rows, :] * jnp.exp(cum - lw_all[rows, :])
        b_t = b_ref[rows, :] * inv
        k_t = k_ref[rows, :] * inv
        b_e = b_ref[rows, :] * to_end
        k_e = k_ref[rows, :] * to_end
        v = v_ref[rows, :]
        for blk in range(n_blocks):
            cols = slice(blk * LANES, (blk + 1) * LANES)
            kk_s = stack(kk_t[:, cols]).astype(BF16)
            r_s = stack(r_t[:, cols])
            a_all = _dot_nt(
                jnp.concatenate([kk_s, r_s.astype(BF16)], axis=0),
                jnp.concatenate([stack(b_t[:, cols]), stack(k_t[:, cols])], axis=0).astype(BF16))
            neg_l = -jnp.where(strict, a_all[:LANES, :LANES], 0.0)
            v_lhs = jnp.concatenate(
                [stack(k_e[:, cols]).T,
                 jnp.where(incl, a_all[LANES:, LANES:], 0.0),
                 jnp.where(strict, a_all[:LANES, LANES:], 0.0)], axis=0).astype(BF16)
            w_lhs = jnp.concatenate(
                [stack(b_e[:, cols]).T, jnp.where(incl, a_all[LANES:, :LANES], 0.0)],
                axis=0).astype(BF16)
            chains.append(dict(kk_s=kk_s, r_s=r_s, v_s=stack(v[:, cols]).astype(BF16),
                               v_lhs=v_lhs, w_lhs=w_lhs, decay=decay[:, cols],
                               power=neg_l.astype(BF16),
                               t_inv=jnp.where(eye, 1.0, 0.0) + neg_l))

    for ch in chains:
        ch["power"] = _dot(ch["power"], ch["power"]).astype(BF16)
    for level in range(1, NEUMANN_LEVELS):
        for ch in chains:
            both = _dot(jnp.concatenate([ch["power"], ch["t_inv"].astype(BF16)], axis=0),
                        ch["power"])
            ch["power"] = both[:LANES].astype(BF16)
            ch["t_inv"] = ch["t_inv"] + both[LANES:]
    for ch in chains:
        ch["t_inv"] = (ch["t_inv"] + _dot(ch["t_inv"].astype(BF16), ch["power"])).astype(BF16)

    for ch in chains:
        ch["xv"] = _dot(ch["v_lhs"], ch["v_s"])
    for ch in chains:
        rhs = jnp.concatenate([ch["kk_s"], ch["xv"][2 * LANES:].astype(BF16)], axis=1)
        ch["w12"] = _dot(ch["t_inv"], rhs).astype(BF16)
    for idx, ch in enumerate(chains):
        xw = _dot(ch["w_lhs"], ch["w12"])
        p_mat = jnp.where(eye, ch["decay"], 0.0) - xw[:LANES, :LANES]
        o1 = ch["r_s"] - xw[LANES:, :LANES]
        po_ref[idx] = jnp.concatenate([p_mat, o1], axis=0).astype(BF16)
        q_ref[idx] = ch["xv"][:LANES] - xw[:LANES, LANES:]
        o2_ref[idx] = ch["xv"][LANES:2 * LANES] - xw[LANES:, LANES:]

    states = [h_ref[blk] for blk in range(n_blocks)]
    for c in range(n_chunks):
        rows = slice(c * c_len, (c + 1) * c_len)
        pos = [_dot(po_ref[c * n_blocks + blk], states[blk].astype(BF16))
               for blk in range(n_blocks)]
        for blk in range(n_blocks):
            chain = c * n_blocks + blk
            states[blk] = pos[blk][:LANES] + q_ref[chain]
            o_st = pos[blk][LANES:] + o2_ref[chain]
            oacc_ref[rows, blk * LANES:(blk + 1) * LANES] = o_st[:c_len] + o_st[c_len:]
    for blk in range(n_blocks):
        h_ref[blk] = states[blk]

    o = oacc_ref[...]
    mean = _head_sums(o) * (1.0 / B_HEAD)
    cen = o - mean
    var = _head_sums(cen * cen) * (1.0 / B_HEAD)
    normed = cen * lax.rsqrt(var + GN_EPS) * gng_ref[...] + gnb_ref[...]
    bonus = _head_sums(r_ref[...] * k_ref[...] * rk_ref[...]) * v_ref[...]
    o_ref[out_rows, :] = ((normed + bonus) * g_ref[...]).astype(o_ref.dtype)


def _rwkv_kernel(*refs, n_chunks):
    p_ref, pn_ref = refs[0:2]
    prep_prm = refs[2:9]
    scan_prm = refs[9:12]
    o_ref = refs[12]
    buf_a = refs[13:20]
    buf_b = refs[20:27]
    h_ref, po_ref, q_ref, o2_ref, oacc_ref = refs[27:32]
    tm = n_chunks * RWKV_CHUNK
    scan_scratch = (h_ref, po_ref, q_ref, o2_ref, oacc_ref, n_chunks)

    @pl.when(pl.program_id(0) == 0)
    def _():
        h_ref[...] = jnp.zeros_like(h_ref)
        _rwkv_prep_tile(p_ref[0:tm, :], jnp.zeros((1, SHIFT_W), F32), prep_prm, buf_a)

    _rwkv_prep_tile(p_ref[tm:2 * tm, :], p_ref[tm - 1:tm, :], prep_prm, buf_b)
    _rwkv_scan_tile(buf_a, scan_prm, o_ref, slice(0, tm), *scan_scratch)
    _rwkv_prep_tile(pn_ref[...], p_ref[2 * tm - 1:2 * tm, :], prep_prm, buf_a)
    _rwkv_scan_tile(buf_b, scan_prm, o_ref, slice(tm, 2 * tm), *scan_scratch)


def rwkv_mixer(p_b, mu, w0, w2, a0, a2, g2, k_k, k_a, r_k, gn_g, gn_b):
    t_len = p_b.shape[0]
    n_chunks = RWKV_CHUNKS_PER_STEP
    tm = n_chunks * RWKV_CHUNK
    n_tiles = t_len // tm
    n_blocks = B_WIDTH // LANES
    n_chains = n_chunks * n_blocks
    zeros = jnp.zeros((LORA_W, B_WIDTH), F32)
    wa = jnp.concatenate([jnp.concatenate([w2, zeros], 1),
                          jnp.concatenate([zeros, a2], 1)], 0).astype(BF16)
    row = lambda z: z.reshape(1, -1)
    full = lambda shape: pl.BlockSpec(shape, lambda i: (0,) * len(shape))
    vec = full((1, B_WIDTH))
    tile_buf = [pltpu.VMEM((tm, B_WIDTH), F32)] * 7
    return pl.pallas_call(
        functools.partial(_rwkv_kernel, n_chunks=n_chunks),
        grid=(n_tiles // 2,),
        in_specs=[pl.BlockSpec((2 * tm, SHIFT_W), lambda i: (i, 0)),
                  pl.BlockSpec((tm, SHIFT_W), lambda i: (jnp.minimum(2 * i + 2, n_tiles - 1), 0)),
                  full((1, SHIFT_W)), vec, full((LORA_W + LORA_A, 2 * B_WIDTH)), vec,
                  full((LORA_G, B_WIDTH)), vec, vec,
                  vec, vec, vec],
        out_specs=pl.BlockSpec((2 * tm, B_WIDTH), lambda i: (i, 0)),
        out_shape=jax.ShapeDtypeStruct((t_len, B_WIDTH), BF16),
        scratch_shapes=tile_buf + tile_buf
                       + [pltpu.VMEM((n_blocks, LANES, LANES), F32),
                          pltpu.VMEM((n_chains, 2 * LANES, LANES), BF16),
                          pltpu.VMEM((n_chains, LANES, LANES), F32),
                          pltpu.VMEM((n_chains, LANES, LANES), F32),
                          pltpu.VMEM((tm, B_WIDTH), F32)],
        compiler_params=_params(1),
        name="rwkv_mixer",
    )(p_b, p_b, row(mu), row(w0), wa, row(a0), g2.astype(BF16), row(k_k), row(k_a),
      row(r_k), row(gn_g), row(gn_b))


def _attn_kernel(q_ref, kp_ref, kc_ref, vp_ref, vc_ref, o_ref, lse_ref, *, n_qb):
    blk = ATTN_BLOCK
    half = blk // 2
    span = blk + half
    assert half == C_HEAD
    i = pl.program_id(0)
    lane = lax.broadcasted_iota(jnp.int32, (1, LANES), 1)
    head0 = lane < C_HEAD
    key = lax.broadcasted_iota(jnp.int32, (span, LANES), 0)
    qry = lax.broadcasted_iota(jnp.int32, (span, LANES), 1) % half
    dist = qry + blk - key
    in_window = (dist >= 0) & (dist <= blk)
    first_valid = [in_window & ((key >= blk - sub * half) | (i > 0)) for sub in range(2)]
    swap_heads = lambda z: pltpu.roll(z, C_HEAD, 1)
    n_pairs = C_HEADS // HEADS_PER_BLOCK
    units = [(pair, sub) for pair in range(n_pairs) for sub in range(2)]

    for qb in range(n_qb):
        q_rows = slice(qb * blk, (qb + 1) * blk)
        prev_k, prev_v = (kp_ref, vp_ref) if qb == 0 else (kc_ref.at[(qb - 1) * blk:qb * blk],
                                                           vc_ref.at[(qb - 1) * blk:qb * blk])
        cur_k, cur_v = kc_ref.at[q_rows], vc_ref.at[q_rows]
        valid = first_valid if qb == 0 else [in_window, in_window]
        scores = []
        for pair, sub in units:
            cols = slice(pair * LANES, (pair + 1) * LANES)
            qsub = q_ref[qb * blk + sub * half:qb * blk + (sub + 1) * half, cols]
            qsub = qsub * (C_HEAD ** -0.5)
            zero = jnp.zeros_like(qsub)
            qs = jnp.concatenate([jnp.where(head0, qsub, zero), jnp.where(head0, zero, qsub)], axis=0)
            ks = jnp.concatenate([prev_k[sub * half:, cols], cur_k[:(sub + 1) * half, cols]], axis=0)
            scores.append(jnp.where(valid[sub], _dot_nt(ks, qs), NEG_INF))
        probs, lses = [], []
        for st in scores:
            mx = jnp.max(st, axis=0, keepdims=True)
            e = jnp.exp(st - mx)
            den = jnp.sum(e, axis=0, keepdims=True)
            probs.append((e * (1.0 / den)).astype(BF16))
            lses.append(mx + jnp.log(den))
        for (pair, sub), prob in zip(units, probs):
            cols = slice(pair * LANES, (pair + 1) * LANES)
            vs = jnp.concatenate([prev_v[sub * half:, cols], cur_v[:(sub + 1) * half, cols]], axis=0)
            pv = lax.dot_general(prob, vs, (((0,), (0,)), ((), ())),
                                 preferred_element_type=F32)
            o_ref[qb * blk + sub * half:qb * blk + (sub + 1) * half, cols] = (
                jnp.where(head0, pv[:half], pv[half:]).astype(o_ref.dtype))
        lse_rows = []
        for pair in range(n_pairs):
            l0, l1 = lses[2 * pair], lses[2 * pair + 1]
            lse_rows.append(jnp.where(head0, l0, swap_heads(l1)))
            lse_rows.append(jnp.where(head0, swap_heads(l0), l1))
        lse_t = jnp.concatenate(lse_rows + [jnp.zeros((LANES - C_HEADS, LANES), F32)], axis=0)
        lse_ref[q_rows, :] = lse_t.T


def dilated_attention_pattern(view, dil):
    sub = view.shape[0]
    d = D_MODEL
    blk = ATTN_BLOCK
    n_qb = min(ATTN_MAX_BLOCKS_PER_STEP, sub // blk)
    rows = n_qb * blk
    assert sub % rows == 0
    cur = lambda part: pl.BlockSpec((rows, d), lambda i, r: (i, 3 * r + part))
    prv = lambda part: pl.BlockSpec((blk, d),
                                    lambda i, r: (jnp.maximum(n_qb * i - 1, 0), 3 * r + part))
    return pl.pallas_call(
        functools.partial(_attn_kernel, n_qb=n_qb),
        grid=(sub // rows, dil),
        in_specs=[cur(0), prv(1), cur(1), prv(2), cur(2)],
        out_specs=[pl.BlockSpec((rows, d), lambda i, r: (i, r)),
                   pl.BlockSpec((rows, LANES), lambda i, r: (i, r))],
        out_shape=[jax.ShapeDtypeStruct((sub, dil * d), BF16),
                   jax.ShapeDtypeStruct((sub, dil * LANES), F32)],
        compiler_params=_params(2),
        name=f"dilated_attn_d{dil}",
    )(view, view, view, view, view)


def _attn_combine_proj_kernel(x_ref, *refs, dilations):
    n = len(dilations)
    o_refs, l_refs = refs[:n], refs[n:2 * n]
    ex_ref, w_ref, o_ref = refs[2 * n:2 * n + 3]
    l_tmps = refs[2 * n + 3:]
    tm, d = x_ref.shape
    row = lax.broadcasted_iota(jnp.int32, (tm, tm), 0)
    col = lax.broadcasted_iota(jnp.int32, (tm, tm), 1)
    outs, lses = [], []
    for dil, o_v, l_v, l_tmp in zip(dilations, o_refs, l_refs, l_tmps):
        if dil == 1:
            outs.append(o_v[...].astype(F32))
            lses.append(l_v[...])
            continue
        per = tm // dil
        assert dil & (dil - 1) == 0
        stacked = jnp.concatenate([o_v[:, r * d:(r + 1) * d] for r in range(dil)], axis=0)
        src_row = (row & (dil - 1)) * per + (row >> (dil.bit_length() - 1))
        outs.append(_dot((col == src_row).astype(BF16), stacked))
        for r in range(dil):
            l_tmp[pl.ds(r, per, stride=dil), :] = l_v[:, r * LANES:(r + 1) * LANES]
        lses.append(l_tmp[...])
    mx = functools.reduce(jnp.maximum, lses)
    es = [jnp.exp(l - mx) for l in lses]
    den = functools.reduce(lambda a, b: a + b, es)
    y = outs[-1]
    for e, o in zip(es[:-1], outs[:-1]):
        y = y + _exact_rhs(e / den, ex_ref[...]) * (o - outs[-1])
    o_ref[...] = x_ref[...] + _dot(y.astype(BF16), w_ref[...])


def attn_combine_proj(x, outs, lses, w_out, dilations):
    t_len, d = x.shape
    tm = min(ROW_TILE, t_len)
    head_of_lane = jnp.arange(d) // C_HEAD
    expand = (jnp.arange(LANES)[:, None] == head_of_lane[None, :]).astype(BF16)
    wide = pl.BlockSpec((tm, d), lambda i: (i, 0))
    view = lambda width: [pl.BlockSpec((tm // dil, dil * width), lambda i: (i, 0))
                          for dil in dilations]
    return pl.pallas_call(
        functools.partial(_attn_combine_proj_kernel, dilations=dilations),
        grid=(t_len // tm,),
        in_specs=[wide] + view(d) + view(LANES)
                 + [pl.BlockSpec((LANES, d), lambda i: (0, 0)),
                    pl.BlockSpec((d, d), lambda i: (0, 0))],
        out_specs=wide,
        out_shape=jax.ShapeDtypeStruct((t_len, d), F32),
        scratch_shapes=[pltpu.VMEM((tm, LANES), F32) for _ in dilations],
        compiler_params=_params(1),
        name="attn_combine_proj",
    )(x, *outs, *lses, expand, w_out.astype(BF16))


def _ffn_kernel(*refs, final_norm, mixer_proj):
    x_ref = refs[0]
    if mixer_proj:
        ya_ref, yb_ref, wo_ref = refs[1:4]
        refs = refs[:1] + refs[4:]
    (_, g_ref, wu_ref, cw_ref, cb_ref, wd_ref, fn_ref, o_ref, ha_ref, hb_ref, zg0_ref, zg1_ref,
     zv0_ref, zv1_ref, carry_g_ref, carry_v_ref) = refs
    tm = x_ref.shape[0] // 2
    n_j, tf = wd_ref.shape[0], wd_ref.shape[1]
    halo = SUBLANES
    zg_refs = (zg0_ref, zg1_ref)
    zv_refs = (zv0_ref, zv1_ref)
    assert n_j % 2 == 1

    def tile_cols(j, base):
        if isinstance(j, int):
            return pl.ds(base + j * tf, tf)
        return pl.ds(pl.multiple_of(base + j * tf, tf), tf)

    @pl.when(pl.program_id(0) == 0)
    def _():
        carry_g_ref[...] = jnp.zeros_like(carry_g_ref)
        carry_v_ref[...] = jnp.zeros_like(carry_v_ref)

    def start(rows, h_ref):
        x = x_ref[rows, :]
        if mixer_proj:
            x = (x + _dot(ya_ref[rows, :], wo_ref[:A_WIDTH, :])
                 + _dot(yb_ref[rows, :], wo_ref[A_WIDTH:, :]))
        h_ref[...] = _rmsnorm(x, g_ref[...]).astype(BF16)
        o_ref[rows, :] = x

    def up(j, slot, h_ref):
        h = h_ref[...]
        for base, z_ref, carry_ref in ((0, zg_refs[slot], carry_g_ref),
                                       (D_FF, zv_refs[slot], carry_v_ref)):
            z = _dot(h, wu_ref[:, tile_cols(j, base)])
            z_ref[0:halo, :] = carry_ref[j]
            z_ref[halo:halo + tm, :] = z
            carry_ref[j] = z[tm - halo:tm, :]

    def finish(j, slot, rows):
        def conv(z_ref, base):
            z = z_ref[halo:halo + tm, :]
            z1 = z_ref[halo - 1:halo - 1 + tm, :]
            z2 = z_ref[halo - 2:halo - 2 + tm, :]
            cw = cw_ref[:, tile_cols(j, base)]
            return (cw[0:1, :] * z2 + cw[1:2, :] * z1 + cw[2:3, :] * z
                    + cb_ref[:, tile_cols(j, base)])

        gate = conv(zg_refs[slot], 0)
        val = conv(zv_refs[slot], D_FF)
        act = (gate * jax.nn.sigmoid(gate) * val).astype(BF16)
        o_ref[rows, :] += _dot(act, wd_ref[j])

    def middle(rows, h_ref, first_slot):
        def body(pair, carry):
            j = 2 * pair
            up(j + 1, 1 - first_slot, h_ref)
            finish(j, first_slot, rows)
            up(j + 2, first_slot, h_ref)
            finish(j + 1, 1 - first_slot, rows)
            return carry
        lax.fori_loop(0, (n_j - 1) // 2, body, 0)

    rows_a, rows_b = slice(0, tm), slice(tm, 2 * tm)
    start(rows_a, ha_ref)
    up(0, 0, ha_ref)
    middle(rows_a, ha_ref, 0)
    start(rows_b, hb_ref)
    up(0, 1, hb_ref)
    finish(n_j - 1, 0, rows_a)
    middle(rows_b, hb_ref, 1)
    finish(n_j - 1, 1, rows_b)

    if final_norm:
        o_ref[...] = _rmsnorm(o_ref[...], fn_ref[...])


def conv_glu_ffn(x, layer, norm_g, w_up, conv_w, conv_b, w_down, final_g, final_norm, mixer=None):
    t_len, d = x.shape
    depth = w_up.shape[0]
    tm = min(FFN_ROW_TILE, t_len // 2)
    tf = FFN_COL_TILE
    n_j = D_FF // tf
    full = lambda shape: pl.BlockSpec(shape, lambda i: (0,) * len(shape),
                                      pipeline_mode=pl.Buffered(1))
    of_layer = lambda shape: pl.BlockSpec((None,) + shape, lambda i: (layer,) + (0,) * len(shape),
                                          pipeline_mode=pl.Buffered(1))
    mixer_specs, mixer_args = [], []
    if mixer is not None:
        y_a, y_b, w_out = mixer
        half = pl.BlockSpec((2 * tm, A_WIDTH), lambda i: (i, 0))
        mixer_specs = [half, half, full((d, d))]
        mixer_args = [y_a, y_b, w_out.astype(BF16)]
    return pl.pallas_call(
        functools.partial(_ffn_kernel, final_norm=final_norm, mixer_proj=mixer is not None),
        grid=(t_len // (2 * tm),),
        in_specs=[pl.BlockSpec((2 * tm, d), lambda i: (i, 0))] + mixer_specs
                 + [of_layer((1, d)),
                    of_layer((d, 2 * D_FF)),
                    of_layer((3, 2 * D_FF)),
                    of_layer((1, 2 * D_FF)),
                    of_layer((n_j, tf, d)),
                    full((1, d))],
        out_specs=pl.BlockSpec((2 * tm, d), lambda i: (i, 0)),
        out_shape=jax.ShapeDtypeStruct((t_len, d), F32),
        scratch_shapes=[pltpu.VMEM((tm, d), BF16)] * 2
                       + [pltpu.VMEM((tm + SUBLANES, tf), F32)] * 4
                       + [pltpu.VMEM((n_j, SUBLANES, tf), F32),
                        pltpu.VMEM((n_j, SUBLANES, tf), F32)],
        compiler_params=_params(1),
        name="conv_glu_ffn",
    )(x, *mixer_args, norm_g.reshape(depth, 1, d), w_up.astype(BF16), conv_w,
      conv_b.reshape(depth, 1, 2 * D_FF), w_down.astype(BF16).reshape(depth, n_j, tf, d),
      final_g.reshape(1, d))


def even_mixer(x, norm, w_in, ln_g, ln_b, w_s, b_s, mu, w0, w2, a0, a2, g2, k_k, k_a, r_k, gn_g,
               gn_b):
    y_a, p_b = even_in(x, norm, w_in, ln_g, ln_b, w_s, b_s)
    y_b = rwkv_mixer(p_b, mu, w0, w2, a0, a2, g2, k_k, k_a, r_k, gn_g, gn_b)
    return y_a, y_b


def odd_mixer(x, norm, w_qkv, w_out):
    assert all(window // dil == ATTN_BLOCK for window, dil in C_PATTERNS)
    dilations = tuple(dil for _, dil in C_PATTERNS)
    views = norm_matmul_views(x, norm, w_qkv.astype(BF16), dilations)
    outs, lses = [], []
    for view, dil in zip(views, dilations):
        o, l = dilated_attention_pattern(view, dil)
        outs.append(o)
        lses.append(l)
    return attn_combine_proj(x, outs, lses, w_out, dilations)


def kernel(x, ev_norm, ev_w_in, ev_ln_g, ev_ln_b, ev_w_s, ev_b_s, ev_mu, ev_w0, ev_w2, ev_a0, ev_a2, ev_g2, ev_k_k, ev_k_a, ev_r_k, ev_gn_g, ev_gn_b, ev_w_out, od_norm, od_w_qkv, od_w_out, ff_norm, ff_w_up, ff_conv_w, ff_conv_b, ff_w_down, final_norm):
    bsz, t_len, d = x.shape
    depth = ff_norm.shape[0]
    outs = []
    for bi in range(bsz):
        xb = x[bi]
        for layer in range(depth):
            j = layer // 2
            mixer = None
            if layer % 2 == 0:
                y_a, y_b = even_mixer(xb, ev_norm[j], ev_w_in[j], ev_ln_g[j], ev_ln_b[j],
                                      ev_w_s[j], ev_b_s[j], ev_mu[j], ev_w0[j], ev_w2[j], ev_a0[j],
                                      ev_a2[j], ev_g2[j], ev_k_k[j], ev_k_a[j], ev_r_k[j],
                                      ev_gn_g[j], ev_gn_b[j])
                mixer = (y_a, y_b, ev_w_out[j])
            else:
                xb = odd_mixer(xb, od_norm[j], od_w_qkv[j], od_w_out[j])
            xb = conv_glu_ffn(xb, layer, ff_norm, ff_w_up, ff_conv_w, ff_conv_b, ff_w_down,
                              final_norm, final_norm=(layer == depth - 1), mixer=mixer)
        outs.append(xb)
    return jnp.stack(outs, axis=0)
```
